```python
import jax, jax.numpy as jnp
from jax import lax
import numpy as np

D_MODEL = 1024
BATCH = 16
SEQ = 4096
DEPTH = 1
DEC_BATCH = 8
DEC_SEQ = 32
PAST_LEN = 4096

CHUNK = 64
D_MIX = D_MODEL
RET_HEADS = 4
RET_DK = 128
RET_DV = 128
HG_HEADS = 4
HG_DK = 128
HG_DV = 128
RET_QK = RET_HEADS * RET_DK
RET_W = RET_HEADS * RET_DV
HG_QK = HG_HEADS * HG_DK
HG_W = HG_HEADS * HG_DV
IN_SIZES = (RET_QK, RET_QK, RET_W, RET_W, HG_QK, HG_QK, HG_W, HG_W)
D_IN = RET_QK * 2 + RET_W * 2 + HG_QK * 2 + HG_W * 2
N_GROUPS = 4
EXP_PER_GROUP = 4
N_EXPERTS = N_GROUPS * EXP_PER_GROUP
TOP_K = 2
D_EXPERT = 512
ROPE_BASE = 10000.0
EPS = 1e-6

kernel_name = "hymba_retention_hgrn2_hmoe_stream_step"


def rmsnorm(x, g):
    xf = x.astype(jnp.float32)
    y = xf * lax.rsqrt(jnp.mean(xf * xf, axis=-1, keepdims=True) + EPS)
    return (y * g.astype(jnp.float32)).astype(x.dtype)


def head_rmsnorm(o, g):
    H, dv = o.shape[2], o.shape[3]
    y = o * lax.rsqrt(jnp.mean(o * o, axis=-1, keepdims=True) + EPS)
    return y * g.astype(jnp.float32).reshape(H, dv)


def rotary(x, pos):
    half = x.shape[-1] // 2
    inv = 1.0 / (ROPE_BASE ** (jnp.arange(half, dtype=jnp.float32) / half))
    ang = pos.astype(jnp.float32)[:, None] * inv[None, :]
    cos = jnp.cos(ang)[None, :, None, :]
    sin = jnp.sin(ang)[None, :, None, :]
    x1 = x[..., :half].astype(jnp.float32)
    x2 = x[..., half:].astype(jnp.float32)
    return jnp.concatenate([x1 * cos - x2 * sin, x2 * cos + x1 * sin], axis=-1)


def retention_log_decay():
    return jnp.log(1.0 - jnp.exp2(-5.0 - jnp.arange(RET_HEADS, dtype=jnp.float32)))


def retention_chunk(S, q, k, v):
    L = q.shape[1]
    lg = retention_log_decay()
    j = jnp.arange(L, dtype=jnp.float32)
    diff = j[:, None] - j[None, :]
    D = jnp.where(diff[None] >= 0, jnp.exp(jnp.maximum(diff, 0.0)[None] * lg[:, None, None]), 0.0)
    A = jnp.einsum('blhk,bmhk->bhlm', q, k) * D[None]
    intra = jnp.einsum('bhlm,bmhv->blhv', A, v)
    inter_dec = jnp.exp((j[:, None] + 1.0) * lg[None, :])
    inter = jnp.einsum('blhk,bhkv->blhv', q, S) * inter_dec[None, :, :, None]
    k_dec = k * jnp.exp((L - 1.0 - j)[:, None] * lg[None, :])[None, :, :, None]
    S_new = jnp.exp(L * lg)[None, :, None, None] * S + jnp.einsum('blhk,blhv->bhkv', k_dec, v)
    return S_new, intra + inter


def hgrn2_chunk(S, q, k, v, logf):
    L = q.shape[1]
    b = jnp.cumsum(logf, axis=1)
    inter = jnp.einsum('blhk,bhkv->blhv', q * jnp.exp(b), S)
    causal = jnp.tril(jnp.ones((L, L), dtype=bool))
    diff = b[:, :, None] - b[:, None, :]
    dec = jnp.exp(jnp.where(causal[None, :, :, None, None], diff, -jnp.inf))
    A = jnp.einsum('blhk,bmhk,blmhk->bhlm', q, k, dec)
    intra = jnp.einsum('bhlm,bmhv->blhv', A, v)
    bl = b[:, -1]
    k_dec = k * jnp.exp(bl[:, None] - b)
    S_new = jnp.exp(bl)[..., None] * S + jnp.einsum('blhk,blhv->bhkv', k_dec, v)
    return S_new, intra + inter


def run_chunks(step, S0, seqs):
    T = seqs[0].shape[1]
    S0 = S0.astype(jnp.float32)
    if T <= CHUNK:
        return step(S0, *seqs)
    nc = T // CHUNK

    def to_chunks(a):
        return jnp.moveaxis(a.reshape((a.shape[0], nc, CHUNK) + a.shape[2:]), 1, 0)

    S, ys = lax.scan(lambda S, xs: step(S, *xs), S0, tuple(to_chunks(a) for a in seqs))
    ys = jnp.moveaxis(ys, 0, 1)
    return S, ys.reshape((ys.shape[0], T) + ys.shape[3:])


def split_cols(a, sizes):
    out, start = [], 0
    for s in sizes:
        out.append(a[..., start:start + s])
        start += s
    return out


def token_mixer(h, s_ret, s_hg, pos, W_in, ret_norm_g, hg_norm_g, lb, W_out):
    B, T, _ = h.shape
    proj = h @ W_in
    rq, rk, rv, rg, hq, hf, hi, hgt = split_cols(proj, IN_SIZES)
    q_r = rotary(rq.reshape(B, T, RET_HEADS, RET_DK), pos)
    k_r = rotary(rk.reshape(B, T, RET_HEADS, RET_DK), pos) * (RET_DK ** -0.5)
    v_r = rv.reshape(B, T, RET_HEADS, RET_DV)
    s_ret_new, o_r = run_chunks(retention_chunk, s_ret, (q_r, k_r, v_r))
    o_r = head_rmsnorm(o_r, ret_norm_g).reshape(B, T, RET_W) * jax.nn.silu(rg.astype(jnp.float32))
    z = hf.astype(jnp.float32)
    lb = lb.astype(jnp.float32)
    logf = jnp.log(lb + (1.0 - lb) * jax.nn.sigmoid(z)).reshape(B, T, HG_HEADS, HG_DK)
    k_h = ((1.0 - lb) * jax.nn.sigmoid(-z)).reshape(B, T, HG_HEADS, HG_DK)
    q_h = hq.reshape(B, T, HG_HEADS, HG_DK) * (HG_DK ** -0.5)
    v_h = hi.reshape(B, T, HG_HEADS, HG_DV)
    s_hg_new, o_h = run_chunks(hgrn2_chunk, s_hg, (q_h, k_h, v_h, logf))
    o_h = head_rmsnorm(o_h, hg_norm_g).reshape(B, T, HG_W) * jax.nn.silu(hgt.astype(jnp.float32))
    o = jnp.concatenate([o_r, o_h], axis=-1).astype(h.dtype)
    return o @ W_out, s_ret_new, s_hg_new


def hier_moe(h, W_rg, b_rg, W_re, b_re, W_gate_e, W_up_e, W_down_e):
    B, T, D = h.shape
    xt = h.reshape(B * T, D)
    pg = jax.nn.softmax((xt @ W_rg + b_rg).astype(jnp.float32), axis=-1)
    g_w, g_idx = lax.top_k(pg, 1)
    g_w, g_idx = g_w[:, 0], g_idx[:, 0]
    el = (xt @ W_re + b_re).astype(jnp.float32).reshape(-1, N_GROUPS, EXP_PER_GROUP)
    el_sel = jnp.einsum('ng,nge->ne', jax.nn.one_hot(g_idx, N_GROUPS, dtype=jnp.float32), el)
    pe = jax.nn.softmax(el_sel, axis=-1)
    e_w, e_loc = lax.top_k(pe, TOP_K)
    e_w = e_w / jnp.sum(e_w, axis=-1, keepdims=True)
    w = g_w[:, None] * e_w
    e_idx = g_idx[:, None] * EXP_PER_GROUP + e_loc
    combine = jnp.sum(jax.nn.one_hot(e_idx, N_EXPERTS, dtype=jnp.float32) * w[..., None], axis=1)
    combine = combine.astype(xt.dtype)
    y = jnp.zeros_like(xt)
    for e in range(N_EXPERTS):
        hid = jax.nn.silu(xt @ W_gate_e[e]) * (xt @ W_up_e[e])
        y = y + combine[:, e:e + 1] * (hid @ W_down_e[e])
    return y.reshape(B, T, D)


def trunk(x, c, s_ret, s_hg, pos, W_ada, b_ada, norm1_g, norm2_g, W_in, ret_norm_g, hgrn_norm_g,
          hgrn_lb_logits, W_out, W_router_group, b_router_group, W_router_expert, b_router_expert,
          W_gate_e, W_up_e, W_down_e, final_norm_g):
    lb_all = jnp.cumsum(jax.nn.softmax(hgrn_lb_logits.astype(jnp.float32), axis=0), axis=0)
    new_r, new_h = [], []
    for l in range(DEPTH):
        mod = jax.nn.silu(c) @ W_ada[l] + b_ada[l]
        sh1, sc1, gt1, sh2, sc2, gt2 = jnp.split(mod[:, None, :], 6, axis=-1)
        h = rmsnorm(x, norm1_g[l]) * (1 + sc1) + sh1
        mix, sr, shg = token_mixer(h, s_ret[l], s_hg[l], pos, W_in[l], ret_norm_g[l], hgrn_norm_g[l],
                                   lb_all[l], W_out[l])
        x = x + (gt1 * mix).astype(x.dtype)
        h = rmsnorm(x, norm2_g[l]) * (1 + sc2) + sh2
        ff = hier_moe(h, W_router_group[l], b_router_group[l], W_router_expert[l], b_router_expert[l],
                      W_gate_e[l], W_up_e[l], W_down_e[l])
        x = x + (gt2 * ff).astype(x.dtype)
        new_r.append(sr)
        new_h.append(shg)
    return rmsnorm(x, final_norm_g), jnp.stack(new_r), jnp.stack(new_h)


def setup_inputs(seed: int = 0) -> dict:
    key = jax.random.key(seed)
    ks = jax.random.split(key, 24)
    f32 = jnp.float32
    D = D_MODEL
    n = lambda k, shape, s: jax.random.normal(k, shape, f32) * s
    return {
        'x_prompt': n(ks[0], (BATCH, SEQ, D), 1.0),
        'x_sample': n(ks[1], (DEC_BATCH, DEC_SEQ, D), 1.0),
        'c_prompt': n(ks[2], (BATCH, D), 1.0),
        'c_sample': n(ks[3], (DEC_BATCH, D), 1.0),
        'state_ret': n(ks[4], (DEPTH, DEC_BATCH, RET_HEADS, RET_DK, RET_DV), 1.0),
        'state_hgrn': n(ks[5], (DEPTH, DEC_BATCH, HG_HEADS, HG_DK, HG_DV), 1.0),
        'W_ada': n(ks[6], (DEPTH, D, 6 * D), 0.5 * D ** -0.5),
        'b_ada': n(ks[7], (DEPTH, 6 * D), 0.02),
        'norm1_g': 1.0 + n(ks[8], (DEPTH, D), 0.02),
        'norm2_g': 1.0 + n(ks[9], (DEPTH, D), 0.02),
        'W_in': n(ks[10], (DEPTH, D, D_IN), D ** -0.5),
        'ret_norm_g': 1.0 + n(ks[11], (DEPTH, RET_W), 0.02),
        'hgrn_norm_g': 1.0 + n(ks[12], (DEPTH, HG_W), 0.02),
        'hgrn_lb_logits': n(ks[13], (DEPTH + 1, HG_QK), 0.5),
        'W_out': n(ks[14], (DEPTH, D_MIX, D), D_MIX ** -0.5),
        'W_router_group': n(ks[15], (DEPTH, D, N_GROUPS), D ** -0.5),
        'b_router_group': n(ks[16], (DEPTH, N_GROUPS), 0.01),
        'W_router_expert': n(ks[17], (DEPTH, D, N_EXPERTS), D ** -0.5),
        'b_router_expert': n(ks[18], (DEPTH, N_EXPERTS), 0.01),
        'W_gate_e': n(ks[19], (DEPTH, N_EXPERTS, D, D_EXPERT), D ** -0.5),
        'W_up_e': n(ks[20], (DEPTH, N_EXPERTS, D, D_EXPERT), D ** -0.5),
        'W_down_e': n(ks[21], (DEPTH, N_EXPERTS, D_EXPERT, D), D_EXPERT ** -0.5),
        'final_norm_g': 1.0 + n(ks[22], (D,), 0.02),
    }


def reference(x_prompt, x_sample, c_prompt, c_sample, state_ret, state_hgrn, W_ada, b_ada, norm1_g,
              norm2_g, W_in, ret_norm_g, hgrn_norm_g, hgrn_lb_logits, W_out, W_router_group,
              b_router_group, W_router_expert, b_router_expert, W_gate_e, W_up_e, W_down_e,
              final_norm_g):
    Bp, Tp, _ = x_prompt.shape
    Ts = x_sample.shape[1]
    s_ret0 = jnp.zeros((DEPTH, Bp, RET_HEADS, RET_DK, RET_DV), jnp.float32)
    s_hg0 = jnp.zeros((DEPTH, Bp, HG_HEADS, HG_DK, HG_DV), jnp.float32)
    pos_p = jnp.arange(Tp, dtype=jnp.int32)
    y_prompt, state_ret_prompt, state_hgrn_prompt = trunk(
        x_prompt, c_prompt, s_ret0, s_hg0, pos_p, W_ada, b_ada, norm1_g, norm2_g, W_in, ret_norm_g,
        hgrn_norm_g, hgrn_lb_logits, W_out, W_router_group, b_router_group, W_router_expert,
        b_router_expert, W_gate_e, W_up_e, W_down_e, final_norm_g)
    pos_s = PAST_LEN + jnp.arange(Ts, dtype=jnp.int32)
    y_sample, state_ret_sample, state_hgrn_sample = trunk(
        x_sample, c_sample, state_ret, state_hgrn, pos_s, W_ada, b_ada, norm1_g, norm2_g, W_in,
        ret_norm_g, hgrn_norm_g, hgrn_lb_logits, W_out, W_router_group, b_router_group,
        W_router_expert, b_router_expert, W_gate_e, W_up_e, W_down_e, final_norm_g)
    return (y_prompt, y_sample, state_ret_prompt, state_hgrn_prompt, state_ret_sample, state_hgrn_sample)
```

```python
import functools

import numpy as np
import jax
import jax.numpy as jnp
from jax import lax
from jax.experimental import pallas as pl
from jax.experimental.pallas import tpu as pltpu

F32 = jnp.float32
BF16 = jnp.bfloat16

D_MODEL = 1024
HEADS = 4
DH = 128
D_IN = 8 * HEADS * DH
N_GROUPS = 4
EXP_PER_GROUP = 4
N_EXPERTS = N_GROUPS * EXP_PER_GROUP
D_EXPERT = 512
ROPE_BASE = 10000.0
EPS = 1e-6
REF_CHUNK = 64
PAST_LEN = 4096

V7X_VMEM_LIMIT = 56 * 1024 * 1024
ROUTE_LANES = 128


def _dot(a, b):
    return jnp.dot(a, b, preferred_element_type=F32)


def _dot_nt(a, b):
    return lax.dot_general(a, b, (((1,), (1,)), ((), ())), preferred_element_type=F32)


def _dot_tn(a, b):
    return lax.dot_general(a, b, (((0,), (0,)), ((), ())), preferred_element_type=F32)


def _split3(a):
    a1 = a.astype(BF16)
    r1 = a - a1.astype(F32)
    a2 = r1.astype(BF16)
    a3 = (r1 - a2.astype(F32)).astype(BF16)
    return a1, a2, a3


def _silu(a):
    return a * (1.0 / (1.0 + jnp.exp(-a)))


def _tables_kernel(inv_ref, lbl_ref, cos_ref, sin_ref, rd_ref, idec_ref, kdec_ref, lb_ref,
                   lvl_ref, tri_ref, *, T, C, pos0, min_s):
    pos = (lax.broadcasted_iota(jnp.int32, (T, DH), 0) + pos0).astype(F32)
    ang = pos * inv_ref[...]
    lane = lax.broadcasted_iota(jnp.int32, (T, DH), 1)
    cos_ref[...] = jnp.cos(ang)
    s = jnp.sin(ang)
    sin_ref[...] = jnp.where(lane < DH // 2, -s, s)

    li = lax.broadcasted_iota(jnp.int32, (C, C), 0)
    mi = lax.broadcasted_iota(jnp.int32, (C, C), 1)
    diff = (li - mi).astype(F32)
    rowj = lax.broadcasted_iota(jnp.int32, (C, DH), 0).astype(F32)
    for h in range(HEADS):
        lg = jnp.log(jnp.full((1, 1), 1.0 - 2.0 ** (-5.0 - h), F32))
        rd_ref[h] = jnp.where(diff >= 0, jnp.exp(jnp.maximum(diff, 0.0) * lg), 0.0)
        idec_ref[h] = jnp.exp((rowj + 1.0) * lg)
        kdec_ref[h] = jnp.exp((C - 1.0 - rowj) * lg)

    lg_all = lbl_ref[...]
    m = jnp.max(lg_all, axis=0, keepdims=True)
    e = jnp.exp(lg_all - m)
    lb_ref[...] = e[0:1, :] / jnp.sum(e, axis=0, keepdims=True)

    x = li ^ mi
    code = jnp.zeros((C, C), jnp.int32)
    sz = min_s
    while sz < C:
        code = code + (x >= sz).astype(jnp.int32)
        sz *= 2
    code = jnp.where(x >= min_s, code + (min_s.bit_length() - 1), 0)
    lvl_ref[...] = jnp.where(li >= mi, code, -1)
    tri_ref[...] = (li >= mi).astype(BF16)


def _tables(inv2, lb_logits, *, T, C, pos0, min_s):
    out_shape = (
        jax.ShapeDtypeStruct((T, DH), F32),
        jax.ShapeDtypeStruct((T, DH), F32),
        jax.ShapeDtypeStruct((HEADS, C, C), F32),
        jax.ShapeDtypeStruct((HEADS, C, DH), F32),
        jax.ShapeDtypeStruct((HEADS, C, DH), F32),
        jax.ShapeDtypeStruct((1, HEADS * DH), F32),
        jax.ShapeDtypeStruct((C, C), jnp.int32),
        jax.ShapeDtypeStruct((C, C), BF16),
    )
    return pl.pallas_call(
        functools.partial(_tables_kernel, T=T, C=C, pos0=pos0, min_s=min_s),
        out_shape=out_shape,
        name="tables",
    )(inv2, lb_logits)


def _ada_kernel(c_ref, w_ref, b_ref, o_ref):
    a = _silu(c_ref[...])
    w = w_ref[...]
    a1, a2, a3 = _split3(a)
    w1, w2, w3 = _split3(w)
    acc = _dot(a1, w1) + (_dot(a1, w2) + _dot(a2, w1)) + (_dot(a1, w3) + _dot(a2, w2) + _dot(a3, w1))
    o_ref[...] = acc + b_ref[...]


def _ada(c, w, b):
    R = c.shape[0]
    N = w.shape[1]
    TN = 512
    return pl.pallas_call(
        _ada_kernel,
        grid=(N // TN,),
        in_specs=[pl.BlockSpec((R, D_MODEL), lambda j: (0, 0)),
                  pl.BlockSpec((D_MODEL, TN), lambda j: (0, j)),
                  pl.BlockSpec((1, TN), lambda j: (0, j))],
        out_specs=pl.BlockSpec((R, TN), lambda j: (0, j)),
        out_shape=jax.ShapeDtypeStruct((R, N), F32),
        name="ada",
    )(c, w, b)


def _block_ref(b, block, idx):
    C, W = b.shape
    if block >= 8:
        b3 = b.reshape(C // block, block, W)
        r = jnp.broadcast_to(b3[:, idx:idx + 1, :], b3.shape)
        return r.reshape(C, W)
    row = lax.broadcasted_iota(jnp.int32, (C, W), 0)
    p = row % block
    out = b
    for q in range(block):
        sh = q - idx
        if sh == 0:
            continue
        rolled = pltpu.roll(b, sh % C, 0)
        out = jnp.where(p == q, rolled, out)
    return out


def _head_norm_gate(o, g_row, gate):
    y = o * lax.rsqrt(jnp.mean(o * o, axis=-1, keepdims=True) + EPS)
    return (y * g_row) * _silu(gate)


def _mixer_kernel(x_ref, mod_ref, n1g_ref, win_ref, wout_ref, cos_ref, sin_ref, rd_ref,
                  idec_ref, kdec_ref, lb_ref, lvl_ref, tri_ref, retg_ref, hgg_ref,
                  sr0_ref, sh0_ref,
                  x1_ref, sro_ref, sho_ref,
                  sr_scr, sht_scr, o_scr, *, C, min_s):
    t = pl.program_id(1)
    nt = pl.num_programs(1)
    W = HEADS * DH

    @pl.when(t == 0)
    def _():
        for h in range(HEADS):
            sr_scr[h] = sr0_ref[0, h]
            sht_scr[h] = sh0_ref[0, h].T

    x = x_ref[0]
    mod = mod_ref[0]
    sh1 = mod[:, 0:D_MODEL]
    sc1 = mod[:, D_MODEL:2 * D_MODEL]
    gt1 = mod[:, 2 * D_MODEL:3 * D_MODEL]
    xn = x * lax.rsqrt(jnp.mean(x * x, axis=-1, keepdims=True) + EPS) * n1g_ref[...]
    hmod = xn * (1.0 + sc1) + sh1
    proj = _dot(hmod.astype(BF16), win_ref[...])

    cosf = cos_ref[...]
    sinf = sin_ref[...]

    for h in range(HEADS):
        c0 = h * DH
        q = proj[:, c0:c0 + DH]
        k = proj[:, W + c0:W + c0 + DH]
        v = proj[:, 2 * W + c0:2 * W + c0 + DH].astype(BF16)
        g = proj[:, 3 * W + c0:3 * W + c0 + DH]
        qr = (q * cosf + pltpu.roll(q, DH // 2, 1) * sinf)
        kr = (k * cosf + pltpu.roll(k, DH // 2, 1) * sinf) * (DH ** -0.5)
        qb = qr.astype(BF16)
        A = _dot_nt(qb, kr.astype(BF16)) * rd_ref[h]
        S = sr_scr[h]
        idec = idec_ref[h]
        o = _dot(A.astype(BF16), v) + _dot(qb, S.astype(BF16)) * idec
        sr_scr[h] = idec[C - 1:C, :] * S + _dot_tn((kr * kdec_ref[h]).astype(BF16), v)
        o_scr[:, c0:c0 + DH] = _head_norm_gate(o, retg_ref[:, c0:c0 + DH], g).astype(BF16)

    z = proj[:, 5 * W:6 * W]
    lb = lb_ref[...]
    e = jnp.exp(-jnp.abs(z))
    r = 1.0 / (1.0 + e)
    er = e * r
    sig = jnp.where(z >= 0, r, er)
    nsig = jnp.where(z >= 0, er, r)
    logf = jnp.log(lb + (1.0 - lb) * sig)
    kk_all = (1.0 - lb) * nsig
    tri = tri_ref[...]
    l1, l2, l3 = _split3(logf)
    b_all = _dot(tri, l1) + _dot(tri, l2) + _dot(tri, l3)

    lvl = lvl_ref[...]
    rowc = lax.broadcasted_iota(jnp.int32, (C, DH), 0)
    for h in range(HEADS):
        c0 = h * DH
        q = proj[:, 4 * W + c0:4 * W + c0 + DH] * (DH ** -0.5)
        kk = kk_all[:, c0:c0 + DH]
        b = b_all[:, c0:c0 + DH]
        v = proj[:, 6 * W + c0:6 * W + c0 + DH]
        vb = v.astype(BF16)
        g = proj[:, 7 * W + c0:7 * W + c0 + DH]

        A = jnp.zeros((C, C), F32)
        s = C // 2
        while s >= min_s:
            isq = (rowc % (2 * s)) >= s
            ref = _block_ref(b, 2 * s, s - 1)
            d = b - ref
            zz = jnp.where(isq, q, kk) * jnp.exp(jnp.where(isq, d, -d))
            zb = zz.astype(BF16)
            A = jnp.where(lvl == s.bit_length(), _dot_nt(zb, zb), A)
            s //= 2
        if min_s > 1:
            ref = _block_ref(b, min_s, 0)
            d = b - ref
            P = _dot_nt((q * jnp.exp(d)).astype(BF16), (kk * jnp.exp(-d)).astype(BF16))
            A = jnp.where(lvl == 0, P, A)
            o = _dot(A.astype(BF16), vb)
        else:
            o = _dot(A.astype(BF16), vb) + jnp.sum(q * kk, axis=-1, keepdims=True) * v

        ST = sht_scr[h]
        bl = b[C - 1:C, :]
        o = o + _dot_nt((q * jnp.exp(b)).astype(BF16), ST.astype(BF16))
        kdec = kk * jnp.exp(bl - b)
        sht_scr[h] = ST * jnp.exp(bl) + _dot_tn(vb, kdec.astype(BF16))
        o_scr[:, W + c0:W + c0 + DH] = _head_norm_gate(o, hgg_ref[:, c0:c0 + DH], g).astype(BF16)

    mix = _dot(o_scr[...], wout_ref[...])
    x1_ref[0] = x + gt1 * mix

    @pl.when(t == nt - 1)
    def _():
        for h in range(HEADS):
            sro_ref[0, h] = sr_scr[h]
            sho_ref[0, h] = sht_scr[h].T


def _mixer(x, mod, n1g, win, wout, tabs, retg, hgg, sr0, sh0, *, C, min_s):
    B, T, D = x.shape
    cosf, sinf, rd, idec, kdec, lb, lvl, tri = tabs
    nt = T // C
    const2 = lambda b, t: (0, 0)
    const3 = lambda b, t: (0, 0, 0)
    state_spec = pl.BlockSpec((1, HEADS, DH, DH), lambda b, t: (b, 0, 0, 0))
    in_specs = [
        pl.BlockSpec((1, C, D), lambda b, t: (b, t, 0)),
        pl.BlockSpec((1, 1, 6 * D), lambda b, t: (b, 0, 0)),
        pl.BlockSpec((1, D), const2),
        pl.BlockSpec((D, D_IN), const2),
        pl.BlockSpec((D, D), const2),
        pl.BlockSpec((C, DH), lambda b, t: (t, 0)),
        pl.BlockSpec((C, DH), lambda b, t: (t, 0)),
        pl.BlockSpec((HEADS, C, C), const3),
        pl.BlockSpec((HEADS, C, DH), const3),
        pl.BlockSpec((HEADS, C, DH), const3),
        pl.BlockSpec((1, HEADS * DH), const2),
        pl.BlockSpec((C, C), const2),
        pl.BlockSpec((C, C), const2),
        pl.BlockSpec((1, HEADS * DH), const2),
        pl.BlockSpec((1, HEADS * DH), const2),
        state_spec, state_spec,
    ]
    out_specs = [pl.BlockSpec((1, C, D), lambda b, t: (b, t, 0)), state_spec, state_spec]
    out_shape = [jax.ShapeDtypeStruct((B, T, D), F32),
                 jax.ShapeDtypeStruct((B, HEADS, DH, DH), F32),
                 jax.ShapeDtypeStruct((B, HEADS, DH, DH), F32)]
    return pl.pallas_call(
        functools.partial(_mixer_kernel, C=C, min_s=min_s),
        grid=(B, nt),
        in_specs=in_specs,
        out_specs=out_specs,
        out_shape=out_shape,
        scratch_shapes=[pltpu.VMEM((HEADS, DH, DH), F32),
                        pltpu.VMEM((HEADS, DH, DH), F32),
                        pltpu.VMEM((C, D), BF16)],
        compiler_params=pltpu.CompilerParams(
            dimension_semantics=("arbitrary", "arbitrary"),
            vmem_limit_bytes=V7X_VMEM_LIMIT),
        name="mixer",
    )(x, mod, n1g, win, wout, cosf, sinf, rd, idec, kdec, lb, lvl, tri, retg, hgg, sr0, sh0)


def _route(logits_t):
    TM = logits_t.shape[1]
    lg = logits_t[0:N_GROUPS, :]
    gi = lax.broadcasted_iota(jnp.int32, (N_GROUPS, TM), 0).astype(F32)
    m = jnp.max(lg, axis=0, keepdims=True)
    g_w = 1.0 / jnp.sum(jnp.exp(lg - m), axis=0, keepdims=True)
    g_idx = jnp.min(jnp.where(lg == m, gi, float(N_GROUPS)), axis=0, keepdims=True)
    el = jnp.zeros((EXP_PER_GROUP, TM), F32)
    for g in range(N_GROUPS):
        lo = N_GROUPS + g * EXP_PER_GROUP
        el = jnp.where(g_idx == float(g), logits_t[lo:lo + EXP_PER_GROUP, :], el)
    ei = lax.broadcasted_iota(jnp.int32, (EXP_PER_GROUP, TM), 0).astype(F32)
    m1 = jnp.max(el, axis=0, keepdims=True)
    i1 = jnp.min(jnp.where(el == m1, ei, float(EXP_PER_GROUP)), axis=0, keepdims=True)
    el2 = jnp.where(ei == i1, -jnp.inf, el)
    m2 = jnp.max(el2, axis=0, keepdims=True)
    i2 = jnp.min(jnp.where(el2 == m2, ei, float(EXP_PER_GROUP)), axis=0, keepdims=True)
    p2 = jnp.exp(m2 - m1)
    den = 1.0 / (1.0 + p2)
    w0 = g_w * den
    w1 = g_w * (p2 * den)
    e0 = g_idx * EXP_PER_GROUP + i1
    e1 = g_idx * EXP_PER_GROUP + i2
    return e0, e1, w0, w1


def _moe_kernel(x1_ref, mod_ref, n2g_ref, wr_ref, br_ref, wg_ref, wu_ref, wd_ref, fg_ref,
                y_ref, h_scr, route_scr, acc_scr):
    e = pl.program_id(1)
    ne = pl.num_programs(1)
    TM, D = x1_ref.shape
    K = mod_ref.shape[0]
    TQ = TM // K

    @pl.when(e == 0)
    def _():
        x = x1_ref[...]
        mod = mod_ref[...]
        sh2 = mod[:, :, 3 * D:4 * D]
        sc2 = mod[:, :, 4 * D:5 * D]
        xn = x * lax.rsqrt(jnp.mean(x * x, axis=-1, keepdims=True) + EPS) * n2g_ref[...]
        h = (xn.reshape(K, TQ, D) * (1.0 + sc2) + sh2).reshape(TM, D)
        h1, h2, _ = _split3(h)
        h_scr[...] = h1
        w1 = wr_ref[0]
        w2 = wr_ref[1]
        logits = _dot(h1, w1) + (_dot(h1, w2) + _dot(h2, w1)) + br_ref[...]
        e0, e1, w0, wgt1 = _route(logits.T)
        ri = lax.broadcasted_iota(jnp.int32, (ROUTE_LANES, TM), 0)
        rows = jnp.where(ri == 0, e0, jnp.where(ri == 1, e1, jnp.where(ri == 2, w0,
                         jnp.where(ri == 3, wgt1, 0.0))))
        route_scr[...] = rows.T
        acc_scr[...] = jnp.zeros_like(acc_scr)

    hb = h_scr[...]
    hid = _silu(_dot(hb, wg_ref[0])) * _dot(hb, wu_ref[0])
    out = _dot(hid.astype(BF16), wd_ref[0])
    ef = e.astype(F32)
    route = route_scr[...]
    comb = (jnp.where(route[:, 0:1] == ef, route[:, 2:3], 0.0)
            + jnp.where(route[:, 1:2] == ef, route[:, 3:4], 0.0))
    acc_scr[...] += comb * out

    @pl.when(e == ne - 1)
    def _():
        gt2 = mod_ref[...][:, :, 5 * D:6 * D]
        ff = (gt2 * acc_scr[...].reshape(K, TQ, D)).reshape(TM, D)
        x2 = x1_ref[...] + ff
        y_ref[...] = x2 * lax.rsqrt(jnp.mean(x2 * x2, axis=-1, keepdims=True) + EPS) * fg_ref[...]


def _moe(x1, mod, n2g, wr, br, wg, wu, wd, fg, *, T, TM):
    N, D = x1.shape
    if T >= TM:
        assert T % TM == 0
        seqs, tiles_per_seq = 1, T // TM
        mod_map = lambda i, e: (i // tiles_per_seq, 0, 0)
    else:
        assert TM % T == 0
        seqs = TM // T
        mod_map = lambda i, e: (i, 0, 0)
    return pl.pallas_call(
        _moe_kernel,
        grid=(N // TM, N_EXPERTS),
        in_specs=[
            pl.BlockSpec((TM, D), lambda i, e: (i, 0)),
            pl.BlockSpec((seqs, 1, 6 * D), mod_map),
            pl.BlockSpec((1, D), lambda i, e: (0, 0)),
            pl.BlockSpec((2, D, ROUTE_LANES), lambda i, e: (0, 0, 0)),
            pl.BlockSpec((1, ROUTE_LANES), lambda i, e: (0, 0)),
            pl.BlockSpec((1, D, D_EXPERT), lambda i, e: (e, 0, 0)),
            pl.BlockSpec((1, D, D_EXPERT), lambda i, e: (e, 0, 0)),
            pl.BlockSpec((1, D_EXPERT, D), lambda i, e: (e, 0, 0)),
            pl.BlockSpec((1, D), lambda i, e: (0, 0)),
        ],
        out_specs=pl.BlockSpec((TM, D), lambda i, e: (i, 0)),
        out_shape=jax.ShapeDtypeStruct((N, D), F32),
        scratch_shapes=[pltpu.VMEM((TM, D), BF16),
                        pltpu.VMEM((TM, ROUTE_LANES), F32),
                        pltpu.VMEM((TM, D), F32)],
        compiler_params=pltpu.CompilerParams(
            dimension_semantics=("arbitrary", "arbitrary"),
            vmem_limit_bytes=V7X_VMEM_LIMIT),
        name="moe",
    )(x1, mod, n2g, wr, br, wg, wu, wd, fg)


def _chunk_len(T):
    return T if T <= REF_CHUNK else 128


def _trunk(x, mod, s_ret, s_hg, pos0, inv2, p, *, min_s=8):
    B, T, D = x.shape
    C = _chunk_len(T)
    min_s = min(min_s, C // 2)
    tabs = _tables(inv2, p["lb_logits"], T=T, C=C, pos0=pos0, min_s=min_s)
    x1, sr, sh = _mixer(x, mod.reshape(B, 1, 6 * D), p["n1g"], p["win"], p["wout"], tabs,
                        p["retg"], p["hgg"], s_ret, s_hg, C=C, min_s=min_s)
    TM = min(B * T, 1024)
    y = _moe(x1.reshape(B * T, D), mod.reshape(B, 1, 6 * D), p["n2g"], p["wr"], p["br"],
             p["wg"], p["wu"], p["wd"], p["fg"], T=T, TM=TM)
    return y.reshape(B, T, D), sr[None], sh[None]


def _prepare(W_ada, b_ada, norm1_g, norm2_g, W_in, ret_norm_g, hgrn_norm_g, hgrn_lb_logits, W_out,
             W_router_group, b_router_group, W_router_expert, b_router_expert, W_gate_e, W_up_e,
             W_down_e, final_norm_g):
    assert W_in.shape[0] == 1, "single-layer stack only"
    D = D_MODEL
    wr = jnp.concatenate([W_router_group[0], W_router_expert[0]], axis=1)
    wr = jnp.pad(wr, ((0, 0), (0, ROUTE_LANES - wr.shape[1])))
    wr1 = wr.astype(BF16)
    wr2 = (wr - wr1.astype(F32)).astype(BF16)
    br = jnp.concatenate([b_router_group[0], b_router_expert[0]])
    br = jnp.pad(br, (0, ROUTE_LANES - br.shape[0])).reshape(1, ROUTE_LANES)
    return dict(
        n1g=norm1_g[0].reshape(1, D), n2g=norm2_g[0].reshape(1, D), fg=final_norm_g.reshape(1, D),
        win=W_in[0].astype(BF16), wout=W_out[0].astype(BF16),
        retg=ret_norm_g[0].reshape(1, -1), hgg=hgrn_norm_g[0].reshape(1, -1),
        lb_logits=hgrn_lb_logits,
        wr=jnp.stack([wr1, wr2]), br=br,
        wg=W_gate_e[0].astype(BF16), wu=W_up_e[0].astype(BF16), wd=W_down_e[0].astype(BF16),
    )


def _rope_inv():
    half = DH // 2
    inv = 1.0 / (ROPE_BASE ** (jnp.arange(half, dtype=F32) / half))
    return jnp.concatenate([inv, inv]).reshape(1, DH)


def kernel(x_prompt, x_sample, c_prompt, c_sample, state_ret, state_hgrn, W_ada, b_ada, norm1_g,
           norm2_g, W_in, ret_norm_g, hgrn_norm_g, hgrn_lb_logits, W_out, W_router_group,
           b_router_group, W_router_expert, b_router_expert, W_gate_e, W_up_e, W_down_e,
           final_norm_g):
    p = _prepare(W_ada, b_ada, norm1_g, norm2_g, W_in, ret_norm_g, hgrn_norm_g, hgrn_lb_logits,
                 W_out, W_router_group, b_router_group, W_router_expert, b_router_expert,
                 W_gate_e, W_up_e, W_down_e, final_norm_g)
    Bp = x_prompt.shape[0]
    Bs = x_sample.shape[0]
    inv2 = _rope_inv()
    mod = _ada(jnp.concatenate([c_prompt, c_sample], axis=0), W_ada[0], b_ada[0].reshape(1, -1))
    zeros = jnp.zeros((Bp, HEADS, DH, DH), F32)
    y_p, sr_p, sh_p = _trunk(x_prompt, mod[:Bp], zeros, zeros, 0, inv2, p)
    y_s, sr_s, sh_s = _trunk(x_sample, mod[Bp:Bp + Bs], state_ret[0], state_hgrn[0], PAST_LEN,
                             inv2, p)
    return (y_p, y_s, sr_p, sh_p, sr_s, sh_s)
```

```python
import functools

import numpy as np
import jax
import jax.numpy as jnp
from jax import lax
from jax.experimental import pallas as pl
from jax.experimental.pallas import tpu as pltpu

F32 = jnp.float32
BF16 = jnp.bfloat16

D_MODEL = 1024
HEADS = 4
DH = 128
D_IN = 8 * HEADS * DH
N_GROUPS = 4
EXP_PER_GROUP = 4
N_EXPERTS = N_GROUPS * EXP_PER_GROUP
D_EXPERT = 512
ROPE_BASE = 10000.0
EPS = 1e-6
REF_CHUNK = 64
PAST_LEN = 4096

V7X_VMEM_LIMIT = 56 * 1024 * 1024
ROUTE_LANES = 128

_PAIRS = [(i, j) for i in range(EXP_PER_GROUP) for j in range(i + 1, EXP_PER_GROUP)]
N_PAIRS = len(_PAIRS)
N_CLASSES = N_GROUPS * N_PAIRS
CLASS_EXPERT_A = np.array([g * EXP_PER_GROUP + i for g in range(N_GROUPS) for i, _ in _PAIRS]
                          + [N_EXPERTS - 2], np.int32)
CLASS_EXPERT_B = np.array([g * EXP_PER_GROUP + j for g in range(N_GROUPS) for _, j in _PAIRS]
                          + [N_EXPERTS - 1], np.int32)
MIXER_CHUNK = 128
MIXER_TILE = 512
ROUTED_MIN_TOKENS = 8192
ROUTED_TILE = 512
DISPATCH_TILE = 2048
COMBINE_TILE = 512


def _dot(a, b):
    return jnp.dot(a, b, preferred_element_type=F32)


def _dot_nt(a, b):
    return lax.dot_general(a, b, (((1,), (1,)), ((), ())), preferred_element_type=F32)


def _dot_tn(a, b):
    return lax.dot_general(a, b, (((0,), (0,)), ((), ())), preferred_element_type=F32)


def _split3(a):
    a1 = a.astype(BF16)
    r1 = a - a1.astype(F32)
    a2 = r1.astype(BF16)
    a3 = (r1 - a2.astype(F32)).astype(BF16)
    return a1, a2, a3


def _silu(a):
    return a * (1.0 / (1.0 + jnp.exp(-a)))


def _tables_kernel(inv_ref, lbl_ref, cos_ref, sin_ref, rd_ref, idec_ref, kdec_ref, lb_ref,
                   lvl_ref, tri_ref, *, T, C, pos0, min_s):
    pos = (lax.broadcasted_iota(jnp.int32, (T, DH), 0) + pos0).astype(F32)
    ang = pos * inv_ref[...]
    lane = lax.broadcasted_iota(jnp.int32, (T, DH), 1)
    cos_ref[...] = jnp.cos(ang)
    s = jnp.sin(ang)
    sin_ref[...] = jnp.where(lane < DH // 2, -s, s)

    li = lax.broadcasted_iota(jnp.int32, (C, C), 0)
    mi = lax.broadcasted_iota(jnp.int32, (C, C), 1)
    diff = (li - mi).astype(F32)
    rowj = lax.broadcasted_iota(jnp.int32, (C, DH), 0).astype(F32)
    for h in range(HEADS):
        lg = jnp.log(jnp.full((1, 1), 1.0 - 2.0 ** (-5.0 - h), F32))
        rd_ref[h] = jnp.where(diff >= 0, jnp.exp(jnp.maximum(diff, 0.0) * lg), 0.0)
        idec_ref[h] = jnp.exp((rowj + 1.0) * lg)
        kdec_ref[h] = jnp.exp((C - 1.0 - rowj) * lg)

    lg_all = lbl_ref[...]
    m = jnp.max(lg_all, axis=0, keepdims=True)
    e = jnp.exp(lg_all - m)
    lb_ref[...] = e[0:1, :] / jnp.sum(e, axis=0, keepdims=True)

    x = li ^ mi
    code = jnp.zeros((C, C), jnp.int32)
    sz = min_s
    while sz < C:
        code = code + (x >= sz).astype(jnp.int32)
        sz *= 2
    code = jnp.where(x >= min_s, code + (min_s.bit_length() - 1), 0)
    lvl_ref[...] = jnp.where(li >= mi, code, -1)
    tri_ref[...] = (li >= mi).astype(BF16)


def _tables(inv2, lb_logits, *, T, C, pos0, min_s):
    out_shape = (
        jax.ShapeDtypeStruct((T, DH), F32),
        jax.ShapeDtypeStruct((T, DH), F32),
        jax.ShapeDtypeStruct((HEADS, C, C), F32),
        jax.ShapeDtypeStruct((HEADS, C, DH), F32),
        jax.ShapeDtypeStruct((HEADS, C, DH), F32),
        jax.ShapeDtypeStruct((1, HEADS * DH), F32),
        jax.ShapeDtypeStruct((C, C), jnp.int32),
        jax.ShapeDtypeStruct((C, C), BF16),
    )
    return pl.pallas_call(
        functools.partial(_tables_kernel, T=T, C=C, pos0=pos0, min_s=min_s),
        out_shape=out_shape,
        name="tables",
    )(inv2, lb_logits)


def _ada_kernel(c_ref, w_ref, b_ref, o_ref):
    a = _silu(c_ref[...])
    w = w_ref[...]
    a1, a2, a3 = _split3(a)
    w1, w2, w3 = _split3(w)
    acc = _dot(a1, w1) + (_dot(a1, w2) + _dot(a2, w1)) + (_dot(a1, w3) + _dot(a2, w2) + _dot(a3, w1))
    o_ref[...] = acc + b_ref[...]


def _ada(c, w, b):
    R = c.shape[0]
    N = w.shape[1]
    TN = 512
    return pl.pallas_call(
        _ada_kernel,
        grid=(N // TN,),
        in_specs=[pl.BlockSpec((R, D_MODEL), lambda j: (0, 0)),
                  pl.BlockSpec((D_MODEL, TN), lambda j: (0, j)),
                  pl.BlockSpec((1, TN), lambda j: (0, j))],
        out_specs=pl.BlockSpec((R, TN), lambda j: (0, j)),
        out_shape=jax.ShapeDtypeStruct((R, N), F32),
        name="ada",
    )(c, w, b)


def _block_ref(b, block, idx):
    C, W = b.shape
    if block >= 8:
        b3 = b.reshape(C // block, block, W)
        r = jnp.broadcast_to(b3[:, idx:idx + 1, :], b3.shape)
        return r.reshape(C, W)
    row = lax.broadcasted_iota(jnp.int32, (C, W), 0)
    p = row % block
    out = b
    for q in range(block):
        sh = q - idx
        if sh == 0:
            continue
        rolled = pltpu.roll(b, sh % C, 0)
        out = jnp.where(p == q, rolled, out)
    return out


def _head_norm_gate(o, g_row, gate):
    y = o * lax.rsqrt(jnp.mean(o * o, axis=-1, keepdims=True) + EPS)
    return (y * g_row) * _silu(gate)


def _mixer_chunk(r0, proj_scr, cos_ref, sin_ref, rd_ref, idec_ref, kdec_ref, lb_ref, lvl_ref,
                 tri_ref, retg_ref, hgg_ref, sr_scr, sht_scr, o_scr, *, C, min_s):
    W = HEADS * DH
    rows = pl.ds(r0, C)
    proj = proj_scr[rows, :]
    cosf = cos_ref[rows, :]
    sinf = sin_ref[rows, :]

    for h in range(HEADS):
        c0 = h * DH
        q = proj[:, c0:c0 + DH]
        k = proj[:, W + c0:W + c0 + DH]
        v = proj[:, 2 * W + c0:2 * W + c0 + DH].astype(BF16)
        g = proj[:, 3 * W + c0:3 * W + c0 + DH]
        qr = (q * cosf + pltpu.roll(q, DH // 2, 1) * sinf)
        kr = (k * cosf + pltpu.roll(k, DH // 2, 1) * sinf) * (DH ** -0.5)
        qb = qr.astype(BF16)
        A = _dot_nt(qb, kr.astype(BF16)) * rd_ref[h]
        S = sr_scr[h]
        idec = idec_ref[h]
        o = _dot(A.astype(BF16), v) + _dot(qb, S.astype(BF16)) * idec
        sr_scr[h] = idec[C - 1:C, :] * S + _dot_tn((kr * kdec_ref[h]).astype(BF16), v)
        o_scr[rows, c0:c0 + DH] = _head_norm_gate(o, retg_ref[:, c0:c0 + DH], g).astype(BF16)

    z = proj[:, 5 * W:6 * W]
    lb = lb_ref[...]
    e = jnp.exp(-jnp.abs(z))
    r = 1.0 / (1.0 + e)
    er = e * r
    sig = jnp.where(z >= 0, r, er)
    nsig = jnp.where(z >= 0, er, r)
    logf = jnp.log(lb + (1.0 - lb) * sig)
    kk_all = (1.0 - lb) * nsig
    tri = tri_ref[...]
    l1, l2, l3 = _split3(logf)
    b_all = _dot(tri, l1) + _dot(tri, l2) + _dot(tri, l3)

    lvl = lvl_ref[...]
    rowc = lax.broadcasted_iota(jnp.int32, (C, DH), 0)
    for h in range(HEADS):
        c0 = h * DH
        q = proj[:, 4 * W + c0:4 * W + c0 + DH] * (DH ** -0.5)
        kk = kk_all[:, c0:c0 + DH]
        b = b_all[:, c0:c0 + DH]
        v = proj[:, 6 * W + c0:6 * W + c0 + DH]
        vb = v.astype(BF16)
        g = proj[:, 7 * W + c0:7 * W + c0 + DH]

        A = jnp.zeros((C, C), F32)
        s = C // 2
        while s >= min_s:
            isq = (rowc % (2 * s)) >= s
            ref = _block_ref(b, 2 * s, s - 1)
            d = b - ref
            zz = jnp.where(isq, q, kk) * jnp.exp(jnp.where(isq, d, -d))
            zb = zz.astype(BF16)
            A = jnp.where(lvl == s.bit_length(), _dot_nt(zb, zb), A)
            s //= 2
        if min_s > 1:
            ref = _block_ref(b, min_s, 0)
            d = b - ref
            P = _dot_nt((q * jnp.exp(d)).astype(BF16), (kk * jnp.exp(-d)).astype(BF16))
            A = jnp.where(lvl == 0, P, A)
            o = _dot(A.astype(BF16), vb)
        else:
            o = _dot(A.astype(BF16), vb) + jnp.sum(q * kk, axis=-1, keepdims=True) * v

        ST = sht_scr[h]
        bl = b[C - 1:C, :]
        o = o + _dot_nt((q * jnp.exp(b)).astype(BF16), ST.astype(BF16))
        kdec = kk * jnp.exp(bl - b)
        sht_scr[h] = ST * jnp.exp(bl) + _dot_tn(vb, kdec.astype(BF16))
        o_scr[rows, W + c0:W + c0 + DH] = _head_norm_gate(o, hgg_ref[:, c0:c0 + DH], g).astype(BF16)


def _mixer_kernel(x_ref, mod_ref, n1g_ref, win_ref, wout_ref, cos_ref, sin_ref, rd_ref,
                  idec_ref, kdec_ref, lb_ref, lvl_ref, tri_ref, retg_ref, hgg_ref,
                  sr0_ref, sh0_ref, n2g_ref, wr_ref, br_ref,
                  x1_ref, h2_ref, cls_ref, sro_ref, sho_ref,
                  sr_scr, sht_scr, proj_scr, o_scr, *, C, min_s, with_cls):
    t = pl.program_id(1)
    nt = pl.num_programs(1)
    TT = x_ref.shape[1]

    @pl.when(t == 0)
    def _():
        for h in range(HEADS):
            sr_scr[h] = sr0_ref[0, h]
            sht_scr[h] = sh0_ref[0, h].T

    x = x_ref[0]
    mod = mod_ref[0]
    sh1 = mod[:, 0:D_MODEL]
    sc1 = mod[:, D_MODEL:2 * D_MODEL]
    gt1 = mod[:, 2 * D_MODEL:3 * D_MODEL]
    xn = x * lax.rsqrt(jnp.mean(x * x, axis=-1, keepdims=True) + EPS) * n1g_ref[...]
    hmod = xn * (1.0 + sc1) + sh1
    proj_scr[...] = _dot(hmod.astype(BF16), win_ref[...])

    chunk = functools.partial(
        _mixer_chunk, proj_scr=proj_scr, cos_ref=cos_ref, sin_ref=sin_ref, rd_ref=rd_ref,
        idec_ref=idec_ref, kdec_ref=kdec_ref, lb_ref=lb_ref, lvl_ref=lvl_ref, tri_ref=tri_ref,
        retg_ref=retg_ref, hgg_ref=hgg_ref, sr_scr=sr_scr, sht_scr=sht_scr, o_scr=o_scr,
        C=C, min_s=min_s)
    if TT == C:
        chunk(0)
    else:
        def body(c, carry):
            chunk(pl.multiple_of(c * C, C))
            return carry
        lax.fori_loop(0, TT // C, body, 0)

    mix = _dot(o_scr[...], wout_ref[...])
    x1 = x + gt1 * mix
    x1_ref[0] = x1

    sh2 = mod[:, 3 * D_MODEL:4 * D_MODEL]
    sc2 = mod[:, 4 * D_MODEL:5 * D_MODEL]
    x1n = x1 * lax.rsqrt(jnp.mean(x1 * x1, axis=-1, keepdims=True) + EPS) * n2g_ref[...]
    h2 = x1n * (1.0 + sc2) + sh2
    h2_ref[0] = h2
    if with_cls:
        g_idx, i1, i2 = _route_topk(_router_logits(h2, wr_ref, br_ref).T)[:3]
        lo = jnp.minimum(i1, i2)
        hi = jnp.maximum(i1, i2)
        pair = hi - 1.0 + jnp.where(lo == 1.0, 2.0, 0.0) + jnp.where(lo == 2.0, 3.0, 0.0)
        cls_ref[0] = (g_idx * float(N_PAIRS) + pair).astype(jnp.int32)
    else:
        cls_ref[0] = jnp.zeros((1, TT), jnp.int32)

    @pl.when(t == nt - 1)
    def _():
        for h in range(HEADS):
            sro_ref[0, h] = sr_scr[h]
            sho_ref[0, h] = sht_scr[h].T


def _mixer(x, mod, n1g, win, wout, tabs, retg, hgg, sr0, sh0, n2g, wr, br, *, C, TT, min_s,
           with_cls):
    B, T, D = x.shape
    cosf, sinf, rd, idec, kdec, lb, lvl, tri = tabs
    nt = T // TT
    const2 = lambda b, t: (0, 0)
    const3 = lambda b, t: (0, 0, 0)
    state_spec = pl.BlockSpec((1, HEADS, DH, DH), lambda b, t: (b, 0, 0, 0))
    in_specs = [
        pl.BlockSpec((1, TT, D), lambda b, t: (b, t, 0)),
        pl.BlockSpec((1, 1, 6 * D), lambda b, t: (b, 0, 0)),
        pl.BlockSpec((1, D), const2),
        pl.BlockSpec((D, D_IN), const2),
        pl.BlockSpec((D, D), const2),
        pl.BlockSpec((TT, DH), lambda b, t: (t, 0)),
        pl.BlockSpec((TT, DH), lambda b, t: (t, 0)),
        pl.BlockSpec((HEADS, C, C), const3),
        pl.BlockSpec((HEADS, C, DH), const3),
        pl.BlockSpec((HEADS, C, DH), const3),
        pl.BlockSpec((1, HEADS * DH), const2),
        pl.BlockSpec((C, C), const2),
        pl.BlockSpec((C, C), const2),
        pl.BlockSpec((1, HEADS * DH), const2),
        pl.BlockSpec((1, HEADS * DH), const2),
        state_spec, state_spec,
        pl.BlockSpec((1, D), const2),
        pl.BlockSpec((2, D, ROUTE_LANES), const3),
        pl.BlockSpec((1, ROUTE_LANES), const2),
    ]
    tok_spec = pl.BlockSpec((1, TT, D), lambda b, t: (b, t, 0))
    out_specs = [tok_spec, tok_spec,
                 pl.BlockSpec((1, 1, TT), lambda b, t: (b * nt + t, 0, 0)),
                 state_spec, state_spec]
    out_shape = [jax.ShapeDtypeStruct((B, T, D), F32),
                 jax.ShapeDtypeStruct((B, T, D), F32),
                 jax.ShapeDtypeStruct((B * nt, 1, TT), jnp.int32),
                 jax.ShapeDtypeStruct((B, HEADS, DH, DH), F32),
                 jax.ShapeDtypeStruct((B, HEADS, DH, DH), F32)]
    return pl.pallas_call(
        functools.partial(_mixer_kernel, C=C, min_s=min_s, with_cls=with_cls),
        grid=(B, nt),
        in_specs=in_specs,
        out_specs=out_specs,
        out_shape=out_shape,
        scratch_shapes=[pltpu.VMEM((HEADS, DH, DH), F32),
                        pltpu.VMEM((HEADS, DH, DH), F32),
                        pltpu.VMEM((TT, D_IN), F32),
                        pltpu.VMEM((TT, D), BF16)],
        compiler_params=pltpu.CompilerParams(
            dimension_semantics=("arbitrary", "arbitrary"),
            vmem_limit_bytes=V7X_VMEM_LIMIT),
        name="mixer",
    )(x, mod, n1g, win, wout, cosf, sinf, rd, idec, kdec, lb, lvl, tri, retg, hgg, sr0, sh0,
      n2g, wr, br)


def _router_logits(h, wr_ref, br_ref):
    h1, h2, _ = _split3(h)
    w1 = wr_ref[0]
    w2 = wr_ref[1]
    return _dot(h1, w1) + (_dot(h1, w2) + _dot(h2, w1)) + br_ref[...]


def _route_topk(logits_t):
    TM = logits_t.shape[1]
    lg = logits_t[0:N_GROUPS, :]
    gi = lax.broadcasted_iota(jnp.int32, (N_GROUPS, TM), 0).astype(F32)
    m = jnp.max(lg, axis=0, keepdims=True)
    g_w = 1.0 / jnp.sum(jnp.exp(lg - m), axis=0, keepdims=True)
    g_idx = jnp.min(jnp.where(lg == m, gi, float(N_GROUPS)), axis=0, keepdims=True)
    el = jnp.zeros((EXP_PER_GROUP, TM), F32)
    for g in range(N_GROUPS):
        lo = N_GROUPS + g * EXP_PER_GROUP
        el = jnp.where(g_idx == float(g), logits_t[lo:lo + EXP_PER_GROUP, :], el)
    ei = lax.broadcasted_iota(jnp.int32, (EXP_PER_GROUP, TM), 0).astype(F32)
    m1 = jnp.max(el, axis=0, keepdims=True)
    i1 = jnp.min(jnp.where(el == m1, ei, float(EXP_PER_GROUP)), axis=0, keepdims=True)
    el2 = jnp.where(ei == i1, -jnp.inf, el)
    m2 = jnp.max(el2, axis=0, keepdims=True)
    i2 = jnp.min(jnp.where(el2 == m2, ei, float(EXP_PER_GROUP)), axis=0, keepdims=True)
    p2 = jnp.exp(m2 - m1)
    den = 1.0 / (1.0 + p2)
    w0 = g_w * den
    w1 = g_w * (p2 * den)
    return g_idx, i1, i2, w0, w1


def _moe_kernel(x1_ref, h2_ref, mod_ref, wr_ref, br_ref, wg_ref, wu_ref, wd_ref, fg_ref,
                y_ref, h_scr, route_scr, acc_scr):
    e = pl.program_id(1)
    ne = pl.num_programs(1)
    TM, D = x1_ref.shape
    K = mod_ref.shape[0]
    TQ = TM // K

    @pl.when(e == 0)
    def _():
        h = h2_ref[...]
        h_scr[...] = h.astype(BF16)
        g_idx, i1, i2, w0, wgt1 = _route_topk(_router_logits(h, wr_ref, br_ref).T)
        e0 = g_idx * EXP_PER_GROUP + i1
        e1 = g_idx * EXP_PER_GROUP + i2
        ri = lax.broadcasted_iota(jnp.int32, (ROUTE_LANES, TM), 0)
        rows = jnp.where(ri == 0, e0, jnp.where(ri == 1, e1, jnp.where(ri == 2, w0,
                         jnp.where(ri == 3, wgt1, 0.0))))
        route_scr[...] = rows.T
        acc_scr[...] = jnp.zeros_like(acc_scr)

    hb = h_scr[...]
    hid = _silu(_dot(hb, wg_ref[0])) * _dot(hb, wu_ref[0])
    out = _dot(hid.astype(BF16), wd_ref[0])
    ef = e.astype(F32)
    route = route_scr[...]
    comb = (jnp.where(route[:, 0:1] == ef, route[:, 2:3], 0.0)
            + jnp.where(route[:, 1:2] == ef, route[:, 3:4], 0.0))
    acc_scr[...] += comb * out

    @pl.when(e == ne - 1)
    def _():
        gt2 = mod_ref[...][:, :, 5 * D:6 * D]
        ff = (gt2 * acc_scr[...].reshape(K, TQ, D)).reshape(TM, D)
        x2 = x1_ref[...] + ff
        y_ref[...] = x2 * lax.rsqrt(jnp.mean(x2 * x2, axis=-1, keepdims=True) + EPS) * fg_ref[...]


def _moe(x1, h2, mod, wr, br, wg, wu, wd, fg, *, T, TM):
    N, D = x1.shape
    if T >= TM:
        assert T % TM == 0
        seqs, tiles_per_seq = 1, T // TM
        mod_map = lambda i, e: (i // tiles_per_seq, 0, 0)
    else:
        assert TM % T == 0
        seqs = TM // T
        mod_map = lambda i, e: (i, 0, 0)
    return pl.pallas_call(
        _moe_kernel,
        grid=(N // TM, N_EXPERTS),
        in_specs=[
            pl.BlockSpec((TM, D), lambda i, e: (i, 0)),
            pl.BlockSpec((TM, D), lambda i, e: (i, 0)),
            pl.BlockSpec((seqs, 1, 6 * D), mod_map),
            pl.BlockSpec((2, D, ROUTE_LANES), lambda i, e: (0, 0, 0)),
            pl.BlockSpec((1, ROUTE_LANES), lambda i, e: (0, 0)),
            pl.BlockSpec((1, D, D_EXPERT), lambda i, e: (e, 0, 0)),
            pl.BlockSpec((1, D, D_EXPERT), lambda i, e: (e, 0, 0)),
            pl.BlockSpec((1, D_EXPERT, D), lambda i, e: (e, 0, 0)),
            pl.BlockSpec((1, D), lambda i, e: (0, 0)),
        ],
        out_specs=pl.BlockSpec((TM, D), lambda i, e: (i, 0)),
        out_shape=jax.ShapeDtypeStruct((N, D), F32),
        scratch_shapes=[pltpu.VMEM((TM, D), BF16),
                        pltpu.VMEM((TM, ROUTE_LANES), F32),
                        pltpu.VMEM((TM, D), F32)],
        compiler_params=pltpu.CompilerParams(
            dimension_semantics=("arbitrary", "arbitrary"),
            vmem_limit_bytes=V7X_VMEM_LIMIT),
        name="moe",
    )(x1, h2, mod, wr, br, wg, wu, wd, fg)


def _rank_kernel(cls_ref, pos_ref, tcls_ref, *, TG):
    R, L = cls_ref.shape
    cf = cls_ref[...].astype(F32)
    ri = lax.broadcasted_iota(jnp.int32, (L, L), 0)
    ci = lax.broadcasted_iota(jnp.int32, (L, L), 1)
    upper = (ri < ci).astype(BF16)
    rr = lax.broadcasted_iota(jnp.int32, (R, R), 0)
    rc = lax.broadcasted_iota(jnp.int32, (R, R), 1)
    lower = (rc < rr).astype(BF16)
    tile_start = lax.broadcasted_iota(jnp.int32, tcls_ref.shape, 1).astype(F32) * float(TG)
    base = jnp.zeros((1, 1), F32)
    pos = jnp.zeros((R, L), F32)
    tcls = jnp.zeros(tcls_ref.shape, F32)
    for c in range(N_CLASSES):
        ind = jnp.where(cf == float(c), 1.0, 0.0)
        lane_pre = _dot(ind.astype(BF16), upper)
        row_tot = jnp.broadcast_to(jnp.sum(ind, axis=1, keepdims=True), (R, L))
        row_pre = _dot(lower, row_tot.astype(BF16))
        total = jnp.sum(row_tot[:, 0:1], axis=0, keepdims=True)
        pos = jnp.where(ind > 0.0, base + row_pre + lane_pre, pos)
        base = base + jnp.floor((total + float(TG - 1)) * (1.0 / TG)) * float(TG)
        tcls = tcls + jnp.where(tile_start >= base, 1.0, 0.0)
    pos_ref[...] = pos.astype(jnp.int32)
    tcls_ref[...] = tcls.astype(jnp.int32)


def _rank(cls2d, *, TG, n_tiles):
    R, L = cls2d.shape
    lanes = -(-n_tiles // 128) * 128
    return pl.pallas_call(
        functools.partial(_rank_kernel, TG=TG),
        out_shape=(jax.ShapeDtypeStruct((R, L), jnp.int32),
                   jax.ShapeDtypeStruct((1, lanes), jnp.int32)),
        name="rank",
    )(cls2d)


def _row_copy(src, s, dst, d, sem):
    return pltpu.make_async_copy(src.at[pl.ds(s, 1)], dst.at[pl.ds(d, 1)], sem)


def _dispatch_kernel(pos_ref, h2_hbm, xs_in_hbm, xs_hbm, sem, *, TD):
    del xs_in_hbm
    base = pl.program_id(0) * TD

    def issue(r, carry):
        _row_copy(h2_hbm, base + r, xs_hbm, pos_ref[r], sem).start()
        return carry

    lax.fori_loop(0, TD, issue, 0, unroll=8)
    pltpu.make_async_copy(h2_hbm.at[pl.ds(0, TD)], xs_hbm.at[pl.ds(0, TD)], sem).wait()


def _dispatch(pos, h2, xs_zero, *, TD):
    N, D = h2.shape
    return pl.pallas_call(
        functools.partial(_dispatch_kernel, TD=TD),
        grid=(N // TD,),
        in_specs=[pl.BlockSpec((TD,), lambda i: (i,), memory_space=pltpu.SMEM),
                  pl.BlockSpec(memory_space=pl.ANY),
                  pl.BlockSpec(memory_space=pl.ANY)],
        out_specs=pl.BlockSpec(memory_space=pl.ANY),
        out_shape=jax.ShapeDtypeStruct(xs_zero.shape, F32),
        scratch_shapes=[pltpu.SemaphoreType.DMA(())],
        input_output_aliases={2: 0},
        compiler_params=pltpu.CompilerParams(dimension_semantics=("arbitrary",)),
        name="dispatch",
    )(pos, h2, xs_zero)


def _expert_kernel(ea_ref, eb_ref, tc_ref, xs_ref, wr_ref, br_ref, wga_ref, wua_ref, wda_ref,
                   wgb_ref, wub_ref, wdb_ref, ys_ref):
    j = pl.program_id(0)
    TG = xs_ref.shape[0]

    @pl.when(tc_ref[j] < N_CLASSES)
    def _():
        a = ea_ref[j]
        b = eb_ref[j]
        g = lax.div(a, jnp.int32(EXP_PER_GROUP))
        h = xs_ref[...]
        logits = _router_logits(h, wr_ref, br_ref)
        lane = lax.broadcasted_iota(jnp.int32, (TG, ROUTE_LANES), 1)
        isg = lane < N_GROUPS
        m = jnp.max(jnp.where(isg, logits, -jnp.inf), axis=1, keepdims=True)
        ssum = jnp.sum(jnp.where(isg, jnp.exp(logits - m), 0.0), axis=1, keepdims=True)
        lg = jnp.sum(jnp.where(lane == g, logits, 0.0), axis=1, keepdims=True)
        la = jnp.sum(jnp.where(lane == N_GROUPS + a, logits, 0.0), axis=1, keepdims=True)
        lb = jnp.sum(jnp.where(lane == N_GROUPS + b, logits, 0.0), axis=1, keepdims=True)
        g_w = jnp.exp(lg - m) / ssum
        mm = jnp.maximum(la, lb)
        pa = jnp.exp(la - mm)
        pb = jnp.exp(lb - mm)
        den = g_w / (pa + pb)
        hb = h.astype(BF16)
        hid_a = _silu(_dot(hb, wga_ref[0])) * _dot(hb, wua_ref[0])
        out = (pa * den) * _dot(hid_a.astype(BF16), wda_ref[0])
        hid_b = _silu(_dot(hb, wgb_ref[0])) * _dot(hb, wub_ref[0])
        ys_ref[...] = out + (pb * den) * _dot(hid_b.astype(BF16), wdb_ref[0])

    @pl.when(tc_ref[j] >= N_CLASSES)
    def _():
        ys_ref[...] = jnp.zeros_like(ys_ref)


def _experts(ea, eb, tcls, xs, wr, br, wg, wu, wd, *, TG):
    NS, D = xs.shape
    wmap_a = lambda j, ea, eb, tc: (ea[j], 0, 0)
    wmap_b = lambda j, ea, eb, tc: (eb[j], 0, 0)
    grid_spec = pltpu.PrefetchScalarGridSpec(
        num_scalar_prefetch=3,
        grid=(NS // TG,),
        in_specs=[
            pl.BlockSpec((TG, D), lambda j, ea, eb, tc: (j, 0)),
            pl.BlockSpec((2, D, ROUTE_LANES), lambda j, ea, eb, tc: (0, 0, 0)),
            pl.BlockSpec((1, ROUTE_LANES), lambda j, ea, eb, tc: (0, 0)),
            pl.BlockSpec((1, D, D_EXPERT), wmap_a),
            pl.BlockSpec((1, D, D_EXPERT), wmap_a),
            pl.BlockSpec((1, D_EXPERT, D), wmap_a),
            pl.BlockSpec((1, D, D_EXPERT), wmap_b),
            pl.BlockSpec((1, D, D_EXPERT), wmap_b),
            pl.BlockSpec((1, D_EXPERT, D), wmap_b),
        ],
        out_specs=pl.BlockSpec((TG, D), lambda j, ea, eb, tc: (j, 0)),
    )
    return pl.pallas_call(
        _expert_kernel,
        grid_spec=grid_spec,
        out_shape=jax.ShapeDtypeStruct((NS, D), F32),
        compiler_params=pltpu.CompilerParams(
            dimension_semantics=("arbitrary",), vmem_limit_bytes=V7X_VMEM_LIMIT),
        name="experts",
    )(ea, eb, tcls, xs, wr, br, wg, wu, wd, wg, wu, wd)


def _combine_kernel(pos_ref, posn_ref, x1_ref, mod_ref, fg_ref, ys_hbm, y_ref, buf, sems, *, TC):
    i = pl.program_id(0)
    n = pl.num_programs(0)
    slot = i % 2

    def gather(p_ref, s):
        def issue(r, carry):
            _row_copy(ys_hbm, p_ref[r], buf.at[s], r, sems.at[s]).start()
            return carry
        lax.fori_loop(0, TC, issue, 0, unroll=8)

    @pl.when(i == 0)
    def _():
        gather(pos_ref, 0)

    @pl.when(i + 1 < n)
    def _():
        gather(posn_ref, 1 - slot)

    pltpu.make_async_copy(ys_hbm.at[pl.ds(0, TC)], buf.at[slot], sems.at[slot]).wait()
    gt2 = mod_ref[0][:, 5 * D_MODEL:6 * D_MODEL]
    x2 = x1_ref[...] + gt2 * buf[slot]
    y_ref[...] = x2 * lax.rsqrt(jnp.mean(x2 * x2, axis=-1, keepdims=True) + EPS) * fg_ref[...]


def _combine(pos, x1, mod, fg, ys, *, T, TC):
    N, D = x1.shape
    n = N // TC
    tiles_per_seq = T // TC
    return pl.pallas_call(
        functools.partial(_combine_kernel, TC=TC),
        grid=(n,),
        in_specs=[pl.BlockSpec((TC,), lambda i: (i,), memory_space=pltpu.SMEM),
                  pl.BlockSpec((TC,), lambda i: (jnp.minimum(i + 1, n - 1),),
                               memory_space=pltpu.SMEM),
                  pl.BlockSpec((TC, D), lambda i: (i, 0)),
                  pl.BlockSpec((1, 1, 6 * D), lambda i: (i // tiles_per_seq, 0, 0)),
                  pl.BlockSpec((1, D), lambda i: (0, 0)),
                  pl.BlockSpec(memory_space=pl.ANY)],
        out_specs=pl.BlockSpec((TC, D), lambda i: (i, 0)),
        out_shape=jax.ShapeDtypeStruct((N, D), F32),
        scratch_shapes=[pltpu.VMEM((2, TC, D), F32), pltpu.SemaphoreType.DMA((2,))],
        compiler_params=pltpu.CompilerParams(
            dimension_semantics=("arbitrary",), vmem_limit_bytes=V7X_VMEM_LIMIT),
        name="combine",
    )(pos, pos, x1, mod, fg, ys)


def _routed_moe(x1, h2, cls, mod, p, *, T):
    N, D = x1.shape
    TG = ROUTED_TILE
    n_tiles = N // TG + N_CLASSES
    pos2d, tcls = _rank(cls.reshape(N // 128, 128), TG=TG, n_tiles=n_tiles)
    pos = pos2d.reshape(N)
    tcls = tcls[0, :n_tiles]
    ea = jnp.asarray(CLASS_EXPERT_A)[tcls]
    eb = jnp.asarray(CLASS_EXPERT_B)[tcls]
    xs = _dispatch(pos, h2, jnp.zeros((n_tiles * TG, D), F32), TD=DISPATCH_TILE)
    ys = _experts(ea, eb, tcls, xs, p["wr"], p["br"], p["wg"], p["wu"], p["wd"], TG=TG)
    return _combine(pos, x1, mod, p["fg"], ys, T=T, TC=COMBINE_TILE)


def _tiling(T):
    if T <= REF_CHUNK:
        return T, T
    return MIXER_CHUNK, min(T, MIXER_TILE)


def _trunk(x, mod, s_ret, s_hg, pos0, inv2, p, *, min_s=8):
    B, T, D = x.shape
    N = B * T
    C, TT = _tiling(T)
    min_s = min(min_s, C // 2)
    routed = N >= ROUTED_MIN_TOKENS
    tabs = _tables(inv2, p["lb_logits"], T=T, C=C, pos0=pos0, min_s=min_s)
    mod3 = mod.reshape(B, 1, 6 * D)
    x1, h2, cls, sr, sh = _mixer(x, mod3, p["n1g"], p["win"], p["wout"], tabs, p["retg"], p["hgg"],
                                 s_ret, s_hg, p["n2g"], p["wr"], p["br"], C=C, TT=TT,
                                 min_s=min_s, with_cls=routed)
    x1 = x1.reshape(N, D)
    h2 = h2.reshape(N, D)
    if routed:
        y = _routed_moe(x1, h2, cls, mod3, p, T=T)
    else:
        y = _moe(x1, h2, mod3, p["wr"], p["br"], p["wg"], p["wu"], p["wd"], p["fg"],
                 T=T, TM=min(N, 1024))
    return y.reshape(B, T, D), sr[None], sh[None]


def _prepare(W_ada, b_ada, norm1_g, norm2_g, W_in, ret_norm_g, hgrn_norm_g, hgrn_lb_logits, W_out,
             W_router_group, b_router_group, W_router_expert, b_router_expert, W_gate_e, W_up_e,
             W_down_e, final_norm_g):
    assert W_in.shape[0] == 1, "single-layer stack only"
    D = D_MODEL
    wr = jnp.concatenate([W_router_group[0], W_router_expert[0]], axis=1)
    wr = jnp.pad(wr, ((0, 0), (0, ROUTE_LANES - wr.shape[1])))
    wr1 = wr.astype(BF16)
    wr2 = (wr - wr1.astype(F32)).astype(BF16)
    br = jnp.concatenate([b_router_group[0], b_router_expert[0]])
    br = jnp.pad(br, (0, ROUTE_LANES - br.shape[0])).reshape(1, ROUTE_LANES)
    return dict(
        n1g=norm1_g[0].reshape(1, D), n2g=norm2_g[0].reshape(1, D), fg=final_norm_g.reshape(1, D),
        win=W_in[0].astype(BF16), wout=W_out[0].astype(BF16),
        retg=ret_norm_g[0].reshape(1, -1), hgg=hgrn_norm_g[0].reshape(1, -1),
        lb_logits=hgrn_lb_logits,
        wr=jnp.stack([wr1, wr2]), br=br,
        wg=W_gate_e[0].astype(BF16), wu=W_up_e[0].astype(BF16), wd=W_down_e[0].astype(BF16),
    )


def _rope_inv():
    half = DH // 2
    inv = 1.0 / (ROPE_BASE ** (jnp.arange(half, dtype=F32) / half))
    return jnp.concatenate([inv, inv]).reshape(1, DH)


def kernel(x_prompt, x_sample, c_prompt, c_sample, state_ret, state_hgrn, W_ada, b_ada, norm1_g,
           norm2_g, W_in, ret_norm_g, hgrn_norm_g, hgrn_lb_logits, W_out, W_router_group,
           b_router_group, W_router_expert, b_router_expert, W_gate_e, W_up_e, W_down_e,
           final_norm_g):
    p = _prepare(W_ada, b_ada, norm1_g, norm2_g, W_in, ret_norm_g, hgrn_norm_g, hgrn_lb_logits,
                 W_out, W_router_group, b_router_group, W_router_expert, b_router_expert,
                 W_gate_e, W_up_e, W_down_e, final_norm_g)
    Bp = x_prompt.shape[0]
    Bs = x_sample.shape[0]
    inv2 = _rope_inv()
    mod = _ada(jnp.concatenate([c_prompt, c_sample], axis=0), W_ada[0], b_ada[0].reshape(1, -1))
    zeros = jnp.zeros((Bp, HEADS, DH, DH), F32)
    y_p, sr_p, sh_p = _trunk(x_prompt, mod[:Bp], zeros, zeros, 0, inv2, p)
    y_s, sr_s, sh_s = _trunk(x_sample, mod[Bp:Bp + Bs], state_ret[0], state_hgrn[0], PAST_LEN,
                             inv2, p)
    return (y_p, y_s, sr_p, sh_p, sr_s, sh_s)
```

```python
import functools

import numpy as np
import jax
import jax.numpy as jnp
from jax import lax
from jax.experimental import pallas as pl
from jax.experimental.pallas import tpu as pltpu

F32 = jnp.float32
BF16 = jnp.bfloat16

D_MODEL = 1024
HEADS = 4
DH = 128
D_IN = 8 * HEADS * DH
N_GROUPS = 4
EXP_PER_GROUP = 4
N_EXPERTS = N_GROUPS * EXP_PER_GROUP
D_EXPERT = 512
ROPE_BASE = 10000.0
EPS = 1e-6
REF_CHUNK = 64
PAST_LEN = 4096

V7X_VMEM_LIMIT = 56 * 1024 * 1024
ROUTE_LANES = 128

_PAIRS = [(i, j) for i in range(EXP_PER_GROUP) for j in range(i + 1, EXP_PER_GROUP)]
N_PAIRS = len(_PAIRS)
N_CLASSES = N_GROUPS * N_PAIRS
CLASS_EXPERT_A = np.array([g * EXP_PER_GROUP + i for g in range(N_GROUPS) for i, _ in _PAIRS]
                          + [N_EXPERTS - 2], np.int32)
CLASS_EXPERT_B = np.array([g * EXP_PER_GROUP + j for g in range(N_GROUPS) for _, j in _PAIRS]
                          + [N_EXPERTS - 1], np.int32)
MIXER_CHUNK = 128
MIXER_TILE = 512
MIXER_PROJ_ROWS = 256
ROUTED_MIN_TOKENS = 8192
ROUTED_TILE = 512
DISPATCH_TILE = 1024
COMBINE_TILE = 512


def _dot(a, b):
    return jnp.dot(a, b, preferred_element_type=F32)


def _dot_nt(a, b):
    return lax.dot_general(a, b, (((1,), (1,)), ((), ())), preferred_element_type=F32)


def _dot_tn(a, b):
    return lax.dot_general(a, b, (((0,), (0,)), ((), ())), preferred_element_type=F32)


def _split3(a):
    a1 = a.astype(BF16)
    r1 = a - a1.astype(F32)
    a2 = r1.astype(BF16)
    a3 = (r1 - a2.astype(F32)).astype(BF16)
    return a1, a2, a3


def _silu(a):
    return a * (1.0 / (1.0 + jnp.exp(-a)))


def _tables_kernel(inv_ref, lbl_ref, cos_ref, sin_ref, rd_ref, idec_ref, kdec_ref, lb_ref,
                   lvl_ref, tri_ref, *, T, C, pos0, min_s):
    pos = (lax.broadcasted_iota(jnp.int32, (T, DH), 0) + pos0).astype(F32)
    ang = pos * inv_ref[...]
    lane = lax.broadcasted_iota(jnp.int32, (T, DH), 1)
    cos_ref[...] = jnp.cos(ang)
    s = jnp.sin(ang)
    sin_ref[...] = jnp.where(lane < DH // 2, -s, s)

    li = lax.broadcasted_iota(jnp.int32, (C, C), 0)
    mi = lax.broadcasted_iota(jnp.int32, (C, C), 1)
    diff = (li - mi).astype(F32)
    rowj = lax.broadcasted_iota(jnp.int32, (C, DH), 0).astype(F32)
    for h in range(HEADS):
        lg = jnp.log(jnp.full((1, 1), 1.0 - 2.0 ** (-5.0 - h), F32))
        rd_ref[h] = jnp.where(diff >= 0, jnp.exp(jnp.maximum(diff, 0.0) * lg), 0.0)
        idec_ref[h] = jnp.exp((rowj + 1.0) * lg)
        kdec_ref[h] = jnp.exp((C - 1.0 - rowj) * lg)

    lg_all = lbl_ref[...]
    m = jnp.max(lg_all, axis=0, keepdims=True)
    e = jnp.exp(lg_all - m)
    lb_ref[...] = e[0:1, :] / jnp.sum(e, axis=0, keepdims=True)

    x = li ^ mi
    code = jnp.zeros((C, C), jnp.int32)
    sz = min_s
    while sz < C:
        code = code + (x >= sz).astype(jnp.int32)
        sz *= 2
    code = jnp.where(x >= min_s, code + (min_s.bit_length() - 1), 0)
    lvl_ref[...] = jnp.where(li >= mi, code, -1)
    tri_ref[...] = (li >= mi).astype(BF16)


def _tables(inv2, lb_logits, *, T, C, pos0, min_s):
    out_shape = (
        jax.ShapeDtypeStruct((T, DH), F32),
        jax.ShapeDtypeStruct((T, DH), F32),
        jax.ShapeDtypeStruct((HEADS, C, C), F32),
        jax.ShapeDtypeStruct((HEADS, C, DH), F32),
        jax.ShapeDtypeStruct((HEADS, C, DH), F32),
        jax.ShapeDtypeStruct((1, HEADS * DH), F32),
        jax.ShapeDtypeStruct((C, C), jnp.int32),
        jax.ShapeDtypeStruct((C, C), BF16),
    )
    return pl.pallas_call(
        functools.partial(_tables_kernel, T=T, C=C, pos0=pos0, min_s=min_s),
        out_shape=out_shape,
        name="tables",
    )(inv2, lb_logits)


def _ada_kernel(c_ref, w_ref, b_ref, o_ref):
    a = _silu(c_ref[...])
    w = w_ref[...]
    a1, a2, a3 = _split3(a)
    w1, w2, w3 = _split3(w)
    acc = _dot(a1, w1) + (_dot(a1, w2) + _dot(a2, w1)) + (_dot(a1, w3) + _dot(a2, w2) + _dot(a3, w1))
    o_ref[...] = acc + b_ref[...]


def _ada(c, w, b):
    R = c.shape[0]
    N = w.shape[1]
    TN = 512
    return pl.pallas_call(
        _ada_kernel,
        grid=(N // TN,),
        in_specs=[pl.BlockSpec((R, D_MODEL), lambda j: (0, 0)),
                  pl.BlockSpec((D_MODEL, TN), lambda j: (0, j)),
                  pl.BlockSpec((1, TN), lambda j: (0, j))],
        out_specs=pl.BlockSpec((R, TN), lambda j: (0, j)),
        out_shape=jax.ShapeDtypeStruct((R, N), F32),
        name="ada",
    )(c, w, b)


def _block_ref(b, block, idx):
    C, W = b.shape
    if block >= 8:
        b3 = b.reshape(C // block, block, W)
        r = jnp.broadcast_to(b3[:, idx:idx + 1, :], b3.shape)
        return r.reshape(C, W)
    row = lax.broadcasted_iota(jnp.int32, (C, W), 0)
    p = row % block
    out = b
    for q in range(block):
        sh = q - idx
        if sh == 0:
            continue
        rolled = pltpu.roll(b, sh % C, 0)
        out = jnp.where(p == q, rolled, out)
    return out


def _head_norm_gate(o, g_row, gate):
    y = o * lax.rsqrt(jnp.mean(o * o, axis=-1, keepdims=True) + EPS)
    return (y * g_row) * _silu(gate)


def _mixer_chunk(r0, proj_scr, cos_ref, sin_ref, rd_ref, idec_ref, kdec_ref, lb_ref, lvl_ref,
                 tri_ref, retg_ref, hgg_ref, sr_scr, sht_scr, o_scr, *, C, min_s):
    W = HEADS * DH
    rows = pl.ds(r0, C)
    proj = proj_scr[rows, :]
    cosf = cos_ref[rows, :]
    sinf = sin_ref[rows, :]

    for h in range(HEADS):
        c0 = h * DH
        q = proj[:, c0:c0 + DH]
        k = proj[:, W + c0:W + c0 + DH]
        v = proj[:, 2 * W + c0:2 * W + c0 + DH].astype(BF16)
        g = proj[:, 3 * W + c0:3 * W + c0 + DH]
        qr = (q * cosf + pltpu.roll(q, DH // 2, 1) * sinf)
        kr = (k * cosf + pltpu.roll(k, DH // 2, 1) * sinf) * (DH ** -0.5)
        qb = qr.astype(BF16)
        A = _dot_nt(qb, kr.astype(BF16)) * rd_ref[h]
        S = sr_scr[h]
        idec = idec_ref[h]
        o = _dot(A.astype(BF16), v) + _dot(qb, S.astype(BF16)) * idec
        sr_scr[h] = idec[C - 1:C, :] * S + _dot_tn((kr * kdec_ref[h]).astype(BF16), v)
        o_scr[rows, c0:c0 + DH] = _head_norm_gate(o, retg_ref[:, c0:c0 + DH], g).astype(BF16)

    z = proj[:, 5 * W:6 * W]
    lb = lb_ref[...]
    e = jnp.exp(-jnp.abs(z))
    r = 1.0 / (1.0 + e)
    er = e * r
    sig = jnp.where(z >= 0, r, er)
    nsig = jnp.where(z >= 0, er, r)
    logf = jnp.log(lb + (1.0 - lb) * sig)
    kk_all = (1.0 - lb) * nsig
    tri = tri_ref[...]
    l1, l2, l3 = _split3(logf)
    b_all = _dot(tri, l1) + _dot(tri, l2) + _dot(tri, l3)

    lvl = lvl_ref[...]
    rowc = lax.broadcasted_iota(jnp.int32, (C, DH), 0)
    for h in range(HEADS):
        c0 = h * DH
        q = proj[:, 4 * W + c0:4 * W + c0 + DH] * (DH ** -0.5)
        kk = kk_all[:, c0:c0 + DH]
        b = b_all[:, c0:c0 + DH]
        v = proj[:, 6 * W + c0:6 * W + c0 + DH]
        vb = v.astype(BF16)
        g = proj[:, 7 * W + c0:7 * W + c0 + DH]

        A = jnp.zeros((C, C), F32)
        s = C // 2
        while s >= min_s:
            isq = (rowc % (2 * s)) >= s
            ref = _block_ref(b, 2 * s, s - 1)
            d = b - ref
            zz = jnp.where(isq, q, kk) * jnp.exp(jnp.where(isq, d, -d))
            zb = zz.astype(BF16)
            A = jnp.where(lvl == s.bit_length(), _dot_nt(zb, zb), A)
            s //= 2
        if min_s > 1:
            ref = _block_ref(b, min_s, 0)
            d = b - ref
            P = _dot_nt((q * jnp.exp(d)).astype(BF16), (kk * jnp.exp(-d)).astype(BF16))
            A = jnp.where(lvl == 0, P, A)
            o = _dot(A.astype(BF16), vb)
        else:
            o = _dot(A.astype(BF16), vb) + jnp.sum(q * kk, axis=-1, keepdims=True) * v

        ST = sht_scr[h]
        bl = b[C - 1:C, :]
        o = o + _dot_nt((q * jnp.exp(b)).astype(BF16), ST.astype(BF16))
        kdec = kk * jnp.exp(bl - b)
        sht_scr[h] = ST * jnp.exp(bl) + _dot_tn(vb, kdec.astype(BF16))
        o_scr[rows, W + c0:W + c0 + DH] = _head_norm_gate(o, hgg_ref[:, c0:c0 + DH], g).astype(BF16)


def _mixer_kernel(x_ref, xnext_ref, mod_ref, n1g_ref, win_ref, wout_ref, cos_ref, sin_ref, rd_ref,
                  idec_ref, kdec_ref, lb_ref, lvl_ref, tri_ref, retg_ref, hgg_ref,
                  sr0_ref, sh0_ref, n2g_ref, wr_ref, br_ref,
                  x1_ref, h2_ref, cls_ref, sro_ref, sho_ref,
                  sr_scr, sht_scr, proj_scr, o_scr, *, C, min_s, with_cls, has_next):
    t = pl.program_id(1)
    nt = pl.num_programs(1)
    TT = x_ref.shape[1]
    n_chunks = TT // C

    mod = mod_ref[0]
    sh1 = mod[:, 0:D_MODEL]
    sc1 = mod[:, D_MODEL:2 * D_MODEL]
    gt1 = mod[:, 2 * D_MODEL:3 * D_MODEL]

    PR = xnext_ref.shape[1]
    per_block = PR // C

    def project(xc, r0):
        xn = xc * lax.rsqrt(jnp.mean(xc * xc, axis=-1, keepdims=True) + EPS) * n1g_ref[...]
        hmod = xn * (1.0 + sc1) + sh1
        proj_scr[r0:r0 + PR, :] = _dot(hmod.astype(BF16), win_ref[...])

    @pl.when(t == 0)
    def _():
        for h in range(HEADS):
            sr_scr[h] = sr0_ref[0, h]
            sht_scr[h] = sh0_ref[0, h].T
        project(x_ref[0, 0:PR, :], 0)

    chunk = functools.partial(
        _mixer_chunk, proj_scr=proj_scr, cos_ref=cos_ref, sin_ref=sin_ref, rd_ref=rd_ref,
        idec_ref=idec_ref, kdec_ref=kdec_ref, lb_ref=lb_ref, lvl_ref=lvl_ref, tri_ref=tri_ref,
        retg_ref=retg_ref, hgg_ref=hgg_ref, sr_scr=sr_scr, sht_scr=sht_scr, o_scr=o_scr,
        C=C, min_s=min_s)
    for c in range(n_chunks):
        chunk(c * C)
        if c % per_block == 0:
            r1 = (c // per_block + 1) * PR
            if r1 < TT:
                project(x_ref[0, r1:r1 + PR, :], r1)
            elif has_next:
                project(xnext_ref[0], 0)

    x = x_ref[0]
    mix = _dot(o_scr[...], wout_ref[...])
    x1 = x + gt1 * mix
    x1_ref[0] = x1

    sh2 = mod[:, 3 * D_MODEL:4 * D_MODEL]
    sc2 = mod[:, 4 * D_MODEL:5 * D_MODEL]
    x1n = x1 * lax.rsqrt(jnp.mean(x1 * x1, axis=-1, keepdims=True) + EPS) * n2g_ref[...]
    h2 = x1n * (1.0 + sc2) + sh2
    h2_ref[0] = h2
    if with_cls:
        g_idx, i1, i2 = _route_topk(_router_logits(h2, wr_ref, br_ref).T)[:3]
        lo = jnp.minimum(i1, i2)
        hi = jnp.maximum(i1, i2)
        pair = hi - 1.0 + jnp.where(lo == 1.0, 2.0, 0.0) + jnp.where(lo == 2.0, 3.0, 0.0)
        cls_ref[0] = (g_idx * float(N_PAIRS) + pair).astype(jnp.int32)
    else:
        cls_ref[0] = jnp.zeros((1, TT), jnp.int32)

    @pl.when(t == nt - 1)
    def _():
        for h in range(HEADS):
            sro_ref[0, h] = sr_scr[h]
            sho_ref[0, h] = sht_scr[h].T


def _mixer(x, mod, n1g, win, wout, tabs, retg, hgg, sr0, sh0, n2g, wr, br, *, C, TT, min_s,
           with_cls):
    B, T, D = x.shape
    cosf, sinf, rd, idec, kdec, lb, lvl, tri = tabs
    nt = T // TT
    const2 = lambda b, t: (0, 0)
    const3 = lambda b, t: (0, 0, 0)
    state_spec = pl.BlockSpec((1, HEADS, DH, DH), lambda b, t: (b, 0, 0, 0))
    PR = min(TT, MIXER_PROJ_ROWS)
    per_tile = TT // PR
    last_block = T // PR - 1
    in_specs = [
        pl.BlockSpec((1, TT, D), lambda b, t: (b, t, 0)),
        pl.BlockSpec((1, PR, D),
                     lambda b, t: (b, jnp.minimum((t + 1) * per_tile, last_block), 0)),
        pl.BlockSpec((1, 1, 6 * D), lambda b, t: (b, 0, 0)),
        pl.BlockSpec((1, D), const2),
        pl.BlockSpec((D, D_IN), const2),
        pl.BlockSpec((D, D), const2),
        pl.BlockSpec((TT, DH), lambda b, t: (t, 0)),
        pl.BlockSpec((TT, DH), lambda b, t: (t, 0)),
        pl.BlockSpec((HEADS, C, C), const3),
        pl.BlockSpec((HEADS, C, DH), const3),
        pl.BlockSpec((HEADS, C, DH), const3),
        pl.BlockSpec((1, HEADS * DH), const2),
        pl.BlockSpec((C, C), const2),
        pl.BlockSpec((C, C), const2),
        pl.BlockSpec((1, HEADS * DH), const2),
        pl.BlockSpec((1, HEADS * DH), const2),
        state_spec, state_spec,
        pl.BlockSpec((1, D), const2),
        pl.BlockSpec((2, D, ROUTE_LANES), const3),
        pl.BlockSpec((1, ROUTE_LANES), const2),
    ]
    tok_spec = pl.BlockSpec((1, TT, D), lambda b, t: (b, t, 0))
    out_specs = [tok_spec, tok_spec,
                 pl.BlockSpec((1, 1, TT), lambda b, t: (b * nt + t, 0, 0)),
                 state_spec, state_spec]
    out_shape = [jax.ShapeDtypeStruct((B, T, D), F32),
                 jax.ShapeDtypeStruct((B, T, D), F32),
                 jax.ShapeDtypeStruct((B * nt, 1, TT), jnp.int32),
                 jax.ShapeDtypeStruct((B, HEADS, DH, DH), F32),
                 jax.ShapeDtypeStruct((B, HEADS, DH, DH), F32)]
    return pl.pallas_call(
        functools.partial(_mixer_kernel, C=C, min_s=min_s, with_cls=with_cls, has_next=nt > 1),
        grid=(B, nt),
        in_specs=in_specs,
        out_specs=out_specs,
        out_shape=out_shape,
        scratch_shapes=[pltpu.VMEM((HEADS, DH, DH), F32),
                        pltpu.VMEM((HEADS, DH, DH), F32),
                        pltpu.VMEM((TT, D_IN), F32),
                        pltpu.VMEM((TT, D), BF16)],
        compiler_params=pltpu.CompilerParams(
            dimension_semantics=("arbitrary", "arbitrary"),
            vmem_limit_bytes=V7X_VMEM_LIMIT),
        name="mixer",
    )(x, x, mod, n1g, win, wout, cosf, sinf, rd, idec, kdec, lb, lvl, tri, retg, hgg, sr0, sh0,
      n2g, wr, br)


def _router_logits(h, wr_ref, br_ref):
    h1, h2, _ = _split3(h)
    w1 = wr_ref[0]
    w2 = wr_ref[1]
    return _dot(h1, w1) + (_dot(h1, w2) + _dot(h2, w1)) + br_ref[...]


def _route_topk(logits_t):
    TM = logits_t.shape[1]
    lg = logits_t[0:N_GROUPS, :]
    gi = lax.broadcasted_iota(jnp.int32, (N_GROUPS, TM), 0).astype(F32)
    m = jnp.max(lg, axis=0, keepdims=True)
    g_w = 1.0 / jnp.sum(jnp.exp(lg - m), axis=0, keepdims=True)
    g_idx = jnp.min(jnp.where(lg == m, gi, float(N_GROUPS)), axis=0, keepdims=True)
    el = jnp.zeros((EXP_PER_GROUP, TM), F32)
    for g in range(N_GROUPS):
        lo = N_GROUPS + g * EXP_PER_GROUP
        el = jnp.where(g_idx == float(g), logits_t[lo:lo + EXP_PER_GROUP, :], el)
    ei = lax.broadcasted_iota(jnp.int32, (EXP_PER_GROUP, TM), 0).astype(F32)
    m1 = jnp.max(el, axis=0, keepdims=True)
    i1 = jnp.min(jnp.where(el == m1, ei, float(EXP_PER_GROUP)), axis=0, keepdims=True)
    el2 = jnp.where(ei == i1, -jnp.inf, el)
    m2 = jnp.max(el2, axis=0, keepdims=True)
    i2 = jnp.min(jnp.where(el2 == m2, ei, float(EXP_PER_GROUP)), axis=0, keepdims=True)
    p2 = jnp.exp(m2 - m1)
    den = 1.0 / (1.0 + p2)
    w0 = g_w * den
    w1 = g_w * (p2 * den)
    return g_idx, i1, i2, w0, w1


def _moe_kernel(x1_ref, h2_ref, mod_ref, wr_ref, br_ref, wg_ref, wu_ref, wd_ref, fg_ref,
                y_ref, h_scr, route_scr, acc_scr):
    e = pl.program_id(1)
    ne = pl.num_programs(1)
    TM, D = x1_ref.shape
    K = mod_ref.shape[0]
    TQ = TM // K

    @pl.when(e == 0)
    def _():
        h = h2_ref[...]
        h_scr[...] = h.astype(BF16)
        g_idx, i1, i2, w0, wgt1 = _route_topk(_router_logits(h, wr_ref, br_ref).T)
        e0 = g_idx * EXP_PER_GROUP + i1
        e1 = g_idx * EXP_PER_GROUP + i2
        ri = lax.broadcasted_iota(jnp.int32, (ROUTE_LANES, TM), 0)
        rows = jnp.where(ri == 0, e0, jnp.where(ri == 1, e1, jnp.where(ri == 2, w0,
                         jnp.where(ri == 3, wgt1, 0.0))))
        route_scr[...] = rows.T
        acc_scr[...] = jnp.zeros_like(acc_scr)

    hb = h_scr[...]
    hid = _silu(_dot(hb, wg_ref[0])) * _dot(hb, wu_ref[0])
    out = _dot(hid.astype(BF16), wd_ref[0])
    ef = e.astype(F32)
    route = route_scr[...]
    comb = (jnp.where(route[:, 0:1] == ef, route[:, 2:3], 0.0)
            + jnp.where(route[:, 1:2] == ef, route[:, 3:4], 0.0))
    acc_scr[...] += comb * out

    @pl.when(e == ne - 1)
    def _():
        gt2 = mod_ref[...][:, :, 5 * D:6 * D]
        ff = (gt2 * acc_scr[...].reshape(K, TQ, D)).reshape(TM, D)
        x2 = x1_ref[...] + ff
        y_ref[...] = x2 * lax.rsqrt(jnp.mean(x2 * x2, axis=-1, keepdims=True) + EPS) * fg_ref[...]


def _moe(x1, h2, mod, wr, br, wg, wu, wd, fg, *, T, TM):
    N, D = x1.shape
    if T >= TM:
        assert T % TM == 0
        seqs, tiles_per_seq = 1, T // TM
        mod_map = lambda i, e: (i // tiles_per_seq, 0, 0)
    else:
        assert TM % T == 0
        seqs = TM // T
        mod_map = lambda i, e: (i, 0, 0)
    return pl.pallas_call(
        _moe_kernel,
        grid=(N // TM, N_EXPERTS),
        in_specs=[
            pl.BlockSpec((TM, D), lambda i, e: (i, 0)),
            pl.BlockSpec((TM, D), lambda i, e: (i, 0)),
            pl.BlockSpec((seqs, 1, 6 * D), mod_map),
            pl.BlockSpec((2, D, ROUTE_LANES), lambda i, e: (0, 0, 0)),
            pl.BlockSpec((1, ROUTE_LANES), lambda i, e: (0, 0)),
            pl.BlockSpec((1, D, D_EXPERT), lambda i, e: (e, 0, 0)),
            pl.BlockSpec((1, D, D_EXPERT), lambda i, e: (e, 0, 0)),
            pl.BlockSpec((1, D_EXPERT, D), lambda i, e: (e, 0, 0)),
            pl.BlockSpec((1, D), lambda i, e: (0, 0)),
        ],
        out_specs=pl.BlockSpec((TM, D), lambda i, e: (i, 0)),
        out_shape=jax.ShapeDtypeStruct((N, D), F32),
        scratch_shapes=[pltpu.VMEM((TM, D), BF16),
                        pltpu.VMEM((TM, ROUTE_LANES), F32),
                        pltpu.VMEM((TM, D), F32)],
        compiler_params=pltpu.CompilerParams(
            dimension_semantics=("arbitrary", "arbitrary"),
            vmem_limit_bytes=V7X_VMEM_LIMIT),
        name="moe",
    )(x1, h2, mod, wr, br, wg, wu, wd, fg)


def _rank_kernel(cls_ref, pos_ref, tcls_ref, *, TG):
    R, L = cls_ref.shape
    cf = cls_ref[...].astype(F32)
    ri = lax.broadcasted_iota(jnp.int32, (L, L), 0)
    ci = lax.broadcasted_iota(jnp.int32, (L, L), 1)
    upper = (ri < ci).astype(BF16)
    rr = lax.broadcasted_iota(jnp.int32, (R, R), 0)
    rc = lax.broadcasted_iota(jnp.int32, (R, R), 1)
    lower = (rc < rr).astype(BF16)
    tile_start = lax.broadcasted_iota(jnp.int32, tcls_ref.shape, 1).astype(F32) * float(TG)
    base = jnp.zeros((1, 1), F32)
    pos = jnp.zeros((R, L), F32)
    tcls = jnp.zeros(tcls_ref.shape, F32)
    for c in range(N_CLASSES):
        ind = jnp.where(cf == float(c), 1.0, 0.0)
        lane_pre = _dot(ind.astype(BF16), upper)
        row_tot = jnp.broadcast_to(jnp.sum(ind, axis=1, keepdims=True), (R, L))
        row_pre = _dot(lower, row_tot.astype(BF16))
        total = jnp.sum(row_tot[:, 0:1], axis=0, keepdims=True)
        pos = jnp.where(ind > 0.0, base + row_pre + lane_pre, pos)
        base = base + jnp.floor((total + float(TG - 1)) * (1.0 / TG)) * float(TG)
        tcls = tcls + jnp.where(tile_start >= base, 1.0, 0.0)
    pos_ref[...] = pos.astype(jnp.int32)
    tcls_ref[...] = tcls.astype(jnp.int32)


def _rank(cls2d, *, TG, n_tiles):
    R, L = cls2d.shape
    lanes = -(-n_tiles // 128) * 128
    return pl.pallas_call(
        functools.partial(_rank_kernel, TG=TG),
        out_shape=(jax.ShapeDtypeStruct((R, L), jnp.int32),
                   jax.ShapeDtypeStruct((1, lanes), jnp.int32)),
        name="rank",
    )(cls2d)


def _row_copy(src, s, dst, d, sem):
    return pltpu.make_async_copy(src.at[pl.ds(s, 1)], dst.at[pl.ds(d, 1)], sem)


def _dispatch_kernel(pos_ref, h2_ref, xs_in_hbm, xs_hbm, sem, *, TD):
    del xs_in_hbm

    def issue(r, carry):
        _row_copy(h2_ref, r, xs_hbm, pos_ref[r], sem).start()
        return carry

    lax.fori_loop(0, TD, issue, 0, unroll=8)
    pltpu.make_async_copy(h2_ref, xs_hbm.at[pl.ds(0, TD)], sem).wait()


def _dispatch(pos, h2, xs_zero, *, TD):
    N, D = h2.shape
    return pl.pallas_call(
        functools.partial(_dispatch_kernel, TD=TD),
        grid=(N // TD,),
        in_specs=[pl.BlockSpec((TD,), lambda i: (i,), memory_space=pltpu.SMEM),
                  pl.BlockSpec((TD, D), lambda i: (i, 0)),
                  pl.BlockSpec(memory_space=pl.ANY)],
        out_specs=pl.BlockSpec(memory_space=pl.ANY),
        out_shape=jax.ShapeDtypeStruct(xs_zero.shape, F32),
        scratch_shapes=[pltpu.SemaphoreType.DMA(())],
        input_output_aliases={2: 0},
        compiler_params=pltpu.CompilerParams(
            dimension_semantics=("arbitrary",), vmem_limit_bytes=V7X_VMEM_LIMIT),
        name="dispatch",
    )(pos, h2, xs_zero)


def _expert_kernel(ea_ref, eb_ref, tc_ref, xs_ref, wr_ref, br_ref, wga_ref, wua_ref, wda_ref,
                   wgb_ref, wub_ref, wdb_ref, ys_ref):
    j = pl.program_id(0)
    TG = xs_ref.shape[0]

    @pl.when(tc_ref[j] < N_CLASSES)
    def _():
        a = ea_ref[j]
        b = eb_ref[j]
        g = lax.div(a, jnp.int32(EXP_PER_GROUP))
        h = xs_ref[...]
        logits = _router_logits(h, wr_ref, br_ref)
        lane = lax.broadcasted_iota(jnp.int32, (TG, ROUTE_LANES), 1)
        isg = lane < N_GROUPS
        m = jnp.max(jnp.where(isg, logits, -jnp.inf), axis=1, keepdims=True)
        ssum = jnp.sum(jnp.where(isg, jnp.exp(logits - m), 0.0), axis=1, keepdims=True)
        lg = jnp.sum(jnp.where(lane == g, logits, 0.0), axis=1, keepdims=True)
        la = jnp.sum(jnp.where(lane == N_GROUPS + a, logits, 0.0), axis=1, keepdims=True)
        lb = jnp.sum(jnp.where(lane == N_GROUPS + b, logits, 0.0), axis=1, keepdims=True)
        g_w = jnp.exp(lg - m) / ssum
        mm = jnp.maximum(la, lb)
        pa = jnp.exp(la - mm)
        pb = jnp.exp(lb - mm)
        den = g_w / (pa + pb)
        hb = h.astype(BF16)
        hid_a = _silu(_dot(hb, wga_ref[0])) * _dot(hb, wua_ref[0])
        out = (pa * den) * _dot(hid_a.astype(BF16), wda_ref[0])
        hid_b = _silu(_dot(hb, wgb_ref[0])) * _dot(hb, wub_ref[0])
        ys_ref[...] = out + (pb * den) * _dot(hid_b.astype(BF16), wdb_ref[0])

    @pl.when(tc_ref[j] >= N_CLASSES)
    def _():
        ys_ref[...] = jnp.zeros_like(ys_ref)


def _experts(ea, eb, tcls, xs, wr, br, wg, wu, wd, *, TG):
    NS, D = xs.shape
    wmap_a = lambda j, ea, eb, tc: (ea[j], 0, 0)
    wmap_b = lambda j, ea, eb, tc: (eb[j], 0, 0)
    grid_spec = pltpu.PrefetchScalarGridSpec(
        num_scalar_prefetch=3,
        grid=(NS // TG,),
        in_specs=[
            pl.BlockSpec((TG, D), lambda j, ea, eb, tc: (j, 0)),
            pl.BlockSpec((2, D, ROUTE_LANES), lambda j, ea, eb, tc: (0, 0, 0)),
            pl.BlockSpec((1, ROUTE_LANES), lambda j, ea, eb, tc: (0, 0)),
            pl.BlockSpec((1, D, D_EXPERT), wmap_a),
            pl.BlockSpec((1, D, D_EXPERT), wmap_a),
            pl.BlockSpec((1, D_EXPERT, D), wmap_a),
            pl.BlockSpec((1, D, D_EXPERT), wmap_b),
            pl.BlockSpec((1, D, D_EXPERT), wmap_b),
            pl.BlockSpec((1, D_EXPERT, D), wmap_b),
        ],
        out_specs=pl.BlockSpec((TG, D), lambda j, ea, eb, tc: (j, 0)),
    )
    return pl.pallas_call(
        _expert_kernel,
        grid_spec=grid_spec,
        out_shape=jax.ShapeDtypeStruct((NS, D), F32),
        compiler_params=pltpu.CompilerParams(
            dimension_semantics=("arbitrary",), vmem_limit_bytes=V7X_VMEM_LIMIT),
        name="experts",
    )(ea, eb, tcls, xs, wr, br, wg, wu, wd, wg, wu, wd)


def _combine_kernel(pos_ref, posn_ref, x1_ref, mod_ref, fg_ref, ys_hbm, y_ref, buf, sems, *, TC):
    i = pl.program_id(0)
    n = pl.num_programs(0)
    slot = i % 2

    def gather(p_ref, s):
        def issue(r, carry):
            _row_copy(ys_hbm, p_ref[r], buf.at[s], r, sems.at[s]).start()
            return carry
        lax.fori_loop(0, TC, issue, 0, unroll=8)

    @pl.when(i == 0)
    def _():
        gather(pos_ref, 0)

    @pl.when(i + 1 < n)
    def _():
        gather(posn_ref, 1 - slot)

    pltpu.make_async_copy(ys_hbm.at[pl.ds(0, TC)], buf.at[slot], sems.at[slot]).wait()
    gt2 = mod_ref[0][:, 5 * D_MODEL:6 * D_MODEL]
    x2 = x1_ref[...] + gt2 * buf[slot]
    y_ref[...] = x2 * lax.rsqrt(jnp.mean(x2 * x2, axis=-1, keepdims=True) + EPS) * fg_ref[...]


def _combine(pos, x1, mod, fg, ys, *, T, TC):
    N, D = x1.shape
    n = N // TC
    tiles_per_seq = T // TC
    return pl.pallas_call(
        functools.partial(_combine_kernel, TC=TC),
        grid=(n,),
        in_specs=[pl.BlockSpec((TC,), lambda i: (i,), memory_space=pltpu.SMEM),
                  pl.BlockSpec((TC,), lambda i: (jnp.minimum(i + 1, n - 1),),
                               memory_space=pltpu.SMEM),
                  pl.BlockSpec((TC, D), lambda i: (i, 0)),
                  pl.BlockSpec((1, 1, 6 * D), lambda i: (i // tiles_per_seq, 0, 0)),
                  pl.BlockSpec((1, D), lambda i: (0, 0)),
                  pl.BlockSpec(memory_space=pl.ANY)],
        out_specs=pl.BlockSpec((TC, D), lambda i: (i, 0)),
        out_shape=jax.ShapeDtypeStruct((N, D), F32),
        scratch_shapes=[pltpu.VMEM((2, TC, D), F32), pltpu.SemaphoreType.DMA((2,))],
        compiler_params=pltpu.CompilerParams(
            dimension_semantics=("arbitrary",), vmem_limit_bytes=V7X_VMEM_LIMIT),
        name="combine",
    )(pos, pos, x1, mod, fg, ys)


def _routed_moe(x1, h2, cls, mod, p, *, T):
    N, D = x1.shape
    TG = ROUTED_TILE
    n_tiles = N // TG + N_CLASSES
    pos2d, tcls = _rank(cls.reshape(N // 128, 128), TG=TG, n_tiles=n_tiles)
    pos = pos2d.reshape(N)
    tcls = tcls[0, :n_tiles]
    ea = jnp.asarray(CLASS_EXPERT_A)[tcls]
    eb = jnp.asarray(CLASS_EXPERT_B)[tcls]
    xs = _dispatch(pos, h2, jnp.zeros((n_tiles * TG, D), F32), TD=DISPATCH_TILE)
    ys = _experts(ea, eb, tcls, xs, p["wr"], p["br"], p["wg"], p["wu"], p["wd"], TG=TG)
    return _combine(pos, x1, mod, p["fg"], ys, T=T, TC=COMBINE_TILE)


def _tiling(T):
    if T <= REF_CHUNK:
        return T, T
    return MIXER_CHUNK, min(T, MIXER_TILE)


def _trunk(x, mod, s_ret, s_hg, pos0, inv2, p, *, min_s=8):
    B, T, D = x.shape
    N = B * T
    C, TT = _tiling(T)
    min_s = min(min_s, C // 2)
    routed = N >= ROUTED_MIN_TOKENS
    tabs = _tables(inv2, p["lb_logits"], T=T, C=C, pos0=pos0, min_s=min_s)
    mod3 = mod.reshape(B, 1, 6 * D)
    x1, h2, cls, sr, sh = _mixer(x, mod3, p["n1g"], p["win"], p["wout"], tabs, p["retg"], p["hgg"],
                                 s_ret, s_hg, p["n2g"], p["wr"], p["br"], C=C, TT=TT,
                                 min_s=min_s, with_cls=routed)
    x1 = x1.reshape(N, D)
    h2 = h2.reshape(N, D)
    if routed:
        y = _routed_moe(x1, h2, cls, mod3, p, T=T)
    else:
        y = _moe(x1, h2, mod3, p["wr"], p["br"], p["wg"], p["wu"], p["wd"], p["fg"],
                 T=T, TM=min(N, 1024))
    return y.reshape(B, T, D), sr[None], sh[None]


def _prepare(W_ada, b_ada, norm1_g, norm2_g, W_in, ret_norm_g, hgrn_norm_g, hgrn_lb_logits, W_out,
             W_router_group, b_router_group, W_router_expert, b_router_expert, W_gate_e, W_up_e,
             W_down_e, final_norm_g):
    assert W_in.shape[0] == 1, "single-layer stack only"
    D = D_MODEL
    wr = jnp.concatenate([W_router_group[0], W_router_expert[0]], axis=1)
    wr = jnp.pad(wr, ((0, 0), (0, ROUTE_LANES - wr.shape[1])))
    wr1 = wr.astype(BF16)
    wr2 = (wr - wr1.astype(F32)).astype(BF16)
    br = jnp.concatenate([b_router_group[0], b_router_expert[0]])
    br = jnp.pad(br, (0, ROUTE_LANES - br.shape[0])).reshape(1, ROUTE_LANES)
    return dict(
        n1g=norm1_g[0].reshape(1, D), n2g=norm2_g[0].reshape(1, D), fg=final_norm_g.reshape(1, D),
        win=W_in[0].astype(BF16), wout=W_out[0].astype(BF16),
        retg=ret_norm_g[0].reshape(1, -1), hgg=hgrn_norm_g[0].reshape(1, -1),
        lb_logits=hgrn_lb_logits,
        wr=jnp.stack([wr1, wr2]), br=br,
        wg=W_gate_e[0].astype(BF16), wu=W_up_e[0].astype(BF16), wd=W_down_e[0].astype(BF16),
    )


def _rope_inv():
    half = DH // 2
    inv = 1.0 / (ROPE_BASE ** (jnp.arange(half, dtype=F32) / half))
    return jnp.concatenate([inv, inv]).reshape(1, DH)


def kernel(x_prompt, x_sample, c_prompt, c_sample, state_ret, state_hgrn, W_ada, b_ada, norm1_g,
           norm2_g, W_in, ret_norm_g, hgrn_norm_g, hgrn_lb_logits, W_out, W_router_group,
           b_router_group, W_router_expert, b_router_expert, W_gate_e, W_up_e, W_down_e,
           final_norm_g):
    p = _prepare(W_ada, b_ada, norm1_g, norm2_g, W_in, ret_norm_g, hgrn_norm_g, hgrn_lb_logits,
                 W_out, W_router_group, b_router_group, W_router_expert, b_router_expert,
                 W_gate_e, W_up_e, W_down_e, final_norm_g)
    Bp = x_prompt.shape[0]
    Bs = x_sample.shape[0]
    inv2 = _rope_inv()
    mod = _ada(jnp.concatenate([c_prompt, c_sample], axis=0), W_ada[0], b_ada[0].reshape(1, -1))
    zeros = jnp.zeros((Bp, HEADS, DH, DH), F32)
    y_p, sr_p, sh_p = _trunk(x_prompt, mod[:Bp], zeros, zeros, 0, inv2, p)
    y_s, sr_s, sh_s = _trunk(x_sample, mod[Bp:Bp + Bs], state_ret[0], state_hgrn[0], PAST_LEN,
                             inv2, p)
    return (y_p, y_s, sr_p, sh_p, sr_s, sh_s)
```

```python
import functools

import numpy as np
import jax
import jax.numpy as jnp
from jax import lax
from jax.experimental import pallas as pl
from jax.experimental.pallas import tpu as pltpu

F32 = jnp.float32
BF16 = jnp.bfloat16

D_MODEL = 1024
HEADS = 4
DH = 128
D_IN = 8 * HEADS * DH
N_GROUPS = 4
EXP_PER_GROUP = 4
N_EXPERTS = N_GROUPS * EXP_PER_GROUP
D_EXPERT = 512
ROPE_BASE = 10000.0
EPS = 1e-6
REF_CHUNK = 64
PAST_LEN = 4096

V7X_VMEM_LIMIT = 56 * 1024 * 1024
ROUTE_LANES = 128

_PAIRS = [(i, j) for i in range(EXP_PER_GROUP) for j in range(i + 1, EXP_PER_GROUP)]
N_PAIRS = len(_PAIRS)
N_CLASSES = N_GROUPS * N_PAIRS
CLASS_EXPERT_A = np.array([g * EXP_PER_GROUP + i for g in range(N_GROUPS) for i, _ in _PAIRS]
                          + [N_EXPERTS - 2], np.int32)
CLASS_EXPERT_B = np.array([g * EXP_PER_GROUP + j for g in range(N_GROUPS) for _, j in _PAIRS]
                          + [N_EXPERTS - 1], np.int32)
MIXER_CHUNK = 128
MIXER_TILE = 512
MIXER_PROJ_ROWS = 256
ROUTED_MIN_TOKENS = 8192
ROUTED_TILE = 512
DISPATCH_TILE = 1024
COMBINE_TILE = 512


def _dot(a, b):
    return jnp.dot(a, b, preferred_element_type=F32)


def _dot_nt(a, b):
    return lax.dot_general(a, b, (((1,), (1,)), ((), ())), preferred_element_type=F32)


def _dot_tn(a, b):
    return lax.dot_general(a, b, (((0,), (0,)), ((), ())), preferred_element_type=F32)


def _split3(a):
    a1 = a.astype(BF16)
    r1 = a - a1.astype(F32)
    a2 = r1.astype(BF16)
    a3 = (r1 - a2.astype(F32)).astype(BF16)
    return a1, a2, a3


def _silu(a):
    return a * (1.0 / (1.0 + jnp.exp(-a)))


def _tables_kernel(inv_ref, lbl_ref, cos_ref, sin_ref, rd_ref, idec_ref, kdec_ref, lb_ref,
                   lvl_ref, tri_ref, *, T, C, pos0, min_s):
    pos = (lax.broadcasted_iota(jnp.int32, (T, DH), 0) + pos0).astype(F32)
    ang = pos * inv_ref[...]
    lane = lax.broadcasted_iota(jnp.int32, (T, DH), 1)
    cos_ref[...] = jnp.cos(ang)
    s = jnp.sin(ang)
    sin_ref[...] = jnp.where(lane < DH // 2, -s, s)

    li = lax.broadcasted_iota(jnp.int32, (C, C), 0)
    mi = lax.broadcasted_iota(jnp.int32, (C, C), 1)
    diff = (li - mi).astype(F32)
    rowj = lax.broadcasted_iota(jnp.int32, (C, DH), 0).astype(F32)
    for h in range(HEADS):
        lg = jnp.log(jnp.full((1, 1), 1.0 - 2.0 ** (-5.0 - h), F32))
        rd_ref[h] = jnp.where(diff >= 0, jnp.exp(jnp.maximum(diff, 0.0) * lg), 0.0)
        idec_ref[h] = jnp.exp((rowj + 1.0) * lg)
        kdec_ref[h] = jnp.exp((C - 1.0 - rowj) * lg)

    lg_all = lbl_ref[...]
    m = jnp.max(lg_all, axis=0, keepdims=True)
    e = jnp.exp(lg_all - m)
    lb_ref[...] = e[0:1, :] / jnp.sum(e, axis=0, keepdims=True)

    x = li ^ mi
    code = jnp.zeros((C, C), jnp.int32)
    sz = min_s
    while sz < C:
        code = code + (x >= sz).astype(jnp.int32)
        sz *= 2
    code = jnp.where(x >= min_s, code + (min_s.bit_length() - 1), 0)
    lvl_ref[...] = jnp.where(li >= mi, code, -1)
    tri_ref[...] = (li >= mi).astype(BF16)


def _tables(inv2, lb_logits, *, T, C, pos0, min_s):
    out_shape = (
        jax.ShapeDtypeStruct((T, DH), F32),
        jax.ShapeDtypeStruct((T, DH), F32),
        jax.ShapeDtypeStruct((HEADS, C, C), F32),
        jax.ShapeDtypeStruct((HEADS, C, DH), F32),
        jax.ShapeDtypeStruct((HEADS, C, DH), F32),
        jax.ShapeDtypeStruct((1, HEADS * DH), F32),
        jax.ShapeDtypeStruct((C, C), jnp.int32),
        jax.ShapeDtypeStruct((C, C), BF16),
    )
    return pl.pallas_call(
        functools.partial(_tables_kernel, T=T, C=C, pos0=pos0, min_s=min_s),
        out_shape=out_shape,
        name="tables",
    )(inv2, lb_logits)


def _ada_kernel(c_ref, w_ref, b_ref, o_ref):
    a = _silu(c_ref[...])
    w = w_ref[...]
    a1, a2, a3 = _split3(a)
    w1, w2, w3 = _split3(w)
    acc = _dot(a1, w1) + (_dot(a1, w2) + _dot(a2, w1)) + (_dot(a1, w3) + _dot(a2, w2) + _dot(a3, w1))
    o_ref[...] = acc + b_ref[...]


def _ada(c, w, b):
    R = c.shape[0]
    N = w.shape[1]
    TN = 512
    return pl.pallas_call(
        _ada_kernel,
        grid=(N // TN,),
        in_specs=[pl.BlockSpec((R, D_MODEL), lambda j: (0, 0)),
                  pl.BlockSpec((D_MODEL, TN), lambda j: (0, j)),
                  pl.BlockSpec((1, TN), lambda j: (0, j))],
        out_specs=pl.BlockSpec((R, TN), lambda j: (0, j)),
        out_shape=jax.ShapeDtypeStruct((R, N), F32),
        name="ada",
    )(c, w, b)


def _block_ref(b, block, idx):
    C, W = b.shape
    if block >= 8:
        b3 = b.reshape(C // block, block, W)
        r = jnp.broadcast_to(b3[:, idx:idx + 1, :], b3.shape)
        return r.reshape(C, W)
    row = lax.broadcasted_iota(jnp.int32, (C, W), 0)
    p = row % block
    out = b
    for q in range(block):
        sh = q - idx
        if sh == 0:
            continue
        rolled = pltpu.roll(b, sh % C, 0)
        out = jnp.where(p == q, rolled, out)
    return out


def _head_norm_gate(o, g_row, gate):
    y = o * lax.rsqrt(jnp.mean(o * o, axis=-1, keepdims=True) + EPS)
    return (y * g_row) * _silu(gate)


def _mixer_chunk(r0, proj_scr, cos_ref, sin_ref, rd_ref, idec_ref, kdec_ref, lb_ref, lvl_ref,
                 tri_ref, retg_ref, hgg_ref, sr_scr, sht_scr, o_scr, *, C, min_s):
    W = HEADS * DH
    rows = pl.ds(r0, C)
    proj = proj_scr[rows, :]
    cosf = cos_ref[rows, :]
    sinf = sin_ref[rows, :]

    for h in range(HEADS):
        c0 = h * DH
        q = proj[:, c0:c0 + DH]
        k = proj[:, W + c0:W + c0 + DH]
        v = proj[:, 2 * W + c0:2 * W + c0 + DH].astype(BF16)
        g = proj[:, 3 * W + c0:3 * W + c0 + DH]
        qr = (q * cosf + pltpu.roll(q, DH // 2, 1) * sinf)
        kr = (k * cosf + pltpu.roll(k, DH // 2, 1) * sinf) * (DH ** -0.5)
        qb = qr.astype(BF16)
        A = _dot_nt(qb, kr.astype(BF16)) * rd_ref[h]
        S = sr_scr[h]
        idec = idec_ref[h]
        o = _dot(A.astype(BF16), v) + _dot(qb, S.astype(BF16)) * idec
        sr_scr[h] = idec[C - 1:C, :] * S + _dot_tn((kr * kdec_ref[h]).astype(BF16), v)
        o_scr[rows, c0:c0 + DH] = _head_norm_gate(o, retg_ref[:, c0:c0 + DH], g).astype(BF16)

    z = proj[:, 5 * W:6 * W]
    lb = lb_ref[...]
    e = jnp.exp(-jnp.abs(z))
    r = 1.0 / (1.0 + e)
    er = e * r
    sig = jnp.where(z >= 0, r, er)
    nsig = jnp.where(z >= 0, er, r)
    logf = jnp.log(lb + (1.0 - lb) * sig)
    kk_all = (1.0 - lb) * nsig
    tri = tri_ref[...]
    l1, l2, l3 = _split3(logf)
    b_all = _dot(tri, l1) + _dot(tri, l2) + _dot(tri, l3)

    lvl = lvl_ref[...]
    rowc = lax.broadcasted_iota(jnp.int32, (C, DH), 0)
    for h in range(HEADS):
        c0 = h * DH
        q = proj[:, 4 * W + c0:4 * W + c0 + DH] * (DH ** -0.5)
        kk = kk_all[:, c0:c0 + DH]
        b = b_all[:, c0:c0 + DH]
        v = proj[:, 6 * W + c0:6 * W + c0 + DH]
        vb = v.astype(BF16)
        g = proj[:, 7 * W + c0:7 * W + c0 + DH]

        A = jnp.zeros((C, C), F32)
        s = C // 2
        while s >= min_s:
            isq = (rowc % (2 * s)) >= s
            ref = _block_ref(b, 2 * s, s - 1)
            d = b - ref
            zz = jnp.where(isq, q, kk) * jnp.exp(jnp.where(isq, d, -d))
            zb = zz.astype(BF16)
            A = jnp.where(lvl == s.bit_length(), _dot_nt(zb, zb), A)
            s //= 2
        if min_s > 1:
            ref = _block_ref(b, min_s, 0)
            d = b - ref
            P = _dot_nt((q * jnp.exp(d)).astype(BF16), (kk * jnp.exp(-d)).astype(BF16))
            A = jnp.where(lvl == 0, P, A)
            o = _dot(A.astype(BF16), vb)
        else:
            o = _dot(A.astype(BF16), vb) + jnp.sum(q * kk, axis=-1, keepdims=True) * v

        ST = sht_scr[h]
        bl = b[C - 1:C, :]
        o = o + _dot_nt((q * jnp.exp(b)).astype(BF16), ST.astype(BF16))
        kdec = kk * jnp.exp(bl - b)
        sht_scr[h] = ST * jnp.exp(bl) + _dot_tn(vb, kdec.astype(BF16))
        o_scr[rows, W + c0:W + c0 + DH] = _head_norm_gate(o, hgg_ref[:, c0:c0 + DH], g).astype(BF16)


def _mixer_kernel(x_ref, xnext_ref, mod_ref, n1g_ref, win_ref, wout_ref, cos_ref, sin_ref, rd_ref,
                  idec_ref, kdec_ref, lb_ref, lvl_ref, tri_ref, retg_ref, hgg_ref,
                  sr0_ref, sh0_ref, n2g_ref, wr_ref, br_ref,
                  x1_ref, h2_ref, cls_ref, sro_ref, sho_ref,
                  sr_scr, sht_scr, proj_scr, o_scr, *, C, min_s, with_cls, has_next):
    t = pl.program_id(1)
    nt = pl.num_programs(1)
    TT = x_ref.shape[1]
    n_chunks = TT // C

    mod = mod_ref[0]
    sh1 = mod[:, 0:D_MODEL]
    sc1 = mod[:, D_MODEL:2 * D_MODEL]
    gt1 = mod[:, 2 * D_MODEL:3 * D_MODEL]

    PR = xnext_ref.shape[1]
    per_block = PR // C

    def project(xc, r0):
        xn = xc * lax.rsqrt(jnp.mean(xc * xc, axis=-1, keepdims=True) + EPS) * n1g_ref[...]
        hmod = xn * (1.0 + sc1) + sh1
        proj_scr[r0:r0 + PR, :] = _dot(hmod.astype(BF16), win_ref[...])

    @pl.when(t == 0)
    def _():
        for h in range(HEADS):
            sr_scr[h] = sr0_ref[0, h]
            sht_scr[h] = sh0_ref[0, h].T
        project(x_ref[0, 0:PR, :], 0)

    chunk = functools.partial(
        _mixer_chunk, proj_scr=proj_scr, cos_ref=cos_ref, sin_ref=sin_ref, rd_ref=rd_ref,
        idec_ref=idec_ref, kdec_ref=kdec_ref, lb_ref=lb_ref, lvl_ref=lvl_ref, tri_ref=tri_ref,
        retg_ref=retg_ref, hgg_ref=hgg_ref, sr_scr=sr_scr, sht_scr=sht_scr, o_scr=o_scr,
        C=C, min_s=min_s)
    for c in range(n_chunks):
        chunk(c * C)
        if c % per_block == 0:
            r1 = (c // per_block + 1) * PR
            if r1 < TT:
                project(x_ref[0, r1:r1 + PR, :], r1)
            elif has_next:
                project(xnext_ref[0], 0)

    x = x_ref[0]
    mix = _dot(o_scr[...], wout_ref[...])
    x1 = x + gt1 * mix
    x1_ref[0] = x1

    sh2 = mod[:, 3 * D_MODEL:4 * D_MODEL]
    sc2 = mod[:, 4 * D_MODEL:5 * D_MODEL]
    x1n = x1 * lax.rsqrt(jnp.mean(x1 * x1, axis=-1, keepdims=True) + EPS) * n2g_ref[...]
    h2 = x1n * (1.0 + sc2) + sh2
    h2_ref[0] = h2
    if with_cls:
        g_idx, i1, i2 = _route_topk(_router_logits(h2, wr_ref, br_ref).T)[:3]
        lo = jnp.minimum(i1, i2)
        hi = jnp.maximum(i1, i2)
        pair = hi - 1.0 + jnp.where(lo == 1.0, 2.0, 0.0) + jnp.where(lo == 2.0, 3.0, 0.0)
        cls_ref[0] = (g_idx * float(N_PAIRS) + pair).astype(jnp.int32)
    else:
        cls_ref[0] = jnp.zeros((1, TT), jnp.int32)

    @pl.when(t == nt - 1)
    def _():
        for h in range(HEADS):
            sro_ref[0, h] = sr_scr[h]
            sho_ref[0, h] = sht_scr[h].T


def _mixer(x, mod, n1g, win, wout, tabs, retg, hgg, sr0, sh0, n2g, wr, br, *, C, TT, min_s,
           with_cls):
    B, T, D = x.shape
    cosf, sinf, rd, idec, kdec, lb, lvl, tri = tabs
    nt = T // TT
    const2 = lambda b, t: (0, 0)
    const3 = lambda b, t: (0, 0, 0)
    state_spec = pl.BlockSpec((1, HEADS, DH, DH), lambda b, t: (b, 0, 0, 0))
    PR = min(TT, MIXER_PROJ_ROWS)
    per_tile = TT // PR
    last_block = T // PR - 1
    in_specs = [
        pl.BlockSpec((1, TT, D), lambda b, t: (b, t, 0)),
        pl.BlockSpec((1, PR, D),
                     lambda b, t: (b, jnp.minimum((t + 1) * per_tile, last_block), 0)),
        pl.BlockSpec((1, 1, 6 * D), lambda b, t: (b, 0, 0)),
        pl.BlockSpec((1, D), const2),
        pl.BlockSpec((D, D_IN), const2),
        pl.BlockSpec((D, D), const2),
        pl.BlockSpec((TT, DH), lambda b, t: (t, 0)),
        pl.BlockSpec((TT, DH), lambda b, t: (t, 0)),
        pl.BlockSpec((HEADS, C, C), const3),
        pl.BlockSpec((HEADS, C, DH), const3),
        pl.BlockSpec((HEADS, C, DH), const3),
        pl.BlockSpec((1, HEADS * DH), const2),
        pl.BlockSpec((C, C), const2),
        pl.BlockSpec((C, C), const2),
        pl.BlockSpec((1, HEADS * DH), const2),
        pl.BlockSpec((1, HEADS * DH), const2),
        state_spec, state_spec,
        pl.BlockSpec((1, D), const2),
        pl.BlockSpec((D, 2 * ROUTE_LANES), const2),
        pl.BlockSpec((1, ROUTE_LANES), const2),
    ]
    tok_spec = pl.BlockSpec((1, TT, D), lambda b, t: (b, t, 0))
    out_specs = [tok_spec, tok_spec,
                 pl.BlockSpec((1, 1, TT), lambda b, t: (b * nt + t, 0, 0)),
                 state_spec, state_spec]
    out_shape = [jax.ShapeDtypeStruct((B, T, D), F32),
                 jax.ShapeDtypeStruct((B, T, D), F32),
                 jax.ShapeDtypeStruct((B * nt, 1, TT), jnp.int32),
                 jax.ShapeDtypeStruct((B, HEADS, DH, DH), F32),
                 jax.ShapeDtypeStruct((B, HEADS, DH, DH), F32)]
    return pl.pallas_call(
        functools.partial(_mixer_kernel, C=C, min_s=min_s, with_cls=with_cls, has_next=nt > 1),
        grid=(B, nt),
        in_specs=in_specs,
        out_specs=out_specs,
        out_shape=out_shape,
        scratch_shapes=[pltpu.VMEM((HEADS, DH, DH), F32),
                        pltpu.VMEM((HEADS, DH, DH), F32),
                        pltpu.VMEM((TT, D_IN), F32),
                        pltpu.VMEM((TT, D), BF16)],
        compiler_params=pltpu.CompilerParams(
            dimension_semantics=("arbitrary", "arbitrary"),
            vmem_limit_bytes=V7X_VMEM_LIMIT),
        name="mixer",
    )(x, x, mod, n1g, win, wout, cosf, sinf, rd, idec, kdec, lb, lvl, tri, retg, hgg, sr0, sh0,
      n2g, wr, br)


def _router_logits(h, wr_ref, br_ref):
    h1, h2, _ = _split3(h)
    hi = _dot(h1, wr_ref[...])
    lo = _dot(h2, wr_ref[:, 0:ROUTE_LANES])
    return hi[:, 0:ROUTE_LANES] + (hi[:, ROUTE_LANES:] + lo) + br_ref[...]


def _route_topk(logits_t):
    TM = logits_t.shape[1]
    lg = logits_t[0:N_GROUPS, :]
    gi = lax.broadcasted_iota(jnp.int32, (N_GROUPS, TM), 0).astype(F32)
    m = jnp.max(lg, axis=0, keepdims=True)
    g_w = 1.0 / jnp.sum(jnp.exp(lg - m), axis=0, keepdims=True)
    g_idx = jnp.min(jnp.where(lg == m, gi, float(N_GROUPS)), axis=0, keepdims=True)
    el = jnp.zeros((EXP_PER_GROUP, TM), F32)
    for g in range(N_GROUPS):
        lo = N_GROUPS + g * EXP_PER_GROUP
        el = jnp.where(g_idx == float(g), logits_t[lo:lo + EXP_PER_GROUP, :], el)
    ei = lax.broadcasted_iota(jnp.int32, (EXP_PER_GROUP, TM), 0).astype(F32)
    m1 = jnp.max(el, axis=0, keepdims=True)
    i1 = jnp.min(jnp.where(el == m1, ei, float(EXP_PER_GROUP)), axis=0, keepdims=True)
    el2 = jnp.where(ei == i1, -jnp.inf, el)
    m2 = jnp.max(el2, axis=0, keepdims=True)
    i2 = jnp.min(jnp.where(el2 == m2, ei, float(EXP_PER_GROUP)), axis=0, keepdims=True)
    p2 = jnp.exp(m2 - m1)
    den = 1.0 / (1.0 + p2)
    w0 = g_w * den
    w1 = g_w * (p2 * den)
    return g_idx, i1, i2, w0, w1


def _moe_kernel(x1_ref, h2_ref, mod_ref, wr_ref, br_ref, wg_ref, wu_ref, wd_ref, fg_ref,
                y_ref, h_scr, route_scr, acc_scr):
    e = pl.program_id(1)
    ne = pl.num_programs(1)
    TM, D = x1_ref.shape
    K = mod_ref.shape[0]
    TQ = TM // K

    @pl.when(e == 0)
    def _():
        h = h2_ref[...]
        h_scr[...] = h.astype(BF16)
        g_idx, i1, i2, w0, wgt1 = _route_topk(_router_logits(h, wr_ref, br_ref).T)
        e0 = g_idx * EXP_PER_GROUP + i1
        e1 = g_idx * EXP_PER_GROUP + i2
        ri = lax.broadcasted_iota(jnp.int32, (ROUTE_LANES, TM), 0)
        rows = jnp.where(ri == 0, e0, jnp.where(ri == 1, e1, jnp.where(ri == 2, w0,
                         jnp.where(ri == 3, wgt1, 0.0))))
        route_scr[...] = rows.T
        acc_scr[...] = jnp.zeros_like(acc_scr)

    hb = h_scr[...]
    hid = _silu(_dot(hb, wg_ref[0])) * _dot(hb, wu_ref[0])
    out = _dot(hid.astype(BF16), wd_ref[0])
    ef = e.astype(F32)
    route = route_scr[...]
    comb = (jnp.where(route[:, 0:1] == ef, route[:, 2:3], 0.0)
            + jnp.where(route[:, 1:2] == ef, route[:, 3:4], 0.0))
    acc_scr[...] += comb * out

    @pl.when(e == ne - 1)
    def _():
        gt2 = mod_ref[...][:, :, 5 * D:6 * D]
        ff = (gt2 * acc_scr[...].reshape(K, TQ, D)).reshape(TM, D)
        x2 = x1_ref[...] + ff
        y_ref[...] = x2 * lax.rsqrt(jnp.mean(x2 * x2, axis=-1, keepdims=True) + EPS) * fg_ref[...]


def _moe(x1, h2, mod, wr, br, wg, wu, wd, fg, *, T, TM):
    N, D = x1.shape
    if T >= TM:
        assert T % TM == 0
        seqs, tiles_per_seq = 1, T // TM
        mod_map = lambda i, e: (i // tiles_per_seq, 0, 0)
    else:
        assert TM % T == 0
        seqs = TM // T
        mod_map = lambda i, e: (i, 0, 0)
    return pl.pallas_call(
        _moe_kernel,
        grid=(N // TM, N_EXPERTS),
        in_specs=[
            pl.BlockSpec((TM, D), lambda i, e: (i, 0)),
            pl.BlockSpec((TM, D), lambda i, e: (i, 0)),
            pl.BlockSpec((seqs, 1, 6 * D), mod_map),
            pl.BlockSpec((D, 2 * ROUTE_LANES), lambda i, e: (0, 0)),
            pl.BlockSpec((1, ROUTE_LANES), lambda i, e: (0, 0)),
            pl.BlockSpec((1, D, D_EXPERT), lambda i, e: (e, 0, 0)),
            pl.BlockSpec((1, D, D_EXPERT), lambda i, e: (e, 0, 0)),
            pl.BlockSpec((1, D_EXPERT, D), lambda i, e: (e, 0, 0)),
            pl.BlockSpec((1, D), lambda i, e: (0, 0)),
        ],
        out_specs=pl.BlockSpec((TM, D), lambda i, e: (i, 0)),
        out_shape=jax.ShapeDtypeStruct((N, D), F32),
        scratch_shapes=[pltpu.VMEM((TM, D), BF16),
                        pltpu.VMEM((TM, ROUTE_LANES), F32),
                        pltpu.VMEM((TM, D), F32)],
        compiler_params=pltpu.CompilerParams(
            dimension_semantics=("arbitrary", "arbitrary"),
            vmem_limit_bytes=V7X_VMEM_LIMIT),
        name="moe",
    )(x1, h2, mod, wr, br, wg, wu, wd, fg)


def _rank_kernel(cls_ref, pos_ref, tcls_ref, *, TG):
    R, L = cls_ref.shape
    cf = cls_ref[...].astype(F32)
    ri = lax.broadcasted_iota(jnp.int32, (L, L), 0)
    ci = lax.broadcasted_iota(jnp.int32, (L, L), 1)
    upper = (ri < ci).astype(BF16)
    rr = lax.broadcasted_iota(jnp.int32, (R, R), 0)
    rc = lax.broadcasted_iota(jnp.int32, (R, R), 1)
    lower = (rc < rr).astype(BF16)
    tile_start = lax.broadcasted_iota(jnp.int32, tcls_ref.shape, 1).astype(F32) * float(TG)
    base = jnp.zeros((1, 1), F32)
    pos = jnp.zeros((R, L), F32)
    tcls = jnp.zeros(tcls_ref.shape, F32)
    for c in range(N_CLASSES):
        ind = jnp.where(cf == float(c), 1.0, 0.0)
        lane_pre = _dot(ind.astype(BF16), upper)
        row_tot = jnp.broadcast_to(jnp.sum(ind, axis=1, keepdims=True), (R, L))
        row_pre = _dot(lower, row_tot.astype(BF16))
        total = jnp.sum(row_tot[:, 0:1], axis=0, keepdims=True)
        pos = jnp.where(ind > 0.0, base + row_pre + lane_pre, pos)
        base = base + jnp.floor((total + float(TG - 1)) * (1.0 / TG)) * float(TG)
        tcls = tcls + jnp.where(tile_start >= base, 1.0, 0.0)
    pos_ref[...] = pos.astype(jnp.int32)
    tcls_ref[...] = tcls.astype(jnp.int32)


def _rank(cls2d, *, TG, n_tiles):
    R, L = cls2d.shape
    lanes = -(-n_tiles // 128) * 128
    return pl.pallas_call(
        functools.partial(_rank_kernel, TG=TG),
        out_shape=(jax.ShapeDtypeStruct((R, L), jnp.int32),
                   jax.ShapeDtypeStruct((1, lanes), jnp.int32)),
        name="rank",
    )(cls2d)


SUBLANES = 8


def _for_each_row(n_rows, fn):
    def group(g, carry):
        r0 = pl.multiple_of(g * SUBLANES, SUBLANES)
        for j in range(SUBLANES):
            fn(pl.ds(r0, SUBLANES), j, r0 + j)
        return carry
    lax.fori_loop(0, n_rows // SUBLANES, group, 0)


def _dispatch_kernel(pos_ref, h2_ref, xs_in_hbm, xs_hbm, sem, *, TD):
    del xs_in_hbm

    def issue(tile, j, r):
        pltpu.make_async_copy(h2_ref.at[tile].at[pl.ds(j, 1)],
                              xs_hbm.at[pl.ds(pos_ref[r], 1)], sem).start()

    _for_each_row(TD, issue)
    pltpu.make_async_copy(h2_ref, xs_hbm.at[pl.ds(0, TD)], sem).wait()


def _dispatch(pos, h2, xs_zero, *, TD):
    N, D = h2.shape
    return pl.pallas_call(
        functools.partial(_dispatch_kernel, TD=TD),
        grid=(N // TD,),
        in_specs=[pl.BlockSpec((TD,), lambda i: (i,), memory_space=pltpu.SMEM),
                  pl.BlockSpec((TD, D), lambda i: (i, 0)),
                  pl.BlockSpec(memory_space=pl.ANY)],
        out_specs=pl.BlockSpec(memory_space=pl.ANY),
        out_shape=jax.ShapeDtypeStruct(xs_zero.shape, F32),
        scratch_shapes=[pltpu.SemaphoreType.DMA(())],
        input_output_aliases={2: 0},
        compiler_params=pltpu.CompilerParams(
            dimension_semantics=("arbitrary",), vmem_limit_bytes=V7X_VMEM_LIMIT),
        name="dispatch",
    )(pos, h2, xs_zero)


def _expert_kernel(ea_ref, eb_ref, tc_ref, xs_ref, wr_ref, br_ref, wga_ref, wua_ref, wda_ref,
                   wgb_ref, wub_ref, wdb_ref, ys_ref):
    j = pl.program_id(0)
    TG = xs_ref.shape[0]

    @pl.when(tc_ref[j] < N_CLASSES)
    def _():
        a = ea_ref[j]
        b = eb_ref[j]
        g = lax.div(a, jnp.int32(EXP_PER_GROUP))
        h = xs_ref[...]
        logits = _router_logits(h, wr_ref, br_ref)
        lane = lax.broadcasted_iota(jnp.int32, (TG, ROUTE_LANES), 1)
        isg = lane < N_GROUPS
        m = jnp.max(jnp.where(isg, logits, -jnp.inf), axis=1, keepdims=True)
        ssum = jnp.sum(jnp.where(isg, jnp.exp(logits - m), 0.0), axis=1, keepdims=True)
        lg = jnp.sum(jnp.where(lane == g, logits, 0.0), axis=1, keepdims=True)
        la = jnp.sum(jnp.where(lane == N_GROUPS + a, logits, 0.0), axis=1, keepdims=True)
        lb = jnp.sum(jnp.where(lane == N_GROUPS + b, logits, 0.0), axis=1, keepdims=True)
        g_w = jnp.exp(lg - m) / ssum
        mm = jnp.maximum(la, lb)
        pa = jnp.exp(la - mm)
        pb = jnp.exp(lb - mm)
        den = g_w / (pa + pb)
        hb = h.astype(BF16)
        hid_a = _silu(_dot(hb, wga_ref[0])) * _dot(hb, wua_ref[0])
        out = (pa * den) * _dot(hid_a.astype(BF16), wda_ref[0])
        hid_b = _silu(_dot(hb, wgb_ref[0])) * _dot(hb, wub_ref[0])
        ys_ref[...] = out + (pb * den) * _dot(hid_b.astype(BF16), wdb_ref[0])

    @pl.when(tc_ref[j] >= N_CLASSES)
    def _():
        ys_ref[...] = jnp.zeros_like(ys_ref)


def _experts(ea, eb, tcls, xs, wr, br, wg, wu, wd, *, TG):
    NS, D = xs.shape
    wmap_a = lambda j, ea, eb, tc: (ea[j], 0, 0)
    wmap_b = lambda j, ea, eb, tc: (eb[j], 0, 0)
    grid_spec = pltpu.PrefetchScalarGridSpec(
        num_scalar_prefetch=3,
        grid=(NS // TG,),
        in_specs=[
            pl.BlockSpec((TG, D), lambda j, ea, eb, tc: (j, 0)),
            pl.BlockSpec((D, 2 * ROUTE_LANES), lambda j, ea, eb, tc: (0, 0)),
            pl.BlockSpec((1, ROUTE_LANES), lambda j, ea, eb, tc: (0, 0)),
            pl.BlockSpec((1, D, D_EXPERT), wmap_a),
            pl.BlockSpec((1, D, D_EXPERT), wmap_a),
            pl.BlockSpec((1, D_EXPERT, D), wmap_a),
            pl.BlockSpec((1, D, D_EXPERT), wmap_b),
            pl.BlockSpec((1, D, D_EXPERT), wmap_b),
            pl.BlockSpec((1, D_EXPERT, D), wmap_b),
        ],
        out_specs=pl.BlockSpec((TG, D), lambda j, ea, eb, tc: (j, 0)),
    )
    return pl.pallas_call(
        _expert_kernel,
        grid_spec=grid_spec,
        out_shape=jax.ShapeDtypeStruct((NS, D), F32),
        compiler_params=pltpu.CompilerParams(
            dimension_semantics=("arbitrary",), vmem_limit_bytes=V7X_VMEM_LIMIT),
        name="experts",
    )(ea, eb, tcls, xs, wr, br, wg, wu, wd, wg, wu, wd)


def _combine_kernel(pos_ref, posn_ref, x1_ref, mod_ref, fg_ref, ys_hbm, y_ref, buf, sems, *, TC):
    i = pl.program_id(0)
    n = pl.num_programs(0)
    slot = i % 2

    def gather(p_ref, s):
        def issue(tile, j, r):
            pltpu.make_async_copy(ys_hbm.at[pl.ds(p_ref[r], 1)],
                                  buf.at[s].at[tile].at[pl.ds(j, 1)], sems.at[s]).start()
        _for_each_row(TC, issue)

    @pl.when(i == 0)
    def _():
        gather(pos_ref, 0)

    @pl.when(i + 1 < n)
    def _():
        gather(posn_ref, 1 - slot)

    pltpu.make_async_copy(ys_hbm.at[pl.ds(0, TC)], buf.at[slot], sems.at[slot]).wait()
    gt2 = mod_ref[0][:, 5 * D_MODEL:6 * D_MODEL]
    x2 = x1_ref[...] + gt2 * buf[slot]
    y_ref[...] = x2 * lax.rsqrt(jnp.mean(x2 * x2, axis=-1, keepdims=True) + EPS) * fg_ref[...]


def _combine(pos, x1, mod, fg, ys, *, T, TC):
    N, D = x1.shape
    n = N // TC
    tiles_per_seq = T // TC
    return pl.pallas_call(
        functools.partial(_combine_kernel, TC=TC),
        grid=(n,),
        in_specs=[pl.BlockSpec((TC,), lambda i: (i,), memory_space=pltpu.SMEM),
                  pl.BlockSpec((TC,), lambda i: (jnp.minimum(i + 1, n - 1),),
                               memory_space=pltpu.SMEM),
                  pl.BlockSpec((TC, D), lambda i: (i, 0)),
                  pl.BlockSpec((1, 1, 6 * D), lambda i: (i // tiles_per_seq, 0, 0)),
                  pl.BlockSpec((1, D), lambda i: (0, 0)),
                  pl.BlockSpec(memory_space=pl.ANY)],
        out_specs=pl.BlockSpec((TC, D), lambda i: (i, 0)),
        out_shape=jax.ShapeDtypeStruct((N, D), F32),
        scratch_shapes=[pltpu.VMEM((2, TC, D), F32), pltpu.SemaphoreType.DMA((2,))],
        compiler_params=pltpu.CompilerParams(
            dimension_semantics=("arbitrary",), vmem_limit_bytes=V7X_VMEM_LIMIT),
        name="combine",
    )(pos, pos, x1, mod, fg, ys)


def _routed_moe(x1, h2, cls, mod, p, *, T):
    N, D = x1.shape
    TG = ROUTED_TILE
    n_tiles = N // TG + N_CLASSES
    pos2d, tcls = _rank(cls.reshape(N // 128, 128), TG=TG, n_tiles=n_tiles)
    pos = pos2d.reshape(N)
    tcls = tcls[0, :n_tiles]
    ea = jnp.asarray(CLASS_EXPERT_A)[tcls]
    eb = jnp.asarray(CLASS_EXPERT_B)[tcls]
    xs = _dispatch(pos, h2, jnp.zeros((n_tiles * TG, D), F32), TD=DISPATCH_TILE)
    ys = _experts(ea, eb, tcls, xs, p["wr"], p["br"], p["wg"], p["wu"], p["wd"], TG=TG)
    return _combine(pos, x1, mod, p["fg"], ys, T=T, TC=COMBINE_TILE)


def _tiling(T):
    if T <= REF_CHUNK:
        return T, T
    return MIXER_CHUNK, min(T, MIXER_TILE)


def _trunk(x, mod, s_ret, s_hg, pos0, inv2, p, *, min_s=8):
    B, T, D = x.shape
    N = B * T
    C, TT = _tiling(T)
    min_s = min(min_s, C // 2)
    routed = N >= ROUTED_MIN_TOKENS
    tabs = _tables(inv2, p["lb_logits"], T=T, C=C, pos0=pos0, min_s=min_s)
    mod3 = mod.reshape(B, 1, 6 * D)
    x1, h2, cls, sr, sh = _mixer(x, mod3, p["n1g"], p["win"], p["wout"], tabs, p["retg"], p["hgg"],
                                 s_ret, s_hg, p["n2g"], p["wr"], p["br"], C=C, TT=TT,
                                 min_s=min_s, with_cls=routed)
    x1 = x1.reshape(N, D)
    h2 = h2.reshape(N, D)
    if routed:
        y = _routed_moe(x1, h2, cls, mod3, p, T=T)
    else:
        y = _moe(x1, h2, mod3, p["wr"], p["br"], p["wg"], p["wu"], p["wd"], p["fg"],
                 T=T, TM=min(N, 1024))
    return y.reshape(B, T, D), sr[None], sh[None]


def _prepare(W_ada, b_ada, norm1_g, norm2_g, W_in, ret_norm_g, hgrn_norm_g, hgrn_lb_logits, W_out,
             W_router_group, b_router_group, W_router_expert, b_router_expert, W_gate_e, W_up_e,
             W_down_e, final_norm_g):
    assert W_in.shape[0] == 1, "single-layer stack only"
    D = D_MODEL
    wr = jnp.concatenate([W_router_group[0], W_router_expert[0]], axis=1)
    wr = jnp.pad(wr, ((0, 0), (0, ROUTE_LANES - wr.shape[1])))
    wr1 = wr.astype(BF16)
    wr2 = (wr - wr1.astype(F32)).astype(BF16)
    br = jnp.concatenate([b_router_group[0], b_router_expert[0]])
    br = jnp.pad(br, (0, ROUTE_LANES - br.shape[0])).reshape(1, ROUTE_LANES)
    return dict(
        n1g=norm1_g[0].reshape(1, D), n2g=norm2_g[0].reshape(1, D), fg=final_norm_g.reshape(1, D),
        win=W_in[0].astype(BF16), wout=W_out[0].astype(BF16),
        retg=ret_norm_g[0].reshape(1, -1), hgg=hgrn_norm_g[0].reshape(1, -1),
        lb_logits=hgrn_lb_logits,
        wr=jnp.concatenate([wr1, wr2], axis=1), br=br,
        wg=W_gate_e[0].astype(BF16), wu=W_up_e[0].astype(BF16), wd=W_down_e[0].astype(BF16),
    )


def _rope_inv():
    half = DH // 2
    inv = 1.0 / (ROPE_BASE ** (jnp.arange(half, dtype=F32) / half))
    return jnp.concatenate([inv, inv]).reshape(1, DH)


def kernel(x_prompt, x_sample, c_prompt, c_sample, state_ret, state_hgrn, W_ada, b_ada, norm1_g,
           norm2_g, W_in, ret_norm_g, hgrn_norm_g, hgrn_lb_logits, W_out, W_router_group,
           b_router_group, W_router_expert, b_router_expert, W_gate_e, W_up_e, W_down_e,
           final_norm_g):
    p = _prepare(W_ada, b_ada, norm1_g, norm2_g, W_in, ret_norm_g, hgrn_norm_g, hgrn_lb_logits,
                 W_out, W_router_group, b_router_group, W_router_expert, b_router_expert,
                 W_gate_e, W_up_e, W_down_e, final_norm_g)
    Bp = x_prompt.shape[0]
    Bs = x_sample.shape[0]
    inv2 = _rope_inv()
    mod = _ada(jnp.concatenate([c_prompt, c_sample], axis=0), W_ada[0], b_ada[0].reshape(1, -1))
    zeros = jnp.zeros((Bp, HEADS, DH, DH), F32)
    y_p, sr_p, sh_p = _trunk(x_prompt, mod[:Bp], zeros, zeros, 0, inv2, p)
    y_s, sr_s, sh_s = _trunk(x_sample, mod[Bp:Bp + Bs], state_ret[0], state_hgrn[0], PAST_LEN,
                             inv2, p)
    return (y_p, y_s, sr_p, sh_p, sr_s, sh_s)
```

```python
import functools

import numpy as np
import jax
import jax.numpy as jnp
from jax import lax
from jax.experimental import pallas as pl
from jax.experimental.pallas import tpu as pltpu

F32 = jnp.float32
BF16 = jnp.bfloat16

D_MODEL = 1024
HEADS = 4
DH = 128
D_IN = 8 * HEADS * DH
N_GROUPS = 4
EXP_PER_GROUP = 4
N_EXPERTS = N_GROUPS * EXP_PER_GROUP
D_EXPERT = 512
ROPE_BASE = 10000.0
EPS = 1e-6
REF_CHUNK = 64
PAST_LEN = 4096

V7X_VMEM_LIMIT = 56 * 1024 * 1024
ROUTE_LANES = 128

_PAIRS = [(i, j) for i in range(EXP_PER_GROUP) for j in range(i + 1, EXP_PER_GROUP)]
N_PAIRS = len(_PAIRS)
N_CLASSES = N_GROUPS * N_PAIRS
CLASS_EXPERT_A = np.array([g * EXP_PER_GROUP + i for g in range(N_GROUPS) for i, _ in _PAIRS]
                          + [N_EXPERTS - 2], np.int32)
CLASS_EXPERT_B = np.array([g * EXP_PER_GROUP + j for g in range(N_GROUPS) for _, j in _PAIRS]
                          + [N_EXPERTS - 1], np.int32)
MIXER_CHUNK = 128
MIXER_TILE = 512
MIXER_PROJ_ROWS = 256
ROUTED_MIN_TOKENS = 8192
ROUTED_TILE = 512
DISPATCH_TILE = 1024
COMBINE_TILE = 512


def _dot(a, b):
    return jnp.dot(a, b, preferred_element_type=F32)


def _dot_nt(a, b):
    return lax.dot_general(a, b, (((1,), (1,)), ((), ())), preferred_element_type=F32)


def _dot_tn(a, b):
    return lax.dot_general(a, b, (((0,), (0,)), ((), ())), preferred_element_type=F32)


def _lanes(parts):
    return parts[0] if len(parts) == 1 else jnp.concatenate(parts, axis=1)


def _block_diag(parts):
    if len(parts) == 1:
        return parts[0]
    rows = []
    for i, p in enumerate(parts):
        rows.append(_lanes([p if j == i else jnp.zeros((p.shape[0], q.shape[1]), p.dtype)
                            for j, q in enumerate(parts)]))
    return jnp.concatenate(rows, axis=0)


def _split3(a):
    a1 = a.astype(BF16)
    r1 = a - a1.astype(F32)
    a2 = r1.astype(BF16)
    a3 = (r1 - a2.astype(F32)).astype(BF16)
    return a1, a2, a3


def _silu(a):
    return a * (1.0 / (1.0 + jnp.exp(-a)))


def _tables_kernel(inv_ref, lbl_ref, cos_ref, sin_ref, rd_ref, idec_ref, kdec_ref, lb_ref,
                   lvl_ref, tri_ref, *, T, C, pos0, min_s):
    pos = (lax.broadcasted_iota(jnp.int32, (T, DH), 0) + pos0).astype(F32)
    ang = pos * inv_ref[...]
    lane = lax.broadcasted_iota(jnp.int32, (T, DH), 1)
    cos_ref[...] = jnp.cos(ang)
    s = jnp.sin(ang)
    sin_ref[...] = jnp.where(lane < DH // 2, -s, s)

    li = lax.broadcasted_iota(jnp.int32, (C, C), 0)
    mi = lax.broadcasted_iota(jnp.int32, (C, C), 1)
    diff = (li - mi).astype(F32)
    rowj = lax.broadcasted_iota(jnp.int32, (C, DH), 0).astype(F32)
    for h in range(HEADS):
        lg = jnp.log(jnp.full((1, 1), 1.0 - 2.0 ** (-5.0 - h), F32))
        rd_ref[h] = jnp.where(diff >= 0, jnp.exp(jnp.maximum(diff, 0.0) * lg), 0.0)
        idec_ref[h] = jnp.exp((rowj + 1.0) * lg)
        kdec_ref[h] = jnp.exp((C - 1.0 - rowj) * lg)

    lg_all = lbl_ref[...]
    m = jnp.max(lg_all, axis=0, keepdims=True)
    e = jnp.exp(lg_all - m)
    lb_ref[...] = e[0:1, :] / jnp.sum(e, axis=0, keepdims=True)

    x = li ^ mi
    code = jnp.zeros((C, C), jnp.int32)
    sz = min_s
    while sz < C:
        code = code + (x >= sz).astype(jnp.int32)
        sz *= 2
    code = jnp.where(x >= min_s, code + (min_s.bit_length() - 1), 0)
    lvl_ref[...] = jnp.where(li >= mi, code, -1)
    tri_ref[...] = (li >= mi).astype(BF16)


def _tables(inv2, lb_logits, *, T, C, pos0, min_s):
    out_shape = (
        jax.ShapeDtypeStruct((T, DH), F32),
        jax.ShapeDtypeStruct((T, DH), F32),
        jax.ShapeDtypeStruct((HEADS, C, C), F32),
        jax.ShapeDtypeStruct((HEADS, C, DH), F32),
        jax.ShapeDtypeStruct((HEADS, C, DH), F32),
        jax.ShapeDtypeStruct((1, HEADS * DH), F32),
        jax.ShapeDtypeStruct((C, C), jnp.int32),
        jax.ShapeDtypeStruct((C, C), BF16),
    )
    return pl.pallas_call(
        functools.partial(_tables_kernel, T=T, C=C, pos0=pos0, min_s=min_s),
        out_shape=out_shape,
        name="tables",
    )(inv2, lb_logits)


def _ada_kernel(c_ref, w_ref, b_ref, o_ref):
    a = _silu(c_ref[...])
    w = w_ref[...]
    a1, a2, a3 = _split3(a)
    w1, w2, w3 = _split3(w)
    acc = _dot(a1, w1) + (_dot(a1, w2) + _dot(a2, w1)) + (_dot(a1, w3) + _dot(a2, w2) + _dot(a3, w1))
    o_ref[...] = acc + b_ref[...]


def _ada(c, w, b):
    R = c.shape[0]
    N = w.shape[1]
    TN = 512
    return pl.pallas_call(
        _ada_kernel,
        grid=(N // TN,),
        in_specs=[pl.BlockSpec((R, D_MODEL), lambda j: (0, 0)),
                  pl.BlockSpec((D_MODEL, TN), lambda j: (0, j)),
                  pl.BlockSpec((1, TN), lambda j: (0, j))],
        out_specs=pl.BlockSpec((R, TN), lambda j: (0, j)),
        out_shape=jax.ShapeDtypeStruct((R, N), F32),
        name="ada",
    )(c, w, b)


def _block_ref(b, block, idx):
    C, W = b.shape
    if block >= 8:
        b3 = b.reshape(C // block, block, W)
        r = jnp.broadcast_to(b3[:, idx:idx + 1, :], b3.shape)
        return r.reshape(C, W)
    row = lax.broadcasted_iota(jnp.int32, (C, W), 0)
    p = row % block
    out = b
    for q in range(block):
        sh = q - idx
        if sh == 0:
            continue
        rolled = pltpu.roll(b, sh % C, 0)
        out = jnp.where(p == q, rolled, out)
    return out


def _head_norm_gate(o, g_row, gate):
    y = o * lax.rsqrt(jnp.mean(o * o, axis=-1, keepdims=True) + EPS)
    return (y * g_row) * _silu(gate)


def _mixer_chunk(r0, proj_scr, cos_ref, sin_ref, rd_ref, idec_ref, kdec_ref, lb_ref, lvl_ref,
                 tri_ref, retg_ref, hgg_ref, sr_scr, sht_scr, o_scr, *, C, min_s):
    W = HEADS * DH
    rows = pl.ds(r0, C)
    proj = proj_scr[rows, :]
    cosf = cos_ref[rows, :]
    sinf = sin_ref[rows, :]

    G = 2 if C % DH == 0 else 1
    groups = [tuple(range(h, h + G)) for h in range(0, HEADS, G)]

    for hs in groups:
        qb_parts, kr_parts, v_parts = [], [], []
        for h in hs:
            c0 = h * DH
            q = proj[:, c0:c0 + DH]
            k = proj[:, W + c0:W + c0 + DH]
            qb_parts.append((q * cosf + pltpu.roll(q, DH // 2, 1) * sinf).astype(BF16))
            kr_parts.append((k * cosf + pltpu.roll(k, DH // 2, 1) * sinf) * (DH ** -0.5))
            v_parts.append(proj[:, 2 * W + c0:2 * W + c0 + DH].astype(BF16))
        qb = _lanes(qb_parts)
        A = (_dot_nt(qb, _block_diag([kr.astype(BF16) for kr in kr_parts]))
             * _lanes([rd_ref[h] for h in hs]))
        S_parts = [sr_scr[h] for h in hs]
        o = (_dot(A.astype(BF16), _block_diag(v_parts))
             + _dot(qb, _block_diag([S.astype(BF16) for S in S_parts]))
             * _lanes([idec_ref[h] for h in hs]))
        for i, h in enumerate(hs):
            c0 = h * DH
            sr_scr[h] = (idec_ref[h][C - 1:C, :] * S_parts[i]
                         + _dot_tn((kr_parts[i] * kdec_ref[h]).astype(BF16), v_parts[i]))
            g = proj[:, 3 * W + c0:3 * W + c0 + DH]
            o_scr[rows, c0:c0 + DH] = _head_norm_gate(
                o[:, i * DH:(i + 1) * DH], retg_ref[:, c0:c0 + DH], g).astype(BF16)

    z = proj[:, 5 * W:6 * W]
    lb = lb_ref[...]
    e = jnp.exp(-jnp.abs(z))
    r = 1.0 / (1.0 + e)
    er = e * r
    sig = jnp.where(z >= 0, r, er)
    nsig = jnp.where(z >= 0, er, r)
    logf = jnp.log(lb + (1.0 - lb) * sig)
    kk_all = (1.0 - lb) * nsig
    tri = tri_ref[...]
    l1, l2, _ = _split3(logf)
    b_all = _dot(tri, l1) + _dot(tri, l2)

    GW = G * DH
    lvl = _lanes([lvl_ref[...]] * G)
    rowc = lax.broadcasted_iota(jnp.int32, (C, GW), 0)

    def heads_of(a):
        return [a[:, i * DH:(i + 1) * DH] for i in range(G)]

    for hs in groups:
        c0 = hs[0] * DH
        q = proj[:, 4 * W + c0:4 * W + c0 + GW] * (DH ** -0.5)
        kk = kk_all[:, c0:c0 + GW]
        b = b_all[:, c0:c0 + GW]
        v_parts = heads_of(proj[:, 6 * W + c0:6 * W + c0 + GW])
        vb_parts = [v.astype(BF16) for v in v_parts]

        A = jnp.zeros((C, G * C), F32)
        s = C // 2
        while s >= min_s:
            isq = (rowc % (2 * s)) >= s
            ref = _block_ref(b, 2 * s, s - 1)
            d = b - ref
            zz = jnp.where(isq, q, kk) * jnp.exp(jnp.where(isq, d, -d))
            zb = zz.astype(BF16)
            A = jnp.where(lvl == s.bit_length(), _dot_nt(zb, _block_diag(heads_of(zb))), A)
            s //= 2
        if min_s > 1:
            ref = _block_ref(b, min_s, 0)
            d = b - ref
            P = _dot_nt((q * jnp.exp(d)).astype(BF16),
                        _block_diag(heads_of((kk * jnp.exp(-d)).astype(BF16))))
            A = jnp.where(lvl == 0, P, A)
            o = _dot(A.astype(BF16), _block_diag(vb_parts))
        else:
            diag = [jnp.sum(qh * kh, axis=-1, keepdims=True) * v
                    for qh, kh, v in zip(heads_of(q), heads_of(kk), v_parts)]
            o = _dot(A.astype(BF16), _block_diag(vb_parts)) + _lanes(diag)

        ST_parts = [sht_scr[h] for h in hs]
        bl = b[C - 1:C, :]
        o = o + _dot_nt((q * jnp.exp(b)).astype(BF16),
                        _block_diag([ST.astype(BF16) for ST in ST_parts]))
        kdec_parts = heads_of((kk * jnp.exp(bl - b)).astype(BF16))
        ebl_parts = heads_of(jnp.exp(bl))
        for i, h in enumerate(hs):
            ch = h * DH
            sht_scr[h] = ST_parts[i] * ebl_parts[i] + _dot_tn(vb_parts[i], kdec_parts[i])
            g = proj[:, 7 * W + ch:7 * W + ch + DH]
            o_scr[rows, W + ch:W + ch + DH] = _head_norm_gate(
                o[:, i * DH:(i + 1) * DH], hgg_ref[:, ch:ch + DH], g).astype(BF16)


def _mixer_kernel(x_ref, xnext_ref, mod_ref, n1g_ref, win_ref, wout_ref, cos_ref, sin_ref, rd_ref,
                  idec_ref, kdec_ref, lb_ref, lvl_ref, tri_ref, retg_ref, hgg_ref,
                  sr0_ref, sh0_ref, n2g_ref, wr_ref, br_ref,
                  x1_ref, h2_ref, cls_ref, sro_ref, sho_ref,
                  sr_scr, sht_scr, proj_scr, o_scr, *, C, min_s, with_cls, has_next):
    t = pl.program_id(1)
    nt = pl.num_programs(1)
    TT = x_ref.shape[1]
    n_chunks = TT // C

    mod = mod_ref[0]
    sh1 = mod[:, 0:D_MODEL]
    sc1 = mod[:, D_MODEL:2 * D_MODEL]
    gt1 = mod[:, 2 * D_MODEL:3 * D_MODEL]

    PR = xnext_ref.shape[1]
    per_block = PR // C

    def project(xc, r0):
        xn = xc * lax.rsqrt(jnp.mean(xc * xc, axis=-1, keepdims=True) + EPS) * n1g_ref[...]
        hmod = xn * (1.0 + sc1) + sh1
        proj_scr[r0:r0 + PR, :] = _dot(hmod.astype(BF16), win_ref[...])

    @pl.when(t == 0)
    def _():
        for h in range(HEADS):
            sr_scr[h] = sr0_ref[0, h]
            sht_scr[h] = sh0_ref[0, h].T
        project(x_ref[0, 0:PR, :], 0)

    chunk = functools.partial(
        _mixer_chunk, proj_scr=proj_scr, cos_ref=cos_ref, sin_ref=sin_ref, rd_ref=rd_ref,
        idec_ref=idec_ref, kdec_ref=kdec_ref, lb_ref=lb_ref, lvl_ref=lvl_ref, tri_ref=tri_ref,
        retg_ref=retg_ref, hgg_ref=hgg_ref, sr_scr=sr_scr, sht_scr=sht_scr, o_scr=o_scr,
        C=C, min_s=min_s)
    for c in range(n_chunks):
        chunk(c * C)
        if c % per_block == 0:
            r1 = (c // per_block + 1) * PR
            if r1 < TT:
                project(x_ref[0, r1:r1 + PR, :], r1)
            elif has_next:
                project(xnext_ref[0], 0)

    x = x_ref[0]
    mix = _dot(o_scr[...], wout_ref[...])
    x1 = x + gt1 * mix
    x1_ref[0] = x1

    sh2 = mod[:, 3 * D_MODEL:4 * D_MODEL]
    sc2 = mod[:, 4 * D_MODEL:5 * D_MODEL]
    x1n = x1 * lax.rsqrt(jnp.mean(x1 * x1, axis=-1, keepdims=True) + EPS) * n2g_ref[...]
    h2 = x1n * (1.0 + sc2) + sh2
    h2_ref[0] = h2
    if with_cls:
        g_idx, i1, i2 = _route_topk(_router_logits(h2, wr_ref, br_ref).T)[:3]
        lo = jnp.minimum(i1, i2)
        hi = jnp.maximum(i1, i2)
        pair = hi - 1.0 + jnp.where(lo == 1.0, 2.0, 0.0) + jnp.where(lo == 2.0, 3.0, 0.0)
        cls_ref[0] = (g_idx * float(N_PAIRS) + pair).astype(jnp.int32)
    else:
        cls_ref[0] = jnp.zeros((1, TT), jnp.int32)

    @pl.when(t == nt - 1)
    def _():
        for h in range(HEADS):
            sro_ref[0, h] = sr_scr[h]
            sho_ref[0, h] = sht_scr[h].T


def _mixer(x, mod, n1g, win, wout, tabs, retg, hgg, sr0, sh0, n2g, wr, br, *, C, TT, min_s,
           with_cls):
    B, T, D = x.shape
    cosf, sinf, rd, idec, kdec, lb, lvl, tri = tabs
    nt = T // TT
    const2 = lambda b, t: (0, 0)
    const3 = lambda b, t: (0, 0, 0)
    state_spec = pl.BlockSpec((1, HEADS, DH, DH), lambda b, t: (b, 0, 0, 0))
    PR = min(TT, MIXER_PROJ_ROWS)
    per_tile = TT // PR
    last_block = T // PR - 1
    in_specs = [
        pl.BlockSpec((1, TT, D), lambda b, t: (b, t, 0)),
        pl.BlockSpec((1, PR, D),
                     lambda b, t: (b, jnp.minimum((t + 1) * per_tile, last_block), 0)),
        pl.BlockSpec((1, 1, 6 * D), lambda b, t: (b, 0, 0)),
        pl.BlockSpec((1, D), const2),
        pl.BlockSpec((D, D_IN), const2),
        pl.BlockSpec((D, D), const2),
        pl.BlockSpec((TT, DH), lambda b, t: (t, 0)),
        pl.BlockSpec((TT, DH), lambda b, t: (t, 0)),
        pl.BlockSpec((HEADS, C, C), const3),
        pl.BlockSpec((HEADS, C, DH), const3),
        pl.BlockSpec((HEADS, C, DH), const3),
        pl.BlockSpec((1, HEADS * DH), const2),
        pl.BlockSpec((C, C), const2),
        pl.BlockSpec((C, C), const2),
        pl.BlockSpec((1, HEADS * DH), const2),
        pl.BlockSpec((1, HEADS * DH), const2),
        state_spec, state_spec,
        pl.BlockSpec((1, D), const2),
        pl.BlockSpec((D, 2 * ROUTE_LANES), const2),
        pl.BlockSpec((1, ROUTE_LANES), const2),
    ]
    tok_spec = pl.BlockSpec((1, TT, D), lambda b, t: (b, t, 0))
    out_specs = [tok_spec, tok_spec,
                 pl.BlockSpec((1, 1, TT), lambda b, t: (b * nt + t, 0, 0)),
                 state_spec, state_spec]
    out_shape = [jax.ShapeDtypeStruct((B, T, D), F32),
                 jax.ShapeDtypeStruct((B, T, D), F32),
                 jax.ShapeDtypeStruct((B * nt, 1, TT), jnp.int32),
                 jax.ShapeDtypeStruct((B, HEADS, DH, DH), F32),
                 jax.ShapeDtypeStruct((B, HEADS, DH, DH), F32)]
    return pl.pallas_call(
        functools.partial(_mixer_kernel, C=C, min_s=min_s, with_cls=with_cls, has_next=nt > 1),
        grid=(B, nt),
        in_specs=in_specs,
        out_specs=out_specs,
        out_shape=out_shape,
        scratch_shapes=[pltpu.VMEM((HEADS, DH, DH), F32),
                        pltpu.VMEM((HEADS, DH, DH), F32),
                        pltpu.VMEM((TT, D_IN), F32),
                        pltpu.VMEM((TT, D), BF16)],
        compiler_params=pltpu.CompilerParams(
            dimension_semantics=("arbitrary", "arbitrary"),
            vmem_limit_bytes=V7X_VMEM_LIMIT),
        name="mixer",
    )(x, x, mod, n1g, win, wout, cosf, sinf, rd, idec, kdec, lb, lvl, tri, retg, hgg, sr0, sh0,
      n2g, wr, br)


def _router_logits(h, wr_ref, br_ref):
    h1, h2, _ = _split3(h)
    hi = _dot(h1, wr_ref[...])
    lo = _dot(h2, wr_ref[:, 0:ROUTE_LANES])
    return hi[:, 0:ROUTE_LANES] + (hi[:, ROUTE_LANES:] + lo) + br_ref[...]


def _route_topk(logits_t):
    TM = logits_t.shape[1]
    lg = logits_t[0:N_GROUPS, :]
    gi = lax.broadcasted_iota(jnp.int32, (N_GROUPS, TM), 0).astype(F32)
    m = jnp.max(lg, axis=0, keepdims=True)
    g_w = 1.0 / jnp.sum(jnp.exp(lg - m), axis=0, keepdims=True)
    g_idx = jnp.min(jnp.where(lg == m, gi, float(N_GROUPS)), axis=0, keepdims=True)
    el = jnp.zeros((EXP_PER_GROUP, TM), F32)
    for g in range(N_GROUPS):
        lo = N_GROUPS + g * EXP_PER_GROUP
        el = jnp.where(g_idx == float(g), logits_t[lo:lo + EXP_PER_GROUP, :], el)
    ei = lax.broadcasted_iota(jnp.int32, (EXP_PER_GROUP, TM), 0).astype(F32)
    m1 = jnp.max(el, axis=0, keepdims=True)
    i1 = jnp.min(jnp.where(el == m1, ei, float(EXP_PER_GROUP)), axis=0, keepdims=True)
    el2 = jnp.where(ei == i1, -jnp.inf, el)
    m2 = jnp.max(el2, axis=0, keepdims=True)
    i2 = jnp.min(jnp.where(el2 == m2, ei, float(EXP_PER_GROUP)), axis=0, keepdims=True)
    p2 = jnp.exp(m2 - m1)
    den = 1.0 / (1.0 + p2)
    w0 = g_w * den
    w1 = g_w * (p2 * den)
    return g_idx, i1, i2, w0, w1


def _moe_kernel(x1_ref, h2_ref, mod_ref, wr_ref, br_ref, wg_ref, wu_ref, wd_ref, fg_ref,
                y_ref, h_scr, route_scr, acc_scr):
    e = pl.program_id(1)
    ne = pl.num_programs(1)
    TM, D = x1_ref.shape
    K = mod_ref.shape[0]
    TQ = TM // K

    @pl.when(e == 0)
    def _():
        h = h2_ref[...]
        h_scr[...] = h.astype(BF16)
        g_idx, i1, i2, w0, wgt1 = _route_topk(_router_logits(h, wr_ref, br_ref).T)
        e0 = g_idx * EXP_PER_GROUP + i1
        e1 = g_idx * EXP_PER_GROUP + i2
        ri = lax.broadcasted_iota(jnp.int32, (ROUTE_LANES, TM), 0)
        rows = jnp.where(ri == 0, e0, jnp.where(ri == 1, e1, jnp.where(ri == 2, w0,
                         jnp.where(ri == 3, wgt1, 0.0))))
        route_scr[...] = rows.T
        acc_scr[...] = jnp.zeros_like(acc_scr)

    hb = h_scr[...]
    hid = _silu(_dot(hb, wg_ref[0])) * _dot(hb, wu_ref[0])
    out = _dot(hid.astype(BF16), wd_ref[0])
    ef = e.astype(F32)
    route = route_scr[...]
    comb = (jnp.where(route[:, 0:1] == ef, route[:, 2:3], 0.0)
            + jnp.where(route[:, 1:2] == ef, route[:, 3:4], 0.0))
    acc_scr[...] += comb * out

    @pl.when(e == ne - 1)
    def _():
        gt2 = mod_ref[...][:, :, 5 * D:6 * D]
        ff = (gt2 * acc_scr[...].reshape(K, TQ, D)).reshape(TM, D)
        x2 = x1_ref[...] + ff
        y_ref[...] = x2 * lax.rsqrt(jnp.mean(x2 * x2, axis=-1, keepdims=True) + EPS) * fg_ref[...]


def _moe(x1, h2, mod, wr, br, wg, wu, wd, fg, *, T, TM):
    N, D = x1.shape
    if T >= TM:
        assert T % TM == 0
        seqs, tiles_per_seq = 1, T // TM
        mod_map = lambda i, e: (i // tiles_per_seq, 0, 0)
    else:
        assert TM % T == 0
        seqs = TM // T
        mod_map = lambda i, e: (i, 0, 0)
    return pl.pallas_call(
        _moe_kernel,
        grid=(N // TM, N_EXPERTS),
        in_specs=[
            pl.BlockSpec((TM, D), lambda i, e: (i, 0)),
            pl.BlockSpec((TM, D), lambda i, e: (i, 0)),
            pl.BlockSpec((seqs, 1, 6 * D), mod_map),
            pl.BlockSpec((D, 2 * ROUTE_LANES), lambda i, e: (0, 0)),
            pl.BlockSpec((1, ROUTE_LANES), lambda i, e: (0, 0)),
            pl.BlockSpec((1, D, D_EXPERT), lambda i, e: (e, 0, 0)),
            pl.BlockSpec((1, D, D_EXPERT), lambda i, e: (e, 0, 0)),
            pl.BlockSpec((1, D_EXPERT, D), lambda i, e: (e, 0, 0)),
            pl.BlockSpec((1, D), lambda i, e: (0, 0)),
        ],
        out_specs=pl.BlockSpec((TM, D), lambda i, e: (i, 0)),
        out_shape=jax.ShapeDtypeStruct((N, D), F32),
        scratch_shapes=[pltpu.VMEM((TM, D), BF16),
                        pltpu.VMEM((TM, ROUTE_LANES), F32),
                        pltpu.VMEM((TM, D), F32)],
        compiler_params=pltpu.CompilerParams(
            dimension_semantics=("arbitrary", "arbitrary"),
            vmem_limit_bytes=V7X_VMEM_LIMIT),
        name="moe",
    )(x1, h2, mod, wr, br, wg, wu, wd, fg)


INFO_TAIL_LEN = 32
INFO_END = 64


def _rank_kernel(cls_ref, pos_ref, tcls_ref, info_ref, *, TG):
    R, L = cls_ref.shape
    info_lane = lax.broadcasted_iota(jnp.int32, info_ref.shape, 1)
    info = jnp.zeros(info_ref.shape, F32)
    cf = cls_ref[...].astype(F32)
    ri = lax.broadcasted_iota(jnp.int32, (L, L), 0)
    ci = lax.broadcasted_iota(jnp.int32, (L, L), 1)
    upper = (ri < ci).astype(BF16)
    rr = lax.broadcasted_iota(jnp.int32, (R, R), 0)
    rc = lax.broadcasted_iota(jnp.int32, (R, R), 1)
    lower = (rc < rr).astype(BF16)
    tile_start = lax.broadcasted_iota(jnp.int32, tcls_ref.shape, 1).astype(F32) * float(TG)
    base = jnp.zeros((1, 1), F32)
    pos = jnp.zeros((R, L), F32)
    tcls = jnp.zeros(tcls_ref.shape, F32)
    for c in range(N_CLASSES):
        ind = jnp.where(cf == float(c), 1.0, 0.0)
        lane_pre = _dot(ind.astype(BF16), upper)
        row_tot = jnp.broadcast_to(jnp.sum(ind, axis=1, keepdims=True), (R, L))
        row_pre = _dot(lower, row_tot.astype(BF16))
        total = jnp.sum(row_tot[:, 0:1], axis=0, keepdims=True)
        pos = jnp.where(ind > 0.0, base + row_pre + lane_pre, pos)
        padded = jnp.floor((total + float(TG - 1)) * (1.0 / TG)) * float(TG)
        info = jnp.where(info_lane == c, base + total, info)
        info = jnp.where(info_lane == INFO_TAIL_LEN + c, padded - total, info)
        base = base + padded
        tcls = tcls + jnp.where(tile_start >= base, 1.0, 0.0)
    info = jnp.where(info_lane == INFO_END, base, info)
    pos_ref[...] = pos.astype(jnp.int32)
    tcls_ref[...] = tcls.astype(jnp.int32)
    info_ref[...] = info.astype(jnp.int32)


def _rank(cls2d, *, TG, n_tiles):
    R, L = cls2d.shape
    lanes = -(-n_tiles // 128) * 128
    return pl.pallas_call(
        functools.partial(_rank_kernel, TG=TG),
        out_shape=(jax.ShapeDtypeStruct((R, L), jnp.int32),
                   jax.ShapeDtypeStruct((1, lanes), jnp.int32),
                   jax.ShapeDtypeStruct((1, 128), jnp.int32)),
        name="rank",
    )(cls2d)


SUBLANES = 8


def _for_each_row(n_rows, fn):
    def group(g, carry):
        r0 = pl.multiple_of(g * SUBLANES, SUBLANES)
        for j in range(SUBLANES):
            fn(pl.ds(r0, SUBLANES), j, r0 + j)
        return carry
    lax.fori_loop(0, n_rows // SUBLANES, group, 0)


def _dispatch_kernel(info_ref, pos_ref, h2_ref, xs_hbm, zero_scr, sem, zsem, *, TD):
    ZR = zero_scr.shape[0]

    def tail_copies(do):
        for c in range(N_CLASSES):
            start = info_ref[c]
            head = (-start) & (SUBLANES - 1)
            for i in range(SUBLANES - 1):
                @pl.when(i < head)
                def _(i=i):
                    do(pltpu.make_async_copy(zero_scr.at[pl.ds(0, 1)],
                                             xs_hbm.at[pl.ds(start + i, 1)], zsem))
            rest = info_ref[INFO_TAIL_LEN + c] - head
            off = start + head
            k = ZR
            while k >= SUBLANES:
                @pl.when((rest & k) != 0)
                def _(k=k, off=off):
                    do(pltpu.make_async_copy(
                        zero_scr.at[pl.ds(0, k)],
                        xs_hbm.at[pl.ds(pl.multiple_of(off, SUBLANES), k)], zsem))
                off = off + (rest & k)
                k //= 2

    def unused_copies(do):
        first = info_ref[INFO_END] // ZR

        def body(i, carry):
            do(pltpu.make_async_copy(
                zero_scr, xs_hbm.at[pl.ds(pl.multiple_of(i * ZR, ZR), ZR)], zsem))
            return carry
        lax.fori_loop(first, xs_hbm.shape[0] // ZR, body, 0)

    @pl.when(pl.program_id(0) == 0)
    def _():
        zero_scr[...] = jnp.zeros_like(zero_scr)
        tail_copies(lambda cp: cp.start())
        unused_copies(lambda cp: cp.start())
        tail_copies(lambda cp: cp.wait())
        unused_copies(lambda cp: cp.wait())

    def issue(tile, j, r):
        pltpu.make_async_copy(h2_ref.at[tile].at[pl.ds(j, 1)],
                              xs_hbm.at[pl.ds(pos_ref[r], 1)], sem).start()

    _for_each_row(TD, issue)
    pltpu.make_async_copy(h2_ref, xs_hbm.at[pl.ds(0, TD)], sem).wait()


def _dispatch(info, pos, h2, *, n_rows, TG, TD):
    N, D = h2.shape
    grid_spec = pltpu.PrefetchScalarGridSpec(
        num_scalar_prefetch=1,
        grid=(N // TD,),
        in_specs=[pl.BlockSpec((TD,), lambda i, info: (i,), memory_space=pltpu.SMEM),
                  pl.BlockSpec((TD, D), lambda i, info: (i, 0))],
        out_specs=pl.BlockSpec(memory_space=pl.ANY),
        scratch_shapes=[pltpu.VMEM((TG // 2, D), F32),
                        pltpu.SemaphoreType.DMA(()), pltpu.SemaphoreType.DMA(())],
    )
    return pl.pallas_call(
        functools.partial(_dispatch_kernel, TD=TD),
        grid_spec=grid_spec,
        out_shape=jax.ShapeDtypeStruct((n_rows, D), F32),
        compiler_params=pltpu.CompilerParams(
            dimension_semantics=("arbitrary",), vmem_limit_bytes=V7X_VMEM_LIMIT),
        name="dispatch",
    )(info, pos, h2)


def _expert_kernel(ea_ref, eb_ref, nv_ref, xs_ref, wr_ref, br_ref, wga_ref, wua_ref, wda_ref,
                   wgb_ref, wub_ref, wdb_ref, ys_ref):
    j = pl.program_id(0)
    TG = xs_ref.shape[0]
    n_valid = nv_ref[0]

    @pl.when(j < n_valid)
    def _():
        a = ea_ref[j]
        b = eb_ref[j]
        g = lax.div(a, jnp.int32(EXP_PER_GROUP))
        hb = xs_ref[...].astype(BF16)
        logits = _dot(hb, wr_ref[:, 0:ROUTE_LANES]) + br_ref[...]
        lane = lax.broadcasted_iota(jnp.int32, (TG, ROUTE_LANES), 1)
        isg = lane < N_GROUPS
        m = jnp.max(jnp.where(isg, logits, -jnp.inf), axis=1, keepdims=True)
        ssum = jnp.sum(jnp.where(isg, jnp.exp(logits - m), 0.0), axis=1, keepdims=True)
        lg = jnp.sum(jnp.where(lane == g, logits, 0.0), axis=1, keepdims=True)
        la = jnp.sum(jnp.where(lane == N_GROUPS + a, logits, 0.0), axis=1, keepdims=True)
        lb = jnp.sum(jnp.where(lane == N_GROUPS + b, logits, 0.0), axis=1, keepdims=True)
        g_w = jnp.exp(lg - m) / ssum
        mm = jnp.maximum(la, lb)
        pa = jnp.exp(la - mm)
        pb = jnp.exp(lb - mm)
        den = g_w / (pa + pb)
        hid_a = _silu(_dot(hb, wga_ref[0])) * _dot(hb, wua_ref[0])
        out = (pa * den) * _dot(hid_a.astype(BF16), wda_ref[0])
        hid_b = _silu(_dot(hb, wgb_ref[0])) * _dot(hb, wub_ref[0])
        ys_ref[...] = out + (pb * den) * _dot(hid_b.astype(BF16), wdb_ref[0])

    @pl.when(j >= n_valid)
    def _():
        ys_ref[...] = jnp.zeros_like(ys_ref)


def _experts(ea, eb, n_valid, xs, wr, br, wg, wu, wd, *, TG):
    NS, D = xs.shape
    const2 = lambda j, ea, eb, nv: (0, 0)
    wmap_a = lambda j, ea, eb, nv: (ea[j], 0, 0)
    wmap_b = lambda j, ea, eb, nv: (eb[j], 0, 0)
    grid_spec = pltpu.PrefetchScalarGridSpec(
        num_scalar_prefetch=3,
        grid=(NS // TG,),
        in_specs=[
            pl.BlockSpec((TG, D), lambda j, ea, eb, nv: (jnp.minimum(j, nv[0] - 1), 0)),
            pl.BlockSpec((D, 2 * ROUTE_LANES), const2),
            pl.BlockSpec((1, ROUTE_LANES), const2),
            pl.BlockSpec((1, D, D_EXPERT), wmap_a),
            pl.BlockSpec((1, D, D_EXPERT), wmap_a),
            pl.BlockSpec((1, D_EXPERT, D), wmap_a),
            pl.BlockSpec((1, D, D_EXPERT), wmap_b),
            pl.BlockSpec((1, D, D_EXPERT), wmap_b),
            pl.BlockSpec((1, D_EXPERT, D), wmap_b),
        ],
        out_specs=pl.BlockSpec((TG, D), lambda j, ea, eb, nv: (j, 0)),
    )
    return pl.pallas_call(
        _expert_kernel,
        grid_spec=grid_spec,
        out_shape=jax.ShapeDtypeStruct((NS, D), F32),
        compiler_params=pltpu.CompilerParams(
            dimension_semantics=("arbitrary",), vmem_limit_bytes=V7X_VMEM_LIMIT),
        name="experts",
    )(ea, eb, n_valid, xs, wr, br, wg, wu, wd, wg, wu, wd)


def _combine_kernel(pos_ref, posn_ref, x1_ref, mod_ref, fg_ref, ys_hbm, y_ref, buf, sems, *, TC):
    i = pl.program_id(0)
    n = pl.num_programs(0)
    slot = i % 2

    def gather(p_ref, s):
        def issue(tile, j, r):
            pltpu.make_async_copy(ys_hbm.at[pl.ds(p_ref[r], 1)],
                                  buf.at[s].at[tile].at[pl.ds(j, 1)], sems.at[s]).start()
        _for_each_row(TC, issue)

    @pl.when(i == 0)
    def _():
        gather(pos_ref, 0)

    @pl.when(i + 1 < n)
    def _():
        gather(posn_ref, 1 - slot)

    pltpu.make_async_copy(ys_hbm.at[pl.ds(0, TC)], buf.at[slot], sems.at[slot]).wait()
    gt2 = mod_ref[0][:, 5 * D_MODEL:6 * D_MODEL]
    x2 = x1_ref[...] + gt2 * buf[slot]
    y_ref[...] = x2 * lax.rsqrt(jnp.mean(x2 * x2, axis=-1, keepdims=True) + EPS) * fg_ref[...]


def _combine(pos, x1, mod, fg, ys, *, T, TC):
    N, D = x1.shape
    n = N // TC
    tiles_per_seq = T // TC
    return pl.pallas_call(
        functools.partial(_combine_kernel, TC=TC),
        grid=(n,),
        in_specs=[pl.BlockSpec((TC,), lambda i: (i,), memory_space=pltpu.SMEM),
                  pl.BlockSpec((TC,), lambda i: (jnp.minimum(i + 1, n - 1),),
                               memory_space=pltpu.SMEM),
                  pl.BlockSpec((TC, D), lambda i: (i, 0)),
                  pl.BlockSpec((1, 1, 6 * D), lambda i: (i // tiles_per_seq, 0, 0)),
                  pl.BlockSpec((1, D), lambda i: (0, 0)),
                  pl.BlockSpec(memory_space=pl.ANY)],
        out_specs=pl.BlockSpec((TC, D), lambda i: (i, 0)),
        out_shape=jax.ShapeDtypeStruct((N, D), F32),
        scratch_shapes=[pltpu.VMEM((2, TC, D), F32), pltpu.SemaphoreType.DMA((2,))],
        compiler_params=pltpu.CompilerParams(
            dimension_semantics=("arbitrary",), vmem_limit_bytes=V7X_VMEM_LIMIT),
        name="combine",
    )(pos, pos, x1, mod, fg, ys)


def _routed_moe(x1, h2, cls, mod, p, *, T):
    N, D = x1.shape
    TG = ROUTED_TILE
    n_tiles = N // TG + N_CLASSES
    pos2d, tcls, info = _rank(cls.reshape(N // 128, 128), TG=TG, n_tiles=n_tiles)
    pos = pos2d.reshape(N)
    tcls = tcls[0, :n_tiles]
    ea = jnp.asarray(CLASS_EXPERT_A)[tcls]
    eb = jnp.asarray(CLASS_EXPERT_B)[tcls]
    n_valid = (info[0, INFO_END:INFO_END + 1] // TG).astype(jnp.int32)
    xs = _dispatch(info.reshape(-1), pos, h2, n_rows=n_tiles * TG, TG=TG, TD=DISPATCH_TILE)
    ys = _experts(ea, eb, n_valid, xs, p["wr"], p["br"], p["wg"], p["wu"], p["wd"], TG=TG)
    return _combine(pos, x1, mod, p["fg"], ys, T=T, TC=COMBINE_TILE)


def _tiling(T):
    if T <= REF_CHUNK:
        return T, T
    return MIXER_CHUNK, min(T, MIXER_TILE)


def _trunk(x, mod, s_ret, s_hg, pos0, inv2, p, *, min_s=8):
    B, T, D = x.shape
    N = B * T
    C, TT = _tiling(T)
    min_s = min(min_s, C // 2)
    routed = N >= ROUTED_MIN_TOKENS
    tabs = _tables(inv2, p["lb_logits"], T=T, C=C, pos0=pos0, min_s=min_s)
    mod3 = mod.reshape(B, 1, 6 * D)
    x1, h2, cls, sr, sh = _mixer(x, mod3, p["n1g"], p["win"], p["wout"], tabs, p["retg"], p["hgg"],
                                 s_ret, s_hg, p["n2g"], p["wr"], p["br"], C=C, TT=TT,
                                 min_s=min_s, with_cls=routed)
    x1 = x1.reshape(N, D)
    h2 = h2.reshape(N, D)
    if routed:
        y = _routed_moe(x1, h2, cls, mod3, p, T=T)
    else:
        y = _moe(x1, h2, mod3, p["wr"], p["br"], p["wg"], p["wu"], p["wd"], p["fg"],
                 T=T, TM=min(N, 1024))
    return y.reshape(B, T, D), sr[None], sh[None]


def _prepare(W_ada, b_ada, norm1_g, norm2_g, W_in, ret_norm_g, hgrn_norm_g, hgrn_lb_logits, W_out,
             W_router_group, b_router_group, W_router_expert, b_router_expert, W_gate_e, W_up_e,
             W_down_e, final_norm_g):
    assert W_in.shape[0] == 1, "single-layer stack only"
    D = D_MODEL
    wr = jnp.concatenate([W_router_group[0], W_router_expert[0]], axis=1)
    wr = jnp.pad(wr, ((0, 0), (0, ROUTE_LANES - wr.shape[1])))
    wr1 = wr.astype(BF16)
    wr2 = (wr - wr1.astype(F32)).astype(BF16)
    br = jnp.concatenate([b_router_group[0], b_router_expert[0]])
    br = jnp.pad(br, (0, ROUTE_LANES - br.shape[0])).reshape(1, ROUTE_LANES)
    return dict(
        n1g=norm1_g[0].reshape(1, D), n2g=norm2_g[0].reshape(1, D), fg=final_norm_g.reshape(1, D),
        win=W_in[0].astype(BF16), wout=W_out[0].astype(BF16),
        retg=ret_norm_g[0].reshape(1, -1), hgg=hgrn_norm_g[0].reshape(1, -1),
        lb_logits=hgrn_lb_logits,
        wr=jnp.concatenate([wr1, wr2], axis=1), br=br,
        wg=W_gate_e[0].astype(BF16), wu=W_up_e[0].astype(BF16), wd=W_down_e[0].astype(BF16),
    )


def _rope_inv():
    half = DH // 2
    inv = 1.0 / (ROPE_BASE ** (jnp.arange(half, dtype=F32) / half))
    return jnp.concatenate([inv, inv]).reshape(1, DH)


def kernel(x_prompt, x_sample, c_prompt, c_sample, state_ret, state_hgrn, W_ada, b_ada, norm1_g,
           norm2_g, W_in, ret_norm_g, hgrn_norm_g, hgrn_lb_logits, W_out, W_router_group,
           b_router_group, W_router_expert, b_router_expert, W_gate_e, W_up_e, W_down_e,
           final_norm_g):
    p = _prepare(W_ada, b_ada, norm1_g, norm2_g, W_in, ret_norm_g, hgrn_norm_g, hgrn_lb_logits,
                 W_out, W_router_group, b_router_group, W_router_expert, b_router_expert,
                 W_gate_e, W_up_e, W_down_e, final_norm_g)
    Bp = x_prompt.shape[0]
    Bs = x_sample.shape[0]
    inv2 = _rope_inv()
    mod = _ada(jnp.concatenate([c_prompt, c_sample], axis=0), W_ada[0], b_ada[0].reshape(1, -1))
    zeros = jnp.zeros((Bp, HEADS, DH, DH), F32)
    y_p, sr_p, sh_p = _trunk(x_prompt, mod[:Bp], zeros, zeros, 0, inv2, p)
    y_s, sr_s, sh_s = _trunk(x_sample, mod[Bp:Bp + Bs], state_ret[0], state_hgrn[0], PAST_LEN,
                             inv2, p)
    return (y_p, y_s, sr_p, sh_p, sr_s, sh_s)
```

```python
import functools

import numpy as np
import jax
import jax.numpy as jnp
from jax import lax
from jax.experimental import pallas as pl
from jax.experimental.pallas import tpu as pltpu

F32 = jnp.float32
BF16 = jnp.bfloat16

D_MODEL = 1024
HEADS = 4
DH = 128
D_IN = 8 * HEADS * DH
N_GROUPS = 4
EXP_PER_GROUP = 4
N_EXPERTS = N_GROUPS * EXP_PER_GROUP
D_EXPERT = 512
ROPE_BASE = 10000.0
EPS = 1e-6
REF_CHUNK = 64
PAST_LEN = 4096

V7X_VMEM_LIMIT = 56 * 1024 * 1024
ROUTE_LANES = 128

_PAIRS = [(i, j) for i in range(EXP_PER_GROUP) for j in range(i + 1, EXP_PER_GROUP)]
N_PAIRS = len(_PAIRS)
N_CLASSES = N_GROUPS * N_PAIRS
CLASS_EXPERT_A = np.array([g * EXP_PER_GROUP + i for g in range(N_GROUPS) for i, _ in _PAIRS]
                          + [N_EXPERTS - 2], np.int32)
CLASS_EXPERT_B = np.array([g * EXP_PER_GROUP + j for g in range(N_GROUPS) for _, j in _PAIRS]
                          + [N_EXPERTS - 1], np.int32)
MIXER_CHUNK = 128
HGRN_MIN_BLOCK = 4
MIXER_TILE = 512
MIXER_PROJ_ROWS = 256
ROUTED_MIN_TOKENS = 8192
ROUTED_TILE = 512
DISPATCH_TILE = 1024
COMBINE_TILE = 512


def _dot(a, b):
    return jnp.dot(a, b, preferred_element_type=F32)


def _dot_nt(a, b):
    return lax.dot_general(a, b, (((1,), (1,)), ((), ())), preferred_element_type=F32)


def _dot_tn(a, b):
    return lax.dot_general(a, b, (((0,), (0,)), ((), ())), preferred_element_type=F32)


def _lanes(parts):
    return parts[0] if len(parts) == 1 else jnp.concatenate(parts, axis=1)


def _block_diag(parts):
    if len(parts) == 1:
        return parts[0]
    rows = []
    for i, p in enumerate(parts):
        rows.append(_lanes([p if j == i else jnp.zeros((p.shape[0], q.shape[1]), p.dtype)
                            for j, q in enumerate(parts)]))
    return jnp.concatenate(rows, axis=0)


def _split3(a):
    a1 = a.astype(BF16)
    r1 = a - a1.astype(F32)
    a2 = r1.astype(BF16)
    a3 = (r1 - a2.astype(F32)).astype(BF16)
    return a1, a2, a3


def _silu(a):
    return a * (1.0 / (1.0 + jnp.exp(-a)))


def _tables_kernel(inv_ref, lbl_ref, cos_ref, sin_ref, rd_ref, idec_ref, kdec_ref, lb_ref,
                   lvl_ref, tri_ref, *, T, C, pos0, min_s):
    pos = (lax.broadcasted_iota(jnp.int32, (T, DH), 0) + pos0).astype(F32)
    ang = pos * inv_ref[...]
    lane = lax.broadcasted_iota(jnp.int32, (T, DH), 1)
    cos_ref[...] = jnp.cos(ang)
    s = jnp.sin(ang)
    sin_ref[...] = jnp.where(lane < DH // 2, -s, s)

    li = lax.broadcasted_iota(jnp.int32, (C, C), 0)
    mi = lax.broadcasted_iota(jnp.int32, (C, C), 1)
    diff = (li - mi).astype(F32)
    rowj = lax.broadcasted_iota(jnp.int32, (C, DH), 0).astype(F32)
    for h in range(HEADS):
        lg = jnp.log(jnp.full((1, 1), 1.0 - 2.0 ** (-5.0 - h), F32))
        rd_ref[h] = jnp.where(diff >= 0, jnp.exp(jnp.maximum(diff, 0.0) * lg), 0.0)
        idec_ref[h] = jnp.exp((rowj + 1.0) * lg)
        kdec_ref[h] = jnp.exp((C - 1.0 - rowj) * lg)

    lg_all = lbl_ref[...]
    m = jnp.max(lg_all, axis=0, keepdims=True)
    e = jnp.exp(lg_all - m)
    lb_ref[...] = e[0:1, :] / jnp.sum(e, axis=0, keepdims=True)

    x = li ^ mi
    code = jnp.zeros((C, C), jnp.int32)
    sz = min_s
    while sz < C:
        code = code + (x >= sz).astype(jnp.int32)
        sz *= 2
    code = jnp.where(x >= min_s, code + (min_s.bit_length() - 1), 0)
    lvl_ref[...] = jnp.where(li >= mi, code, -1)
    tri_ref[...] = (li >= mi).astype(BF16)


def _tables(inv2, lb_logits, *, T, C, pos0, min_s):
    out_shape = (
        jax.ShapeDtypeStruct((T, DH), F32),
        jax.ShapeDtypeStruct((T, DH), F32),
        jax.ShapeDtypeStruct((HEADS, C, C), F32),
        jax.ShapeDtypeStruct((HEADS, C, DH), F32),
        jax.ShapeDtypeStruct((HEADS, C, DH), F32),
        jax.ShapeDtypeStruct((1, HEADS * DH), F32),
        jax.ShapeDtypeStruct((C, C), jnp.int32),
        jax.ShapeDtypeStruct((C, C), BF16),
    )
    return pl.pallas_call(
        functools.partial(_tables_kernel, T=T, C=C, pos0=pos0, min_s=min_s),
        out_shape=out_shape,
        name="tables",
    )(inv2, lb_logits)


def _ada_kernel(c_ref, w_ref, b_ref, o_ref):
    a = _silu(c_ref[...])
    w = w_ref[...]
    a1, a2, _ = _split3(a)
    w1, w2, _ = _split3(w)
    o_ref[...] = _dot(a1, w1) + (_dot(a1, w2) + _dot(a2, w1)) + b_ref[...]


def _ada(c, w, b):
    R = c.shape[0]
    N = w.shape[1]
    TN = 512
    return pl.pallas_call(
        _ada_kernel,
        grid=(N // TN,),
        in_specs=[pl.BlockSpec((R, D_MODEL), lambda j: (0, 0)),
                  pl.BlockSpec((D_MODEL, TN), lambda j: (0, j)),
                  pl.BlockSpec((1, TN), lambda j: (0, j))],
        out_specs=pl.BlockSpec((R, TN), lambda j: (0, j)),
        out_shape=jax.ShapeDtypeStruct((R, N), F32),
        name="ada",
    )(c, w, b)


def _block_ref(b, block, idx):
    C, W = b.shape
    if block >= 8:
        b3 = b.reshape(C // block, block, W)
        r = jnp.broadcast_to(b3[:, idx:idx + 1, :], b3.shape)
        return r.reshape(C, W)
    row = lax.broadcasted_iota(jnp.int32, (C, W), 0)
    p = row % block
    out = b
    for q in range(block):
        sh = q - idx
        if sh == 0:
            continue
        rolled = pltpu.roll(b, sh % C, 0)
        out = jnp.where(p == q, rolled, out)
    return out


def _head_norm_gate(o, g_row, gate):
    y = o * lax.rsqrt(jnp.mean(o * o, axis=-1, keepdims=True) + EPS)
    return (y * g_row) * _silu(gate)


def _mixer_chunk(r0, proj_scr, cos_ref, sin_ref, rd_ref, idec_ref, kdec_ref, lb_ref, lvl_ref,
                 tri_ref, retg_ref, hgg_ref, sr_scr, sht_scr, o_scr, *, C, min_s):
    W = HEADS * DH
    rows = pl.ds(r0, C)
    proj = proj_scr[rows, :]
    cosf = cos_ref[rows, :]
    sinf = sin_ref[rows, :]

    G = 2 if C % DH == 0 else 1
    groups = [tuple(range(h, h + G)) for h in range(0, HEADS, G)]

    for hs in groups:
        qb_parts, kr_parts, v_parts = [], [], []
        for h in hs:
            c0 = h * DH
            q = proj[:, c0:c0 + DH]
            k = proj[:, W + c0:W + c0 + DH]
            qb_parts.append((q * cosf + pltpu.roll(q, DH // 2, 1) * sinf).astype(BF16))
            kr_parts.append((k * cosf + pltpu.roll(k, DH // 2, 1) * sinf) * (DH ** -0.5))
            v_parts.append(proj[:, 2 * W + c0:2 * W + c0 + DH].astype(BF16))
        qb = _lanes(qb_parts)
        A = (_dot_nt(qb, _block_diag([kr.astype(BF16) for kr in kr_parts]))
             * _lanes([rd_ref[h] for h in hs]))
        S_parts = [sr_scr[h] for h in hs]
        o = (_dot(A.astype(BF16), _block_diag(v_parts))
             + _dot(qb, _block_diag([S.astype(BF16) for S in S_parts]))
             * _lanes([idec_ref[h] for h in hs]))
        for i, h in enumerate(hs):
            c0 = h * DH
            sr_scr[h] = (idec_ref[h][C - 1:C, :] * S_parts[i]
                         + _dot_tn((kr_parts[i] * kdec_ref[h]).astype(BF16), v_parts[i]))
            g = proj[:, 3 * W + c0:3 * W + c0 + DH]
            o_scr[rows, c0:c0 + DH] = _head_norm_gate(
                o[:, i * DH:(i + 1) * DH], retg_ref[:, c0:c0 + DH], g).astype(BF16)

    z = proj[:, 5 * W:6 * W]
    lb = lb_ref[...]
    e = jnp.exp(-jnp.abs(z))
    r = 1.0 / (1.0 + e)
    er = e * r
    sig = jnp.where(z >= 0, r, er)
    nsig = jnp.where(z >= 0, er, r)
    logf = jnp.log(lb + (1.0 - lb) * sig)
    kk_all = (1.0 - lb) * nsig
    tri = tri_ref[...]
    l1, l2, _ = _split3(logf)
    b_all = _dot(tri, l1) + _dot(tri, l2)

    GW = G * DH
    lvl = _lanes([lvl_ref[...]] * G)
    rowc = lax.broadcasted_iota(jnp.int32, (C, GW), 0)
    levels = []
    s = C // 2
    while s >= min_s:
        isq = (rowc % (2 * s)) >= s
        levels.append((s, isq, jnp.where(isq, 1.0, -1.0)))
        s //= 2

    def heads_of(a):
        return [a[:, i * DH:(i + 1) * DH] for i in range(G)]

    for hs in groups:
        c0 = hs[0] * DH
        q = proj[:, 4 * W + c0:4 * W + c0 + GW] * (DH ** -0.5)
        kk = kk_all[:, c0:c0 + GW]
        b = b_all[:, c0:c0 + GW]
        v_parts = heads_of(proj[:, 6 * W + c0:6 * W + c0 + GW])
        vb_parts = [v.astype(BF16) for v in v_parts]

        A = jnp.zeros((C, G * C), F32)
        for s, isq, sign in levels:
            ref = _block_ref(b, 2 * s, s - 1)
            zz = jnp.where(isq, q, kk) * jnp.exp((b - ref) * sign)
            zb = zz.astype(BF16)
            A = jnp.where(lvl == s.bit_length(), _dot_nt(zb, _block_diag(heads_of(zb))), A)
        if min_s > 1:
            ref = _block_ref(b, min_s, min_s // 2)
            d = b - ref
            P = _dot_nt((q * jnp.exp(d)).astype(BF16),
                        _block_diag(heads_of((kk * jnp.exp(-d)).astype(BF16))))
            A = jnp.where(lvl == 0, P, A)
            o = _dot(A.astype(BF16), _block_diag(vb_parts))
        else:
            diag = [jnp.sum(qh * kh, axis=-1, keepdims=True) * v
                    for qh, kh, v in zip(heads_of(q), heads_of(kk), v_parts)]
            o = _dot(A.astype(BF16), _block_diag(vb_parts)) + _lanes(diag)

        ST_parts = [sht_scr[h] for h in hs]
        bl = b[C - 1:C, :]
        o = o + _dot_nt((q * jnp.exp(b)).astype(BF16),
                        _block_diag([ST.astype(BF16) for ST in ST_parts]))
        kdec_parts = heads_of((kk * jnp.exp(bl - b)).astype(BF16))
        ebl_parts = heads_of(jnp.exp(bl))
        for i, h in enumerate(hs):
            ch = h * DH
            sht_scr[h] = ST_parts[i] * ebl_parts[i] + _dot_tn(vb_parts[i], kdec_parts[i])
            g = proj[:, 7 * W + ch:7 * W + ch + DH]
            o_scr[rows, W + ch:W + ch + DH] = _head_norm_gate(
                o[:, i * DH:(i + 1) * DH], hgg_ref[:, ch:ch + DH], g).astype(BF16)


def _mixer_kernel(x_ref, xnext_ref, mod_ref, n1g_ref, win_ref, wout_ref, cos_ref, sin_ref, rd_ref,
                  idec_ref, kdec_ref, lb_ref, lvl_ref, tri_ref, retg_ref, hgg_ref,
                  sr0_ref, sh0_ref, n2g_ref, wr_ref, br_ref,
                  x1_ref, h2_ref, cls_ref, sro_ref, sho_ref,
                  sr_scr, sht_scr, proj_scr, o_scr, *, C, min_s, with_cls, has_next):
    t = pl.program_id(1)
    nt = pl.num_programs(1)
    TT = x_ref.shape[1]
    n_chunks = TT // C

    mod = mod_ref[0]
    sh1 = mod[:, 0:D_MODEL]
    sc1 = mod[:, D_MODEL:2 * D_MODEL]
    gt1 = mod[:, 2 * D_MODEL:3 * D_MODEL]

    PR = xnext_ref.shape[1]
    per_block = PR // C

    def project(xc, r0):
        xn = xc * lax.rsqrt(jnp.mean(xc * xc, axis=-1, keepdims=True) + EPS) * n1g_ref[...]
        hmod = xn * (1.0 + sc1) + sh1
        proj_scr[r0:r0 + PR, :] = _dot(hmod.astype(BF16), win_ref[...])

    @pl.when(t == 0)
    def _():
        for h in range(HEADS):
            sr_scr[h] = sr0_ref[0, h]
            sht_scr[h] = sh0_ref[0, h].T
        project(x_ref[0, 0:PR, :], 0)

    chunk = functools.partial(
        _mixer_chunk, proj_scr=proj_scr, cos_ref=cos_ref, sin_ref=sin_ref, rd_ref=rd_ref,
        idec_ref=idec_ref, kdec_ref=kdec_ref, lb_ref=lb_ref, lvl_ref=lvl_ref, tri_ref=tri_ref,
        retg_ref=retg_ref, hgg_ref=hgg_ref, sr_scr=sr_scr, sht_scr=sht_scr, o_scr=o_scr,
        C=C, min_s=min_s)
    for c in range(n_chunks):
        chunk(c * C)
        if c % per_block == 0:
            r1 = (c // per_block + 1) * PR
            if r1 < TT:
                project(x_ref[0, r1:r1 + PR, :], r1)
            elif has_next:
                project(xnext_ref[0], 0)

    x = x_ref[0]
    mix = _dot(o_scr[...], wout_ref[...])
    x1 = x + gt1 * mix
    x1_ref[0] = x1

    sh2 = mod[:, 3 * D_MODEL:4 * D_MODEL]
    sc2 = mod[:, 4 * D_MODEL:5 * D_MODEL]
    x1n = x1 * lax.rsqrt(jnp.mean(x1 * x1, axis=-1, keepdims=True) + EPS) * n2g_ref[...]
    h2 = x1n * (1.0 + sc2) + sh2
    h2_ref[0] = h2
    if with_cls:
        g_idx, i1, i2 = _route_topk(_router_logits(h2, wr_ref, br_ref).T)[:3]
        lo = jnp.minimum(i1, i2)
        hi = jnp.maximum(i1, i2)
        pair = hi - 1.0 + jnp.where(lo == 1.0, 2.0, 0.0) + jnp.where(lo == 2.0, 3.0, 0.0)
        cls_ref[0] = (g_idx * float(N_PAIRS) + pair).astype(jnp.int32)
    else:
        cls_ref[0] = jnp.zeros((1, TT), jnp.int32)

    @pl.when(t == nt - 1)
    def _():
        for h in range(HEADS):
            sro_ref[0, h] = sr_scr[h]
            sho_ref[0, h] = sht_scr[h].T


def _mixer(x, mod, n1g, win, wout, tabs, retg, hgg, sr0, sh0, n2g, wr, br, *, C, TT, min_s,
           with_cls):
    B, T, D = x.shape
    cosf, sinf, rd, idec, kdec, lb, lvl, tri = tabs
    nt = T // TT
    const2 = lambda b, t: (0, 0)
    const3 = lambda b, t: (0, 0, 0)
    state_spec = pl.BlockSpec((1, HEADS, DH, DH), lambda b, t: (b, 0, 0, 0))
    PR = min(TT, MIXER_PROJ_ROWS)
    per_tile = TT // PR
    last_block = T // PR - 1
    in_specs = [
        pl.BlockSpec((1, TT, D), lambda b, t: (b, t, 0)),
        pl.BlockSpec((1, PR, D),
                     lambda b, t: (b, jnp.minimum((t + 1) * per_tile, last_block), 0)),
        pl.BlockSpec((1, 1, 6 * D), lambda b, t: (b, 0, 0)),
        pl.BlockSpec((1, D), const2),
        pl.BlockSpec((D, D_IN), const2),
        pl.BlockSpec((D, D), const2),
        pl.BlockSpec((TT, DH), lambda b, t: (t, 0)),
        pl.BlockSpec((TT, DH), lambda b, t: (t, 0)),
        pl.BlockSpec((HEADS, C, C), const3),
        pl.BlockSpec((HEADS, C, DH), const3),
        pl.BlockSpec((HEADS, C, DH), const3),
        pl.BlockSpec((1, HEADS * DH), const2),
        pl.BlockSpec((C, C), const2),
        pl.BlockSpec((C, C), const2),
        pl.BlockSpec((1, HEADS * DH), const2),
        pl.BlockSpec((1, HEADS * DH), const2),
        state_spec, state_spec,
        pl.BlockSpec((1, D), const2),
        pl.BlockSpec((D, 2 * ROUTE_LANES), const2),
        pl.BlockSpec((1, ROUTE_LANES), const2),
    ]
    tok_spec = pl.BlockSpec((1, TT, D), lambda b, t: (b, t, 0))
    out_specs = [tok_spec, tok_spec,
                 pl.BlockSpec((1, 1, TT), lambda b, t: (b * nt + t, 0, 0)),
                 state_spec, state_spec]
    out_shape = [jax.ShapeDtypeStruct((B, T, D), F32),
                 jax.ShapeDtypeStruct((B, T, D), F32),
                 jax.ShapeDtypeStruct((B * nt, 1, TT), jnp.int32),
                 jax.ShapeDtypeStruct((B, HEADS, DH, DH), F32),
                 jax.ShapeDtypeStruct((B, HEADS, DH, DH), F32)]
    return pl.pallas_call(
        functools.partial(_mixer_kernel, C=C, min_s=min_s, with_cls=with_cls, has_next=nt > 1),
        grid=(B, nt),
        in_specs=in_specs,
        out_specs=out_specs,
        out_shape=out_shape,
        scratch_shapes=[pltpu.VMEM((HEADS, DH, DH), F32),
                        pltpu.VMEM((HEADS, DH, DH), F32),
                        pltpu.VMEM((TT, D_IN), F32),
                        pltpu.VMEM((TT, D), BF16)],
        compiler_params=pltpu.CompilerParams(
            dimension_semantics=("arbitrary", "arbitrary"),
            vmem_limit_bytes=V7X_VMEM_LIMIT),
        name="mixer",
    )(x, x, mod, n1g, win, wout, cosf, sinf, rd, idec, kdec, lb, lvl, tri, retg, hgg, sr0, sh0,
      n2g, wr, br)


def _router_logits(h, wr_ref, br_ref):
    h1, h2, _ = _split3(h)
    hi = _dot(h1, wr_ref[...])
    lo = _dot(h2, wr_ref[:, 0:ROUTE_LANES])
    return hi[:, 0:ROUTE_LANES] + (hi[:, ROUTE_LANES:] + lo) + br_ref[...]


def _route_topk(logits_t):
    TM = logits_t.shape[1]
    lg = logits_t[0:N_GROUPS, :]
    gi = lax.broadcasted_iota(jnp.int32, (N_GROUPS, TM), 0).astype(F32)
    m = jnp.max(lg, axis=0, keepdims=True)
    g_w = 1.0 / jnp.sum(jnp.exp(lg - m), axis=0, keepdims=True)
    g_idx = jnp.min(jnp.where(lg == m, gi, float(N_GROUPS)), axis=0, keepdims=True)
    el = jnp.zeros((EXP_PER_GROUP, TM), F32)
    for g in range(N_GROUPS):
        lo = N_GROUPS + g * EXP_PER_GROUP
        el = jnp.where(g_idx == float(g), logits_t[lo:lo + EXP_PER_GROUP, :], el)
    ei = lax.broadcasted_iota(jnp.int32, (EXP_PER_GROUP, TM), 0).astype(F32)
    m1 = jnp.max(el, axis=0, keepdims=True)
    i1 = jnp.min(jnp.where(el == m1, ei, float(EXP_PER_GROUP)), axis=0, keepdims=True)
    el2 = jnp.where(ei == i1, -jnp.inf, el)
    m2 = jnp.max(el2, axis=0, keepdims=True)
    i2 = jnp.min(jnp.where(el2 == m2, ei, float(EXP_PER_GROUP)), axis=0, keepdims=True)
    p2 = jnp.exp(m2 - m1)
    den = 1.0 / (1.0 + p2)
    w0 = g_w * den
    w1 = g_w * (p2 * den)
    return g_idx, i1, i2, w0, w1


def _moe_kernel(x1_ref, h2_ref, mod_ref, wr_ref, br_ref, wg_ref, wu_ref, wd_ref, fg_ref,
                y_ref, h_scr, route_scr, acc_scr):
    e = pl.program_id(1)
    ne = pl.num_programs(1)
    TM, D = x1_ref.shape
    K = mod_ref.shape[0]
    TQ = TM // K

    @pl.when(e == 0)
    def _():
        h = h2_ref[...]
        h_scr[...] = h.astype(BF16)
        g_idx, i1, i2, w0, wgt1 = _route_topk(_router_logits(h, wr_ref, br_ref).T)
        e0 = g_idx * EXP_PER_GROUP + i1
        e1 = g_idx * EXP_PER_GROUP + i2
        ri = lax.broadcasted_iota(jnp.int32, (ROUTE_LANES, TM), 0)
        rows = jnp.where(ri == 0, e0, jnp.where(ri == 1, e1, jnp.where(ri == 2, w0,
                         jnp.where(ri == 3, wgt1, 0.0))))
        route_scr[...] = rows.T
        acc_scr[...] = jnp.zeros_like(acc_scr)

    hb = h_scr[...]
    hid = _silu(_dot(hb, wg_ref[0])) * _dot(hb, wu_ref[0])
    out = _dot(hid.astype(BF16), wd_ref[0])
    ef = e.astype(F32)
    route = route_scr[...]
    comb = (jnp.where(route[:, 0:1] == ef, route[:, 2:3], 0.0)
            + jnp.where(route[:, 1:2] == ef, route[:, 3:4], 0.0))
    acc_scr[...] += comb * out

    @pl.when(e == ne - 1)
    def _():
        gt2 = mod_ref[...][:, :, 5 * D:6 * D]
        ff = (gt2 * acc_scr[...].reshape(K, TQ, D)).reshape(TM, D)
        x2 = x1_ref[...] + ff
        y_ref[...] = x2 * lax.rsqrt(jnp.mean(x2 * x2, axis=-1, keepdims=True) + EPS) * fg_ref[...]


def _moe(x1, h2, mod, wr, br, wg, wu, wd, fg, *, T, TM):
    N, D = x1.shape
    if T >= TM:
        assert T % TM == 0
        seqs, tiles_per_seq = 1, T // TM
        mod_map = lambda i, e: (i // tiles_per_seq, 0, 0)
    else:
        assert TM % T == 0
        seqs = TM // T
        mod_map = lambda i, e: (i, 0, 0)
    return pl.pallas_call(
        _moe_kernel,
        grid=(N // TM, N_EXPERTS),
        in_specs=[
            pl.BlockSpec((TM, D), lambda i, e: (i, 0)),
            pl.BlockSpec((TM, D), lambda i, e: (i, 0)),
            pl.BlockSpec((seqs, 1, 6 * D), mod_map),
            pl.BlockSpec((D, 2 * ROUTE_LANES), lambda i, e: (0, 0)),
            pl.BlockSpec((1, ROUTE_LANES), lambda i, e: (0, 0)),
            pl.BlockSpec((1, D, D_EXPERT), lambda i, e: (e, 0, 0)),
            pl.BlockSpec((1, D, D_EXPERT), lambda i, e: (e, 0, 0)),
            pl.BlockSpec((1, D_EXPERT, D), lambda i, e: (e, 0, 0)),
            pl.BlockSpec((1, D), lambda i, e: (0, 0)),
        ],
        out_specs=pl.BlockSpec((TM, D), lambda i, e: (i, 0)),
        out_shape=jax.ShapeDtypeStruct((N, D), F32),
        scratch_shapes=[pltpu.VMEM((TM, D), BF16),
                        pltpu.VMEM((TM, ROUTE_LANES), F32),
                        pltpu.VMEM((TM, D), F32)],
        compiler_params=pltpu.CompilerParams(
            dimension_semantics=("arbitrary", "arbitrary"),
            vmem_limit_bytes=V7X_VMEM_LIMIT),
        name="moe",
    )(x1, h2, mod, wr, br, wg, wu, wd, fg)


INFO_TAIL_LEN = 32
INFO_END = 64


def _rank_kernel(cls_ref, pos_ref, tcls_ref, info_ref, *, TG):
    R, L = cls_ref.shape
    info_lane = lax.broadcasted_iota(jnp.int32, info_ref.shape, 1)
    info = jnp.zeros(info_ref.shape, F32)
    cf = cls_ref[...].astype(F32)
    ri = lax.broadcasted_iota(jnp.int32, (L, L), 0)
    ci = lax.broadcasted_iota(jnp.int32, (L, L), 1)
    upper = (ri < ci).astype(BF16)
    rr = lax.broadcasted_iota(jnp.int32, (R, R), 0)
    rc = lax.broadcasted_iota(jnp.int32, (R, R), 1)
    lower = (rc < rr).astype(BF16)
    tile_start = lax.broadcasted_iota(jnp.int32, tcls_ref.shape, 1).astype(F32) * float(TG)
    base = jnp.zeros((1, 1), F32)
    pos = jnp.zeros((R, L), F32)
    tcls = jnp.zeros(tcls_ref.shape, F32)
    for c in range(N_CLASSES):
        ind = jnp.where(cf == float(c), 1.0, 0.0)
        lane_pre = _dot(ind.astype(BF16), upper)
        row_tot = jnp.broadcast_to(jnp.sum(ind, axis=1, keepdims=True), (R, L))
        row_pre = _dot(lower, row_tot.astype(BF16))
        total = jnp.sum(row_tot[:, 0:1], axis=0, keepdims=True)
        pos = jnp.where(ind > 0.0, base + row_pre + lane_pre, pos)
        padded = jnp.floor((total + float(TG - 1)) * (1.0 / TG)) * float(TG)
        info = jnp.where(info_lane == c, base + total, info)
        info = jnp.where(info_lane == INFO_TAIL_LEN + c, padded - total, info)
        base = base + padded
        tcls = tcls + jnp.where(tile_start >= base, 1.0, 0.0)
    info = jnp.where(info_lane == INFO_END, base, info)
    pos_ref[...] = pos.astype(jnp.int32)
    tcls_ref[...] = tcls.astype(jnp.int32)
    info_ref[...] = info.astype(jnp.int32)


def _rank(cls2d, *, TG, n_tiles):
    R, L = cls2d.shape
    lanes = -(-n_tiles // 128) * 128
    return pl.pallas_call(
        functools.partial(_rank_kernel, TG=TG),
        out_shape=(jax.ShapeDtypeStruct((R, L), jnp.int32),
                   jax.ShapeDtypeStruct((1, lanes), jnp.int32),
                   jax.ShapeDtypeStruct((1, 128), jnp.int32)),
        name="rank",
    )(cls2d)


SUBLANES = 8


def _for_each_row(n_rows, fn):
    def group(g, carry):
        r0 = pl.multiple_of(g * SUBLANES, SUBLANES)
        for j in range(SUBLANES):
            fn(pl.ds(r0, SUBLANES), j, r0 + j)
        return carry
    lax.fori_loop(0, n_rows // SUBLANES, group, 0)


def _dispatch_kernel(info_ref, pos_ref, h2_ref, xs_hbm, zero_scr, sem, zsem, *, TD):
    ZR = zero_scr.shape[0]

    def tail_copies(do):
        for c in range(N_CLASSES):
            start = info_ref[c]
            head = (-start) & (SUBLANES - 1)
            for i in range(SUBLANES - 1):
                @pl.when(i < head)
                def _(i=i):
                    do(pltpu.make_async_copy(zero_scr.at[pl.ds(0, 1)],
                                             xs_hbm.at[pl.ds(start + i, 1)], zsem))
            rest = info_ref[INFO_TAIL_LEN + c] - head
            off = start + head
            k = ZR
            while k >= SUBLANES:
                @pl.when((rest & k) != 0)
                def _(k=k, off=off):
                    do(pltpu.make_async_copy(
                        zero_scr.at[pl.ds(0, k)],
                        xs_hbm.at[pl.ds(pl.multiple_of(off, SUBLANES), k)], zsem))
                off = off + (rest & k)
                k //= 2

    def unused_copies(do):
        first = info_ref[INFO_END] // ZR

        def body(i, carry):
            do(pltpu.make_async_copy(
                zero_scr, xs_hbm.at[pl.ds(pl.multiple_of(i * ZR, ZR), ZR)], zsem))
            return carry
        lax.fori_loop(first, xs_hbm.shape[0] // ZR, body, 0)

    @pl.when(pl.program_id(0) == 0)
    def _():
        zero_scr[...] = jnp.zeros_like(zero_scr)
        tail_copies(lambda cp: cp.start())
        unused_copies(lambda cp: cp.start())
        tail_copies(lambda cp: cp.wait())
        unused_copies(lambda cp: cp.wait())

    def issue(tile, j, r):
        pltpu.make_async_copy(h2_ref.at[tile].at[pl.ds(j, 1)],
                              xs_hbm.at[pl.ds(pos_ref[r], 1)], sem).start()

    _for_each_row(TD, issue)
    pltpu.make_async_copy(h2_ref, xs_hbm.at[pl.ds(0, TD)], sem).wait()


def _dispatch(info, pos, h2, *, n_rows, TG, TD):
    N, D = h2.shape
    grid_spec = pltpu.PrefetchScalarGridSpec(
        num_scalar_prefetch=1,
        grid=(N // TD,),
        in_specs=[pl.BlockSpec((TD,), lambda i, info: (i,), memory_space=pltpu.SMEM),
                  pl.BlockSpec((TD, D), lambda i, info: (i, 0))],
        out_specs=pl.BlockSpec(memory_space=pl.ANY),
        scratch_shapes=[pltpu.VMEM((TG // 2, D), F32),
                        pltpu.SemaphoreType.DMA(()), pltpu.SemaphoreType.DMA(())],
    )
    return pl.pallas_call(
        functools.partial(_dispatch_kernel, TD=TD),
        grid_spec=grid_spec,
        out_shape=jax.ShapeDtypeStruct((n_rows, D), F32),
        compiler_params=pltpu.CompilerParams(
            dimension_semantics=("arbitrary",), vmem_limit_bytes=V7X_VMEM_LIMIT),
        name="dispatch",
    )(info, pos, h2)


def _expert_kernel(ea_ref, eb_ref, nv_ref, xs_ref, wr_ref, br_ref, wga_ref, wua_ref, wda_ref,
                   wgb_ref, wub_ref, wdb_ref, ys_ref):
    j = pl.program_id(0)
    TG = xs_ref.shape[0]
    n_valid = nv_ref[0]

    @pl.when(j < n_valid)
    def _():
        a = ea_ref[j]
        b = eb_ref[j]
        g = lax.div(a, jnp.int32(EXP_PER_GROUP))
        hb = xs_ref[...].astype(BF16)
        logits = _dot(hb, wr_ref[:, 0:ROUTE_LANES]) + br_ref[...]
        lane = lax.broadcasted_iota(jnp.int32, (TG, ROUTE_LANES), 1)
        isg = lane < N_GROUPS
        m = jnp.max(jnp.where(isg, logits, -jnp.inf), axis=1, keepdims=True)
        ssum = jnp.sum(jnp.where(isg, jnp.exp(logits - m), 0.0), axis=1, keepdims=True)
        lg = jnp.sum(jnp.where(lane == g, logits, 0.0), axis=1, keepdims=True)
        la = jnp.sum(jnp.where(lane == N_GROUPS + a, logits, 0.0), axis=1, keepdims=True)
        lb = jnp.sum(jnp.where(lane == N_GROUPS + b, logits, 0.0), axis=1, keepdims=True)
        g_w = jnp.exp(lg - m) / ssum
        mm = jnp.maximum(la, lb)
        pa = jnp.exp(la - mm)
        pb = jnp.exp(lb - mm)
        den = g_w / (pa + pb)
        hid_a = _silu(_dot(hb, wga_ref[0])) * _dot(hb, wua_ref[0])
        out = (pa * den) * _dot(hid_a.astype(BF16), wda_ref[0])
        hid_b = _silu(_dot(hb, wgb_ref[0])) * _dot(hb, wub_ref[0])
        ys_ref[...] = out + (pb * den) * _dot(hid_b.astype(BF16), wdb_ref[0])

    @pl.when(j >= n_valid)
    def _():
        ys_ref[...] = jnp.zeros_like(ys_ref)


def _experts(ea, eb, n_valid, xs, wr, br, wg, wu, wd, *, TG):
    NS, D = xs.shape
    const2 = lambda j, ea, eb, nv: (0, 0)
    wmap_a = lambda j, ea, eb, nv: (ea[j], 0, 0)
    wmap_b = lambda j, ea, eb, nv: (eb[j], 0, 0)
    grid_spec = pltpu.PrefetchScalarGridSpec(
        num_scalar_prefetch=3,
        grid=(NS // TG,),
        in_specs=[
            pl.BlockSpec((TG, D), lambda j, ea, eb, nv: (jnp.minimum(j, nv[0] - 1), 0)),
            pl.BlockSpec((D, 2 * ROUTE_LANES), const2),
            pl.BlockSpec((1, ROUTE_LANES), const2),
            pl.BlockSpec((1, D, D_EXPERT), wmap_a),
            pl.BlockSpec((1, D, D_EXPERT), wmap_a),
            pl.BlockSpec((1, D_EXPERT, D), wmap_a),
            pl.BlockSpec((1, D, D_EXPERT), wmap_b),
            pl.BlockSpec((1, D, D_EXPERT), wmap_b),
            pl.BlockSpec((1, D_EXPERT, D), wmap_b),
        ],
        out_specs=pl.BlockSpec((TG, D), lambda j, ea, eb, nv: (j, 0)),
    )
    return pl.pallas_call(
        _expert_kernel,
        grid_spec=grid_spec,
        out_shape=jax.ShapeDtypeStruct((NS, D), F32),
        compiler_params=pltpu.CompilerParams(
            dimension_semantics=("arbitrary",), vmem_limit_bytes=V7X_VMEM_LIMIT),
        name="experts",
    )(ea, eb, n_valid, xs, wr, br, wg, wu, wd, wg, wu, wd)


def _combine_kernel(pos_ref, posn_ref, x1_ref, mod_ref, fg_ref, ys_hbm, y_ref, buf, sems, *, TC):
    i = pl.program_id(0)
    n = pl.num_programs(0)
    slot = i % 2

    def gather(p_ref, s):
        def issue(tile, j, r):
            pltpu.make_async_copy(ys_hbm.at[pl.ds(p_ref[r], 1)],
                                  buf.at[s].at[tile].at[pl.ds(j, 1)], sems.at[s]).start()
        _for_each_row(TC, issue)

    @pl.when(i == 0)
    def _():
        gather(pos_ref, 0)

    @pl.when(i + 1 < n)
    def _():
        gather(posn_ref, 1 - slot)

    pltpu.make_async_copy(ys_hbm.at[pl.ds(0, TC)], buf.at[slot], sems.at[slot]).wait()
    gt2 = mod_ref[0][:, 5 * D_MODEL:6 * D_MODEL]
    x2 = x1_ref[...] + gt2 * buf[slot]
    y_ref[...] = x2 * lax.rsqrt(jnp.mean(x2 * x2, axis=-1, keepdims=True) + EPS) * fg_ref[...]


def _combine(pos, x1, mod, fg, ys, *, T, TC):
    N, D = x1.shape
    n = N // TC
    tiles_per_seq = T // TC
    return pl.pallas_call(
        functools.partial(_combine_kernel, TC=TC),
        grid=(n,),
        in_specs=[pl.BlockSpec((TC,), lambda i: (i,), memory_space=pltpu.SMEM),
                  pl.BlockSpec((TC,), lambda i: (jnp.minimum(i + 1, n - 1),),
                               memory_space=pltpu.SMEM),
                  pl.BlockSpec((TC, D), lambda i: (i, 0)),
                  pl.BlockSpec((1, 1, 6 * D), lambda i: (i // tiles_per_seq, 0, 0)),
                  pl.BlockSpec((1, D), lambda i: (0, 0)),
                  pl.BlockSpec(memory_space=pl.ANY)],
        out_specs=pl.BlockSpec((TC, D), lambda i: (i, 0)),
        out_shape=jax.ShapeDtypeStruct((N, D), F32),
        scratch_shapes=[pltpu.VMEM((2, TC, D), F32), pltpu.SemaphoreType.DMA((2,))],
        compiler_params=pltpu.CompilerParams(
            dimension_semantics=("arbitrary",), vmem_limit_bytes=V7X_VMEM_LIMIT),
        name="combine",
    )(pos, pos, x1, mod, fg, ys)


def _routed_moe(x1, h2, cls, mod, p, *, T):
    N, D = x1.shape
    TG = ROUTED_TILE
    n_tiles = N // TG + N_CLASSES
    pos2d, tcls, info = _rank(cls.reshape(N // 128, 128), TG=TG, n_tiles=n_tiles)
    pos = pos2d.reshape(N)
    tcls = tcls[0, :n_tiles]
    ea = jnp.asarray(CLASS_EXPERT_A)[tcls]
    eb = jnp.asarray(CLASS_EXPERT_B)[tcls]
    n_valid = (info[0, INFO_END:INFO_END + 1] // TG).astype(jnp.int32)
    xs = _dispatch(info.reshape(-1), pos, h2, n_rows=n_tiles * TG, TG=TG, TD=DISPATCH_TILE)
    ys = _experts(ea, eb, n_valid, xs, p["wr"], p["br"], p["wg"], p["wu"], p["wd"], TG=TG)
    return _combine(pos, x1, mod, p["fg"], ys, T=T, TC=COMBINE_TILE)


def _tiling(T):
    if T <= REF_CHUNK:
        return T, T
    return MIXER_CHUNK, min(T, MIXER_TILE)


def _trunk(x, mod, s_ret, s_hg, pos0, inv2, p, *, min_s=HGRN_MIN_BLOCK):
    B, T, D = x.shape
    N = B * T
    C, TT = _tiling(T)
    min_s = min(min_s, C // 2)
    routed = N >= ROUTED_MIN_TOKENS
    tabs = _tables(inv2, p["lb_logits"], T=T, C=C, pos0=pos0, min_s=min_s)
    mod3 = mod.reshape(B, 1, 6 * D)
    x1, h2, cls, sr, sh = _mixer(x, mod3, p["n1g"], p["win"], p["wout"], tabs, p["retg"], p["hgg"],
                                 s_ret, s_hg, p["n2g"], p["wr"], p["br"], C=C, TT=TT,
                                 min_s=min_s, with_cls=routed)
    x1 = x1.reshape(N, D)
    h2 = h2.reshape(N, D)
    if routed:
        y = _routed_moe(x1, h2, cls, mod3, p, T=T)
    else:
        y = _moe(x1, h2, mod3, p["wr"], p["br"], p["wg"], p["wu"], p["wd"], p["fg"],
                 T=T, TM=min(N, 1024))
    return y.reshape(B, T, D), sr[None], sh[None]


def _prepare(W_ada, b_ada, norm1_g, norm2_g, W_in, ret_norm_g, hgrn_norm_g, hgrn_lb_logits, W_out,
             W_router_group, b_router_group, W_router_expert, b_router_expert, W_gate_e, W_up_e,
             W_down_e, final_norm_g):
    assert W_in.shape[0] == 1, "single-layer stack only"
    D = D_MODEL
    wr = jnp.concatenate([W_router_group[0], W_router_expert[0]], axis=1)
    wr = jnp.pad(wr, ((0, 0), (0, ROUTE_LANES - wr.shape[1])))
    wr1 = wr.astype(BF16)
    wr2 = (wr - wr1.astype(F32)).astype(BF16)
    br = jnp.concatenate([b_router_group[0], b_router_expert[0]])
    br = jnp.pad(br, (0, ROUTE_LANES - br.shape[0])).reshape(1, ROUTE_LANES)
    return dict(
        n1g=norm1_g[0].reshape(1, D), n2g=norm2_g[0].reshape(1, D), fg=final_norm_g.reshape(1, D),
        win=W_in[0].astype(BF16), wout=W_out[0].astype(BF16),
        retg=ret_norm_g[0].reshape(1, -1), hgg=hgrn_norm_g[0].reshape(1, -1),
        lb_logits=hgrn_lb_logits,
        wr=jnp.concatenate([wr1, wr2], axis=1), br=br,
        wg=W_gate_e[0].astype(BF16), wu=W_up_e[0].astype(BF16), wd=W_down_e[0].astype(BF16),
    )


def _rope_inv():
    half = DH // 2
    inv = 1.0 / (ROPE_BASE ** (jnp.arange(half, dtype=F32) / half))
    return jnp.concatenate([inv, inv]).reshape(1, DH)


def kernel(x_prompt, x_sample, c_prompt, c_sample, state_ret, state_hgrn, W_ada, b_ada, norm1_g,
           norm2_g, W_in, ret_norm_g, hgrn_norm_g, hgrn_lb_logits, W_out, W_router_group,
           b_router_group, W_router_expert, b_router_expert, W_gate_e, W_up_e, W_down_e,
           final_norm_g):
    p = _prepare(W_ada, b_ada, norm1_g, norm2_g, W_in, ret_norm_g, hgrn_norm_g, hgrn_lb_logits,
                 W_out, W_router_group, b_router_group, W_router_expert, b_router_expert,
                 W_gate_e, W_up_e, W_down_e, final_norm_g)
    Bp = x_prompt.shape[0]
    Bs = x_sample.shape[0]
    inv2 = _rope_inv()
    mod = _ada(jnp.concatenate([c_prompt, c_sample], axis=0), W_ada[0], b_ada[0].reshape(1, -1))
    zeros = jnp.zeros((Bp, HEADS, DH, DH), F32)
    y_p, sr_p, sh_p = _trunk(x_prompt, mod[:Bp], zeros, zeros, 0, inv2, p)
    y_s, sr_s, sh_s = _trunk(x_sample, mod[Bp:Bp + Bs], state_ret[0], state_hgrn[0], PAST_LEN,
                             inv2, p)
    return (y_p, y_s, sr_p, sh_p, sr_s, sh_s)
```

```python
import functools

import numpy as np
import jax
import jax.numpy as jnp
from jax import lax
from jax.experimental import pallas as pl
from jax.experimental.pallas import tpu as pltpu

F32 = jnp.float32
BF16 = jnp.bfloat16

D_MODEL = 1024
HEADS = 4
DH = 128
D_IN = 8 * HEADS * DH
N_GROUPS = 4
EXP_PER_GROUP = 4
N_EXPERTS = N_GROUPS * EXP_PER_GROUP
D_EXPERT = 512
ROPE_BASE = 10000.0
EPS = 1e-6
REF_CHUNK = 64
PAST_LEN = 4096

V7X_VMEM_LIMIT = 56 * 1024 * 1024
ROUTE_LANES = 128

_PAIRS = [(0, 1), (0, 2), (0, 3), (1, 3), (1, 2), (3, 2)]
N_PAIRS = len(_PAIRS)
N_CLASSES = N_GROUPS * N_PAIRS
CLASS_EXPERT_A = np.array([g * EXP_PER_GROUP + i for g in range(N_GROUPS) for i, _ in _PAIRS],
                          np.int32)
CLASS_EXPERT_A = np.append(CLASS_EXPERT_A, CLASS_EXPERT_A[-1])
CLASS_EXPERT_B = np.array([g * EXP_PER_GROUP + j for g in range(N_GROUPS) for _, j in _PAIRS],
                          np.int32)
CLASS_EXPERT_B = np.append(CLASS_EXPERT_B, CLASS_EXPERT_B[-1])
MIXER_CHUNK = 128
HGRN_MIN_BLOCK = 4
MIXER_TILE = 512
MIXER_PROJ_ROWS = 256
ROUTED_MIN_TOKENS = 8192
ROUTED_TILE = 512
DISPATCH_TILE = 1024
COMBINE_TILE = 512


def _dot(a, b):
    return jnp.dot(a, b, preferred_element_type=F32)


def _dot_nt(a, b):
    return lax.dot_general(a, b, (((1,), (1,)), ((), ())), preferred_element_type=F32)


def _dot_tn(a, b):
    return lax.dot_general(a, b, (((0,), (0,)), ((), ())), preferred_element_type=F32)


def _lanes(parts):
    return parts[0] if len(parts) == 1 else jnp.concatenate(parts, axis=1)


def _block_diag(parts):
    if len(parts) == 1:
        return parts[0]
    rows = []
    for i, p in enumerate(parts):
        rows.append(_lanes([p if j == i else jnp.zeros((p.shape[0], q.shape[1]), p.dtype)
                            for j, q in enumerate(parts)]))
    return jnp.concatenate(rows, axis=0)


def _split3(a):
    a1 = a.astype(BF16)
    r1 = a - a1.astype(F32)
    a2 = r1.astype(BF16)
    a3 = (r1 - a2.astype(F32)).astype(BF16)
    return a1, a2, a3


def _silu(a):
    return a * (1.0 / (1.0 + jnp.exp(-a)))


def _tables_kernel(inv_ref, lbl_ref, cos_ref, sin_ref, rd_ref, idec_ref, kdec_ref, lb_ref,
                   lvl_ref, tri_ref, *, T, C, pos0, min_s):
    pos = (lax.broadcasted_iota(jnp.int32, (T, DH), 0) + pos0).astype(F32)
    ang = pos * inv_ref[...]
    lane = lax.broadcasted_iota(jnp.int32, (T, DH), 1)
    cos_ref[...] = jnp.cos(ang)
    s = jnp.sin(ang)
    sin_ref[...] = jnp.where(lane < DH // 2, -s, s)

    li = lax.broadcasted_iota(jnp.int32, (C, C), 0)
    mi = lax.broadcasted_iota(jnp.int32, (C, C), 1)
    diff = (li - mi).astype(F32)
    rowj = lax.broadcasted_iota(jnp.int32, (C, DH), 0).astype(F32)
    for h in range(HEADS):
        lg = jnp.log(jnp.full((1, 1), 1.0 - 2.0 ** (-5.0 - h), F32))
        rd_ref[h] = jnp.where(diff >= 0, jnp.exp(jnp.maximum(diff, 0.0) * lg), 0.0)
        idec_ref[h] = jnp.exp((rowj + 1.0) * lg)
        kdec_ref[h] = jnp.exp((C - 1.0 - rowj) * lg)

    lg_all = lbl_ref[...]
    m = jnp.max(lg_all, axis=0, keepdims=True)
    e = jnp.exp(lg_all - m)
    lb_ref[...] = e[0:1, :] / jnp.sum(e, axis=0, keepdims=True)

    x = li ^ mi
    code = jnp.zeros((C, C), jnp.int32)
    sz = min_s
    while sz < C:
        code = code + (x >= sz).astype(jnp.int32)
        sz *= 2
    code = jnp.where(x >= min_s, code + (min_s.bit_length() - 1), 0)
    lvl_ref[...] = jnp.where(li >= mi, code, -1)
    tri_ref[...] = (li >= mi).astype(BF16)


def _tables(inv2, lb_logits, *, T, C, pos0, min_s):
    out_shape = (
        jax.ShapeDtypeStruct((T, DH), F32),
        jax.ShapeDtypeStruct((T, DH), F32),
        jax.ShapeDtypeStruct((HEADS, C, C), F32),
        jax.ShapeDtypeStruct((HEADS, C, DH), F32),
        jax.ShapeDtypeStruct((HEADS, C, DH), F32),
        jax.ShapeDtypeStruct((1, HEADS * DH), F32),
        jax.ShapeDtypeStruct((C, C), jnp.int32),
        jax.ShapeDtypeStruct((C, C), BF16),
    )
    return pl.pallas_call(
        functools.partial(_tables_kernel, T=T, C=C, pos0=pos0, min_s=min_s),
        out_shape=out_shape,
        name="tables",
    )(inv2, lb_logits)


def _ada_kernel(c_ref, w_ref, b_ref, o_ref):
    a = _silu(c_ref[...])
    w = w_ref[...]
    a1, a2, _ = _split3(a)
    w1, w2, _ = _split3(w)
    o_ref[...] = _dot(a1, w1) + (_dot(a1, w2) + _dot(a2, w1)) + b_ref[...]


def _ada(c, w, b):
    R = c.shape[0]
    N = w.shape[1]
    TN = 1024
    return pl.pallas_call(
        _ada_kernel,
        grid=(N // TN,),
        in_specs=[pl.BlockSpec((R, D_MODEL), lambda j: (0, 0)),
                  pl.BlockSpec((D_MODEL, TN), lambda j: (0, j)),
                  pl.BlockSpec((1, TN), lambda j: (0, j))],
        out_specs=pl.BlockSpec((R, TN), lambda j: (0, j)),
        out_shape=jax.ShapeDtypeStruct((R, N), F32),
        name="ada",
    )(c, w, b)


def _block_ref(b, block, idx):
    C, W = b.shape
    if block >= 8:
        b3 = b.reshape(C // block, block, W)
        r = jnp.broadcast_to(b3[:, idx:idx + 1, :], b3.shape)
        return r.reshape(C, W)
    row = lax.broadcasted_iota(jnp.int32, (C, W), 0)
    p = row % block
    out = b
    for q in range(block):
        sh = q - idx
        if sh == 0:
            continue
        rolled = pltpu.roll(b, sh % C, 0)
        out = jnp.where(p == q, rolled, out)
    return out


def _head_norm_gate(o, g_row, gate):
    y = o * lax.rsqrt(jnp.mean(o * o, axis=-1, keepdims=True) + EPS)
    return (y * g_row) * _silu(gate)


def _mixer_chunk(r0, proj_scr, cos_ref, sin_ref, rd_ref, idec_ref, kdec_ref, lb_ref, lvl_ref,
                 tri_ref, retg_ref, hgg_ref, sr_scr, sht_scr, o_scr, *, C, min_s):
    W = HEADS * DH
    rows = pl.ds(r0, C)
    proj = proj_scr[rows, :]
    cosf = cos_ref[rows, :]
    sinf = sin_ref[rows, :]

    G = 2 if C % DH == 0 else 1
    groups = [tuple(range(h, h + G)) for h in range(0, HEADS, G)]

    for hs in groups:
        qb_parts, kr_parts, v_parts = [], [], []
        for h in hs:
            c0 = h * DH
            q = proj[:, c0:c0 + DH]
            k = proj[:, W + c0:W + c0 + DH]
            qb_parts.append((q * cosf + pltpu.roll(q, DH // 2, 1) * sinf).astype(BF16))
            kr_parts.append((k * cosf + pltpu.roll(k, DH // 2, 1) * sinf) * (DH ** -0.5))
            v_parts.append(proj[:, 2 * W + c0:2 * W + c0 + DH].astype(BF16))
        qb = _lanes(qb_parts)
        A = (_dot_nt(qb, _block_diag([kr.astype(BF16) for kr in kr_parts]))
             * _lanes([rd_ref[h] for h in hs]))
        S_parts = [sr_scr[h] for h in hs]
        o = (_dot(A.astype(BF16), _block_diag(v_parts))
             + _dot(qb, _block_diag([S.astype(BF16) for S in S_parts]))
             * _lanes([idec_ref[h] for h in hs]))
        for i, h in enumerate(hs):
            c0 = h * DH
            sr_scr[h] = (idec_ref[h][C - 1:C, :] * S_parts[i]
                         + _dot_tn((kr_parts[i] * kdec_ref[h]).astype(BF16), v_parts[i]))
            g = proj[:, 3 * W + c0:3 * W + c0 + DH]
            o_scr[rows, c0:c0 + DH] = _head_norm_gate(
                o[:, i * DH:(i + 1) * DH], retg_ref[:, c0:c0 + DH], g).astype(BF16)

    z = proj[:, 5 * W:6 * W]
    lb = lb_ref[...]
    e = jnp.exp(-jnp.abs(z))
    r = 1.0 / (1.0 + e)
    er = e * r
    sig = jnp.where(z >= 0, r, er)
    nsig = jnp.where(z >= 0, er, r)
    logf = jnp.log(lb + (1.0 - lb) * sig)
    kk_all = (1.0 - lb) * nsig
    tri = tri_ref[...]
    l1, l2, _ = _split3(logf)
    b_all = _dot(tri, l1) + _dot(tri, l2)

    GW = G * DH
    lvl = _lanes([lvl_ref[...]] * G)
    rowc = lax.broadcasted_iota(jnp.int32, (C, GW), 0)
    levels = []
    s = C // 2
    while s >= min_s:
        isq = (rowc % (2 * s)) >= s
        levels.append((s, isq, jnp.where(isq, 1.0, -1.0)))
        s //= 2

    def heads_of(a):
        return [a[:, i * DH:(i + 1) * DH] for i in range(G)]

    for hs in groups:
        c0 = hs[0] * DH
        q = proj[:, 4 * W + c0:4 * W + c0 + GW] * (DH ** -0.5)
        kk = kk_all[:, c0:c0 + GW]
        b = b_all[:, c0:c0 + GW]
        v_parts = heads_of(proj[:, 6 * W + c0:6 * W + c0 + GW])
        vb_parts = [v.astype(BF16) for v in v_parts]

        A = jnp.zeros((C, G * C), F32)
        for s, isq, sign in levels:
            ref = _block_ref(b, 2 * s, s - 1)
            zz = jnp.where(isq, q, kk) * jnp.exp((b - ref) * sign)
            zb = zz.astype(BF16)
            A = jnp.where(lvl == s.bit_length(), _dot_nt(zb, _block_diag(heads_of(zb))), A)
        if min_s > 1:
            ref = _block_ref(b, min_s, min_s // 2)
            d = b - ref
            P = _dot_nt((q * jnp.exp(d)).astype(BF16),
                        _block_diag(heads_of((kk * jnp.exp(-d)).astype(BF16))))
            A = jnp.where(lvl == 0, P, A)
            o = _dot(A.astype(BF16), _block_diag(vb_parts))
        else:
            diag = [jnp.sum(qh * kh, axis=-1, keepdims=True) * v
                    for qh, kh, v in zip(heads_of(q), heads_of(kk), v_parts)]
            o = _dot(A.astype(BF16), _block_diag(vb_parts)) + _lanes(diag)

        ST_parts = [sht_scr[h] for h in hs]
        bl = b[C - 1:C, :]
        o = o + _dot_nt((q * jnp.exp(b)).astype(BF16),
                        _block_diag([ST.astype(BF16) for ST in ST_parts]))
        kdec_parts = heads_of((kk * jnp.exp(bl - b)).astype(BF16))
        ebl_parts = heads_of(jnp.exp(bl))
        for i, h in enumerate(hs):
            ch = h * DH
            sht_scr[h] = ST_parts[i] * ebl_parts[i] + _dot_tn(vb_parts[i], kdec_parts[i])
            g = proj[:, 7 * W + ch:7 * W + ch + DH]
            o_scr[rows, W + ch:W + ch + DH] = _head_norm_gate(
                o[:, i * DH:(i + 1) * DH], hgg_ref[:, ch:ch + DH], g).astype(BF16)


def _mixer_kernel(x_ref, xnext_ref, mod_ref, modnext_ref, n1g_ref, win_ref, wout_ref, cos_ref,
                  sin_ref, rd_ref, idec_ref, kdec_ref, lb_ref, lvl_ref, tri_ref, retg_ref, hgg_ref,
                  sr0_ref, sh0_ref, n2g_ref, wr_ref, br_ref,
                  x1_ref, h2_ref, cls_ref, sro_ref, sho_ref,
                  sr_scr, sht_scr, proj_scr, o_scr, *, C, min_s, with_cls, has_next):
    t = pl.program_id(1)
    nt = pl.num_programs(1)
    TT = x_ref.shape[1]
    n_chunks = TT // C

    mod = mod_ref[0]
    gt1 = mod[:, 2 * D_MODEL:3 * D_MODEL]

    PR = xnext_ref.shape[1]
    per_block = PR // C

    def project(xc, m, r0):
        xn = xc * lax.rsqrt(jnp.mean(xc * xc, axis=-1, keepdims=True) + EPS) * n1g_ref[...]
        hmod = xn * (1.0 + m[:, D_MODEL:2 * D_MODEL]) + m[:, 0:D_MODEL]
        proj_scr[r0:r0 + PR, :] = _dot(hmod.astype(BF16), win_ref[...])

    @pl.when(t == 0)
    def _():
        for h in range(HEADS):
            sr_scr[h] = sr0_ref[0, h]
            sht_scr[h] = sh0_ref[0, h].T

    first = (t == 0) & (pl.program_id(0) == 0) if has_next else t == 0

    @pl.when(first)
    def _():
        project(x_ref[0, 0:PR, :], mod, 0)

    chunk = functools.partial(
        _mixer_chunk, proj_scr=proj_scr, cos_ref=cos_ref, sin_ref=sin_ref, rd_ref=rd_ref,
        idec_ref=idec_ref, kdec_ref=kdec_ref, lb_ref=lb_ref, lvl_ref=lvl_ref, tri_ref=tri_ref,
        retg_ref=retg_ref, hgg_ref=hgg_ref, sr_scr=sr_scr, sht_scr=sht_scr, o_scr=o_scr,
        C=C, min_s=min_s)
    for c in range(n_chunks):
        chunk(c * C)
        if c % per_block == 0:
            r1 = (c // per_block + 1) * PR
            if r1 < TT:
                project(x_ref[0, r1:r1 + PR, :], mod, r1)
            elif has_next:
                project(xnext_ref[0], modnext_ref[0], 0)

    x = x_ref[0]
    mix = _dot(o_scr[...], wout_ref[...])
    x1 = x + gt1 * mix
    x1_ref[0] = x1

    sh2 = mod[:, 3 * D_MODEL:4 * D_MODEL]
    sc2 = mod[:, 4 * D_MODEL:5 * D_MODEL]
    x1n = x1 * lax.rsqrt(jnp.mean(x1 * x1, axis=-1, keepdims=True) + EPS) * n2g_ref[...]
    h2 = x1n * (1.0 + sc2) + sh2
    h2_ref[0] = h2
    if with_cls:
        g_idx, i1, i2 = _route_topk(_router_logits(h2, wr_ref, br_ref).T)[:3]
        lo = jnp.minimum(i1, i2)
        hi = jnp.maximum(i1, i2)
        pair = jnp.where(lo == 0.0, hi - 1.0, jnp.where(lo == 1.0, 6.0 - hi, 5.0))
        cls_ref[0] = (g_idx * float(N_PAIRS) + pair).astype(jnp.int32)
    else:
        cls_ref[0] = jnp.zeros((1, TT), jnp.int32)

    @pl.when(t == nt - 1)
    def _():
        for h in range(HEADS):
            sro_ref[0, h] = sr_scr[h]
            sho_ref[0, h] = sht_scr[h].T


def _mixer(x, mod, n1g, win, wout, tabs, retg, hgg, sr0, sh0, n2g, wr, br, *, C, TT, min_s,
           with_cls):
    B, T, D = x.shape
    cosf, sinf, rd, idec, kdec, lb, lvl, tri = tabs
    nt = T // TT
    const2 = lambda b, t: (0, 0)
    const3 = lambda b, t: (0, 0, 0)
    state_spec = pl.BlockSpec((1, HEADS, DH, DH), lambda b, t: (b, 0, 0, 0))
    PR = min(TT, MIXER_PROJ_ROWS)
    per_tile = TT // PR
    n_blocks = T // PR

    def next_seq(b, t):
        return jnp.minimum(b + ((t + 1) * per_tile) // n_blocks, B - 1)

    in_specs = [
        pl.BlockSpec((1, TT, D), lambda b, t: (b, t, 0)),
        pl.BlockSpec((1, PR, D),
                     lambda b, t: (next_seq(b, t), ((t + 1) * per_tile) % n_blocks, 0)),
        pl.BlockSpec((1, 1, 6 * D), lambda b, t: (b, 0, 0)),
        pl.BlockSpec((1, 1, 6 * D), lambda b, t: (next_seq(b, t), 0, 0)),
        pl.BlockSpec((1, D), const2),
        pl.BlockSpec((D, D_IN), const2),
        pl.BlockSpec((D, D), const2),
        pl.BlockSpec((TT, DH), lambda b, t: (t, 0)),
        pl.BlockSpec((TT, DH), lambda b, t: (t, 0)),
        pl.BlockSpec((HEADS, C, C), const3),
        pl.BlockSpec((HEADS, C, DH), const3),
        pl.BlockSpec((HEADS, C, DH), const3),
        pl.BlockSpec((1, HEADS * DH), const2),
        pl.BlockSpec((C, C), const2),
        pl.BlockSpec((C, C), const2),
        pl.BlockSpec((1, HEADS * DH), const2),
        pl.BlockSpec((1, HEADS * DH), const2),
        state_spec, state_spec,
        pl.BlockSpec((1, D), const2),
        pl.BlockSpec((D, 2 * ROUTE_LANES), const2),
        pl.BlockSpec((1, ROUTE_LANES), const2),
    ]
    tok_spec = pl.BlockSpec((1, TT, D), lambda b, t: (b, t, 0))
    out_specs = [tok_spec, tok_spec,
                 pl.BlockSpec((1, 1, TT), lambda b, t: (b * nt + t, 0, 0)),
                 state_spec, state_spec]
    out_shape = [jax.ShapeDtypeStruct((B, T, D), F32),
                 jax.ShapeDtypeStruct((B, T, D), F32),
                 jax.ShapeDtypeStruct((B * nt, 1, TT), jnp.int32),
                 jax.ShapeDtypeStruct((B, HEADS, DH, DH), F32),
                 jax.ShapeDtypeStruct((B, HEADS, DH, DH), F32)]
    return pl.pallas_call(
        functools.partial(_mixer_kernel, C=C, min_s=min_s, with_cls=with_cls, has_next=nt > 1),
        grid=(B, nt),
        in_specs=in_specs,
        out_specs=out_specs,
        out_shape=out_shape,
        scratch_shapes=[pltpu.VMEM((HEADS, DH, DH), F32),
                        pltpu.VMEM((HEADS, DH, DH), F32),
                        pltpu.VMEM((TT, D_IN), F32),
                        pltpu.VMEM((TT, D), BF16)],
        compiler_params=pltpu.CompilerParams(
            dimension_semantics=("arbitrary", "arbitrary"),
            vmem_limit_bytes=V7X_VMEM_LIMIT),
        name="mixer",
    )(x, x, mod, mod, n1g, win, wout, cosf, sinf, rd, idec, kdec, lb, lvl, tri, retg, hgg, sr0, sh0,
      n2g, wr, br)


def _expert_w(w_ref):
    return w_ref[0].astype(BF16)


def _router_logits(h, wr_ref, br_ref):
    h1, h2, _ = _split3(h)
    hi = _dot(h1, wr_ref[...])
    lo = _dot(h2, wr_ref[:, 0:ROUTE_LANES])
    return hi[:, 0:ROUTE_LANES] + (hi[:, ROUTE_LANES:] + lo) + br_ref[...]


def _route_topk(logits_t):
    TM = logits_t.shape[1]
    lg = logits_t[0:N_GROUPS, :]
    gi = lax.broadcasted_iota(jnp.int32, (N_GROUPS, TM), 0).astype(F32)
    m = jnp.max(lg, axis=0, keepdims=True)
    g_w = 1.0 / jnp.sum(jnp.exp(lg - m), axis=0, keepdims=True)
    g_idx = jnp.min(jnp.where(lg == m, gi, float(N_GROUPS)), axis=0, keepdims=True)
    el = jnp.zeros((EXP_PER_GROUP, TM), F32)
    for g in range(N_GROUPS):
        lo = N_GROUPS + g * EXP_PER_GROUP
        el = jnp.where(g_idx == float(g), logits_t[lo:lo + EXP_PER_GROUP, :], el)
    ei = lax.broadcasted_iota(jnp.int32, (EXP_PER_GROUP, TM), 0).astype(F32)
    m1 = jnp.max(el, axis=0, keepdims=True)
    i1 = jnp.min(jnp.where(el == m1, ei, float(EXP_PER_GROUP)), axis=0, keepdims=True)
    el2 = jnp.where(ei == i1, -jnp.inf, el)
    m2 = jnp.max(el2, axis=0, keepdims=True)
    i2 = jnp.min(jnp.where(el2 == m2, ei, float(EXP_PER_GROUP)), axis=0, keepdims=True)
    p2 = jnp.exp(m2 - m1)
    den = 1.0 / (1.0 + p2)
    w0 = g_w * den
    w1 = g_w * (p2 * den)
    return g_idx, i1, i2, w0, w1


def _moe_kernel(x1_ref, h2_ref, mod_ref, wr_ref, br_ref, wg_ref, wu_ref, wd_ref, fg_ref,
                y_ref, h_scr, route_scr, acc_scr):
    e = pl.program_id(1)
    ne = pl.num_programs(1)
    TM, D = x1_ref.shape
    K = mod_ref.shape[0]
    TQ = TM // K

    @pl.when(e == 0)
    def _():
        h = h2_ref[...]
        h_scr[...] = h.astype(BF16)
        g_idx, i1, i2, w0, wgt1 = _route_topk(_router_logits(h, wr_ref, br_ref).T)
        e0 = g_idx * EXP_PER_GROUP + i1
        e1 = g_idx * EXP_PER_GROUP + i2
        ri = lax.broadcasted_iota(jnp.int32, (ROUTE_LANES, TM), 0)
        rows = jnp.where(ri == 0, e0, jnp.where(ri == 1, e1, jnp.where(ri == 2, w0,
                         jnp.where(ri == 3, wgt1, 0.0))))
        route_scr[...] = rows.T
        acc_scr[...] = jnp.zeros_like(acc_scr)

    hb = h_scr[...]
    hid = _silu(_dot(hb, _expert_w(wg_ref))) * _dot(hb, _expert_w(wu_ref))
    out = _dot(hid.astype(BF16), _expert_w(wd_ref))
    ef = e.astype(F32)
    route = route_scr[...]
    comb = (jnp.where(route[:, 0:1] == ef, route[:, 2:3], 0.0)
            + jnp.where(route[:, 1:2] == ef, route[:, 3:4], 0.0))
    acc_scr[...] += comb * out

    @pl.when(e == ne - 1)
    def _():
        gt2 = mod_ref[...][:, :, 5 * D:6 * D]
        ff = (gt2 * acc_scr[...].reshape(K, TQ, D)).reshape(TM, D)
        x2 = x1_ref[...] + ff
        y_ref[...] = x2 * lax.rsqrt(jnp.mean(x2 * x2, axis=-1, keepdims=True) + EPS) * fg_ref[...]


def _moe(x1, h2, mod, wr, br, wg, wu, wd, fg, *, T, TM):
    N, D = x1.shape
    if T >= TM:
        assert T % TM == 0
        seqs, tiles_per_seq = 1, T // TM
        mod_map = lambda i, e: (i // tiles_per_seq, 0, 0)
    else:
        assert TM % T == 0
        seqs = TM // T
        mod_map = lambda i, e: (i, 0, 0)
    return pl.pallas_call(
        _moe_kernel,
        grid=(N // TM, N_EXPERTS),
        in_specs=[
            pl.BlockSpec((TM, D), lambda i, e: (i, 0)),
            pl.BlockSpec((TM, D), lambda i, e: (i, 0)),
            pl.BlockSpec((seqs, 1, 6 * D), mod_map),
            pl.BlockSpec((D, 2 * ROUTE_LANES), lambda i, e: (0, 0)),
            pl.BlockSpec((1, ROUTE_LANES), lambda i, e: (0, 0)),
            pl.BlockSpec((1, D, D_EXPERT), lambda i, e: (e, 0, 0)),
            pl.BlockSpec((1, D, D_EXPERT), lambda i, e: (e, 0, 0)),
            pl.BlockSpec((1, D_EXPERT, D), lambda i, e: (e, 0, 0)),
            pl.BlockSpec((1, D), lambda i, e: (0, 0)),
        ],
        out_specs=pl.BlockSpec((TM, D), lambda i, e: (i, 0)),
        out_shape=jax.ShapeDtypeStruct((N, D), F32),
        scratch_shapes=[pltpu.VMEM((TM, D), BF16),
                        pltpu.VMEM((TM, ROUTE_LANES), F32),
                        pltpu.VMEM((TM, D), F32)],
        compiler_params=pltpu.CompilerParams(
            dimension_semantics=("arbitrary", "arbitrary"),
            vmem_limit_bytes=V7X_VMEM_LIMIT),
        name="moe",
    )(x1, h2, mod, wr, br, wg, wu, wd, fg)


INFO_TAIL_LEN = 32
INFO_END = 64


def _rank_kernel(cls_ref, pos_ref, tcls_ref, info_ref, *, TG):
    R, L = cls_ref.shape
    info_lane = lax.broadcasted_iota(jnp.int32, info_ref.shape, 1)
    info = jnp.zeros(info_ref.shape, F32)
    cf = cls_ref[...].astype(F32)
    ri = lax.broadcasted_iota(jnp.int32, (L, L), 0)
    ci = lax.broadcasted_iota(jnp.int32, (L, L), 1)
    upper = (ri < ci).astype(BF16)
    rr = lax.broadcasted_iota(jnp.int32, (R, R), 0)
    rc = lax.broadcasted_iota(jnp.int32, (R, R), 1)
    lower = (rc < rr).astype(BF16)
    tile_start = lax.broadcasted_iota(jnp.int32, tcls_ref.shape, 1).astype(F32) * float(TG)
    base = jnp.zeros((1, 1), F32)
    pos = jnp.zeros((R, L), F32)
    tcls = jnp.zeros(tcls_ref.shape, F32)
    for c in range(N_CLASSES):
        ind = jnp.where(cf == float(c), 1.0, 0.0)
        lane_pre = _dot(ind.astype(BF16), upper)
        row_tot = jnp.broadcast_to(jnp.sum(ind, axis=1, keepdims=True), (R, L))
        row_pre = _dot(lower, row_tot.astype(BF16))
        total = jnp.sum(row_tot[:, 0:1], axis=0, keepdims=True)
        pos = jnp.where(ind > 0.0, base + row_pre + lane_pre, pos)
        padded = jnp.floor((total + float(TG - 1)) * (1.0 / TG)) * float(TG)
        info = jnp.where(info_lane == c, base + total, info)
        info = jnp.where(info_lane == INFO_TAIL_LEN + c, padded - total, info)
        base = base + padded
        tcls = tcls + jnp.where(tile_start >= base, 1.0, 0.0)
    info = jnp.where(info_lane == INFO_END, base, info)
    pos_ref[...] = pos.astype(jnp.int32)
    tcls_ref[...] = tcls.astype(jnp.int32)
    info_ref[...] = info.astype(jnp.int32)


def _rank(cls2d, *, TG, n_tiles):
    R, L = cls2d.shape
    lanes = -(-n_tiles // 128) * 128
    return pl.pallas_call(
        functools.partial(_rank_kernel, TG=TG),
        out_shape=(jax.ShapeDtypeStruct((R, L), jnp.int32),
                   jax.ShapeDtypeStruct((1, lanes), jnp.int32),
                   jax.ShapeDtypeStruct((1, 128), jnp.int32)),
        name="rank",
    )(cls2d)


SUBLANES = 8


def _for_each_row(n_rows, fn):
    def group(g, carry):
        r0 = pl.multiple_of(g * SUBLANES, SUBLANES)
        for j in range(SUBLANES):
            fn(pl.ds(r0, SUBLANES), j, r0 + j)
        return carry
    lax.fori_loop(0, n_rows // SUBLANES, group, 0)


def _dispatch_kernel(info_ref, pos_ref, h2_ref, xs_hbm, zero_scr, sem, zsem, *, TD):
    ZR = zero_scr.shape[0]

    def tail_copies(do):
        for c in range(N_CLASSES):
            start = info_ref[c]
            head = (-start) & (SUBLANES - 1)
            for i in range(SUBLANES - 1):
                @pl.when(i < head)
                def _(i=i):
                    do(pltpu.make_async_copy(zero_scr.at[pl.ds(0, 1)],
                                             xs_hbm.at[pl.ds(start + i, 1)], zsem))
            rest = info_ref[INFO_TAIL_LEN + c] - head
            off = start + head
            k = ZR
            while k >= SUBLANES:
                @pl.when((rest & k) != 0)
                def _(k=k, off=off):
                    do(pltpu.make_async_copy(
                        zero_scr.at[pl.ds(0, k)],
                        xs_hbm.at[pl.ds(pl.multiple_of(off, SUBLANES), k)], zsem))
                off = off + (rest & k)
                k //= 2

    def unused_copies(do):
        first = info_ref[INFO_END] // ZR

        def body(i, carry):
            do(pltpu.make_async_copy(
                zero_scr, xs_hbm.at[pl.ds(pl.multiple_of(i * ZR, ZR), ZR)], zsem))
            return carry
        lax.fori_loop(first, xs_hbm.shape[0] // ZR, body, 0)

    @pl.when(pl.program_id(0) == 0)
    def _():
        zero_scr[...] = jnp.zeros_like(zero_scr)
        tail_copies(lambda cp: cp.start())
        unused_copies(lambda cp: cp.start())
        tail_copies(lambda cp: cp.wait())
        unused_copies(lambda cp: cp.wait())

    def issue(tile, j, r):
        pltpu.make_async_copy(h2_ref.at[tile].at[pl.ds(j, 1)],
                              xs_hbm.at[pl.ds(pos_ref[r], 1)], sem).start()

    _for_each_row(TD, issue)
    pltpu.make_async_copy(h2_ref, xs_hbm.at[pl.ds(0, TD)], sem).wait()


def _dispatch(info, pos, h2, *, n_rows, TG, TD):
    N, D = h2.shape
    grid_spec = pltpu.PrefetchScalarGridSpec(
        num_scalar_prefetch=1,
        grid=(N // TD,),
        in_specs=[pl.BlockSpec((TD,), lambda i, info: (i,), memory_space=pltpu.SMEM),
                  pl.BlockSpec((TD, D), lambda i, info: (i, 0))],
        out_specs=pl.BlockSpec(memory_space=pl.ANY),
        scratch_shapes=[pltpu.VMEM((TG // 2, D), F32),
                        pltpu.SemaphoreType.DMA(()), pltpu.SemaphoreType.DMA(())],
    )
    return pl.pallas_call(
        functools.partial(_dispatch_kernel, TD=TD),
        grid_spec=grid_spec,
        out_shape=jax.ShapeDtypeStruct((n_rows, D), F32),
        compiler_params=pltpu.CompilerParams(
            dimension_semantics=("arbitrary",), vmem_limit_bytes=V7X_VMEM_LIMIT),
        name="dispatch",
    )(info, pos, h2)


def _expert_kernel(ea_ref, eb_ref, nv_ref, xs_ref, wr_ref, br_ref, wga_ref, wua_ref, wda_ref,
                   wgb_ref, wub_ref, wdb_ref, ys_ref):
    j = pl.program_id(0)
    TG = xs_ref.shape[0]
    n_valid = nv_ref[0]

    @pl.when(j < n_valid)
    def _():
        a = ea_ref[j]
        b = eb_ref[j]
        g = lax.div(a, jnp.int32(EXP_PER_GROUP))
        hb = xs_ref[...].astype(BF16)
        logits = _dot(hb, wr_ref[:, 0:ROUTE_LANES]) + br_ref[...]
        lane = lax.broadcasted_iota(jnp.int32, (TG, ROUTE_LANES), 1)
        isg = lane < N_GROUPS
        m = jnp.max(jnp.where(isg, logits, -jnp.inf), axis=1, keepdims=True)
        ssum = jnp.sum(jnp.where(isg, jnp.exp(logits - m), 0.0), axis=1, keepdims=True)
        lg = jnp.sum(jnp.where(lane == g, logits, 0.0), axis=1, keepdims=True)
        la = jnp.sum(jnp.where(lane == N_GROUPS + a, logits, 0.0), axis=1, keepdims=True)
        lb = jnp.sum(jnp.where(lane == N_GROUPS + b, logits, 0.0), axis=1, keepdims=True)
        g_w = jnp.exp(lg - m) / ssum
        mm = jnp.maximum(la, lb)
        pa = jnp.exp(la - mm)
        pb = jnp.exp(lb - mm)
        den = g_w / (pa + pb)
        hid_a = _silu(_dot(hb, _expert_w(wga_ref))) * _dot(hb, _expert_w(wua_ref))
        out = (pa * den) * _dot(hid_a.astype(BF16), _expert_w(wda_ref))
        hid_b = _silu(_dot(hb, _expert_w(wgb_ref))) * _dot(hb, _expert_w(wub_ref))
        ys_ref[...] = out + (pb * den) * _dot(hid_b.astype(BF16), _expert_w(wdb_ref))

    @pl.when(j >= n_valid)
    def _():
        ys_ref[...] = jnp.zeros_like(ys_ref)


def _experts(ea, eb, n_valid, xs, wr, br, wg, wu, wd, *, TG):
    NS, D = xs.shape
    const2 = lambda j, ea, eb, nv: (0, 0)
    wmap_a = lambda j, ea, eb, nv: (ea[j], 0, 0)
    wmap_b = lambda j, ea, eb, nv: (eb[j], 0, 0)
    grid_spec = pltpu.PrefetchScalarGridSpec(
        num_scalar_prefetch=3,
        grid=(NS // TG,),
        in_specs=[
            pl.BlockSpec((TG, D), lambda j, ea, eb, nv: (jnp.minimum(j, nv[0] - 1), 0)),
            pl.BlockSpec((D, 2 * ROUTE_LANES), const2),
            pl.BlockSpec((1, ROUTE_LANES), const2),
            pl.BlockSpec((1, D, D_EXPERT), wmap_a),
            pl.BlockSpec((1, D, D_EXPERT), wmap_a),
            pl.BlockSpec((1, D_EXPERT, D), wmap_a),
            pl.BlockSpec((1, D, D_EXPERT), wmap_b),
            pl.BlockSpec((1, D, D_EXPERT), wmap_b),
            pl.BlockSpec((1, D_EXPERT, D), wmap_b),
        ],
        out_specs=pl.BlockSpec((TG, D), lambda j, ea, eb, nv: (j, 0)),
    )
    return pl.pallas_call(
        _expert_kernel,
        grid_spec=grid_spec,
        out_shape=jax.ShapeDtypeStruct((NS, D), F32),
        compiler_params=pltpu.CompilerParams(
            dimension_semantics=("arbitrary",), vmem_limit_bytes=V7X_VMEM_LIMIT),
        name="experts",
    )(ea, eb, n_valid, xs, wr, br, wg, wu, wd, wg, wu, wd)


def _combine_kernel(pos_ref, posn_ref, x1_ref, mod_ref, fg_ref, ys_hbm, y_ref, buf, sems, *, TC):
    i = pl.program_id(0)
    n = pl.num_programs(0)
    slot = i % 2

    def gather(p_ref, s):
        def issue(tile, j, r):
            pltpu.make_async_copy(ys_hbm.at[pl.ds(p_ref[r], 1)],
                                  buf.at[s].at[tile].at[pl.ds(j, 1)], sems.at[s]).start()
        _for_each_row(TC, issue)

    @pl.when(i == 0)
    def _():
        gather(pos_ref, 0)

    @pl.when(i + 1 < n)
    def _():
        gather(posn_ref, 1 - slot)

    pltpu.make_async_copy(ys_hbm.at[pl.ds(0, TC)], buf.at[slot], sems.at[slot]).wait()
    gt2 = mod_ref[0][:, 5 * D_MODEL:6 * D_MODEL]
    x2 = x1_ref[...] + gt2 * buf[slot]
    y_ref[...] = x2 * lax.rsqrt(jnp.mean(x2 * x2, axis=-1, keepdims=True) + EPS) * fg_ref[...]


def _combine(pos, x1, mod, fg, ys, *, T, TC):
    N, D = x1.shape
    n = N // TC
    tiles_per_seq = T // TC
    return pl.pallas_call(
        functools.partial(_combine_kernel, TC=TC),
        grid=(n,),
        in_specs=[pl.BlockSpec((TC,), lambda i: (i,), memory_space=pltpu.SMEM),
                  pl.BlockSpec((TC,), lambda i: (jnp.minimum(i + 1, n - 1),),
                               memory_space=pltpu.SMEM),
                  pl.BlockSpec((TC, D), lambda i: (i, 0)),
                  pl.BlockSpec((1, 1, 6 * D), lambda i: (i // tiles_per_seq, 0, 0)),
                  pl.BlockSpec((1, D), lambda i: (0, 0)),
                  pl.BlockSpec(memory_space=pl.ANY)],
        out_specs=pl.BlockSpec((TC, D), lambda i: (i, 0)),
        out_shape=jax.ShapeDtypeStruct((N, D), F32),
        scratch_shapes=[pltpu.VMEM((2, TC, D), F32), pltpu.SemaphoreType.DMA((2,))],
        compiler_params=pltpu.CompilerParams(
            dimension_semantics=("arbitrary",), vmem_limit_bytes=V7X_VMEM_LIMIT),
        name="combine",
    )(pos, pos, x1, mod, fg, ys)


def _routed_moe(x1, h2, cls, mod, p, *, T):
    N, D = x1.shape
    TG = ROUTED_TILE
    n_tiles = N // TG + N_CLASSES
    pos2d, tcls, info = _rank(cls.reshape(N // 128, 128), TG=TG, n_tiles=n_tiles)
    pos = pos2d.reshape(N)
    tcls = tcls[0, :n_tiles]
    ea = jnp.asarray(CLASS_EXPERT_A)[tcls]
    eb = jnp.asarray(CLASS_EXPERT_B)[tcls]
    n_valid = (info[0, INFO_END:INFO_END + 1] // TG).astype(jnp.int32)
    xs = _dispatch(info.reshape(-1), pos, h2, n_rows=n_tiles * TG, TG=TG, TD=DISPATCH_TILE)
    ys = _experts(ea, eb, n_valid, xs, p["wr"], p["br"], p["wg"], p["wu"], p["wd"], TG=TG)
    return _combine(pos, x1, mod, p["fg"], ys, T=T, TC=COMBINE_TILE)


def _tiling(T):
    if T <= REF_CHUNK:
        return T, T
    return MIXER_CHUNK, min(T, MIXER_TILE)


def _trunk(x, mod, s_ret, s_hg, pos0, inv2, p, *, min_s=HGRN_MIN_BLOCK):
    B, T, D = x.shape
    N = B * T
    C, TT = _tiling(T)
    min_s = min(min_s, C // 2)
    routed = N >= ROUTED_MIN_TOKENS
    tabs = _tables(inv2, p["lb_logits"], T=T, C=C, pos0=pos0, min_s=min_s)
    mod3 = mod.reshape(B, 1, 6 * D)
    x1, h2, cls, sr, sh = _mixer(x, mod3, p["n1g"], p["win"], p["wout"], tabs, p["retg"], p["hgg"],
                                 s_ret, s_hg, p["n2g"], p["wr"], p["br"], C=C, TT=TT,
                                 min_s=min_s, with_cls=routed)
    x1 = x1.reshape(N, D)
    h2 = h2.reshape(N, D)
    if routed:
        y = _routed_moe(x1, h2, cls, mod3, p, T=T)
    else:
        y = _moe(x1, h2, mod3, p["wr"], p["br"], p["wg"], p["wu"], p["wd"], p["fg"],
                 T=T, TM=min(N, 1024))
    return y.reshape(B, T, D), sr[None], sh[None]


def _prepare(W_ada, b_ada, norm1_g, norm2_g, W_in, ret_norm_g, hgrn_norm_g, hgrn_lb_logits, W_out,
             W_router_group, b_router_group, W_router_expert, b_router_expert, W_gate_e, W_up_e,
             W_down_e, final_norm_g):
    assert W_in.shape[0] == 1, "single-layer stack only"
    D = D_MODEL
    wr = jnp.concatenate([W_router_group[0], W_router_expert[0]], axis=1)
    wr = jnp.pad(wr, ((0, 0), (0, ROUTE_LANES - wr.shape[1])))
    wr1 = wr.astype(BF16)
    wr2 = (wr - wr1.astype(F32)).astype(BF16)
    br = jnp.concatenate([b_router_group[0], b_router_expert[0]])
    br = jnp.pad(br, (0, ROUTE_LANES - br.shape[0])).reshape(1, ROUTE_LANES)
    return dict(
        n1g=norm1_g[0].reshape(1, D), n2g=norm2_g[0].reshape(1, D), fg=final_norm_g.reshape(1, D),
        win=W_in[0].astype(BF16), wout=W_out[0].astype(BF16),
        retg=ret_norm_g[0].reshape(1, -1), hgg=hgrn_norm_g[0].reshape(1, -1),
        lb_logits=hgrn_lb_logits,
        wr=jnp.concatenate([wr1, wr2], axis=1), br=br,
        wg=W_gate_e[0], wu=W_up_e[0], wd=W_down_e[0],
    )


def _rope_inv():
    half = DH // 2
    inv = 1.0 / (ROPE_BASE ** (jnp.arange(half, dtype=F32) / half))
    return jnp.concatenate([inv, inv]).reshape(1, DH)


def kernel(x_prompt, x_sample, c_prompt, c_sample, state_ret, state_hgrn, W_ada, b_ada, norm1_g,
           norm2_g, W_in, ret_norm_g, hgrn_norm_g, hgrn_lb_logits, W_out, W_router_group,
           b_router_group, W_router_expert, b_router_expert, W_gate_e, W_up_e, W_down_e,
           final_norm_g):
    p = _prepare(W_ada, b_ada, norm1_g, norm2_g, W_in, ret_norm_g, hgrn_norm_g, hgrn_lb_logits,
                 W_out, W_router_group, b_router_group, W_router_expert, b_router_expert,
                 W_gate_e, W_up_e, W_down_e, final_norm_g)
    Bp = x_prompt.shape[0]
    Bs = x_sample.shape[0]
    inv2 = _rope_inv()
    mod = _ada(jnp.concatenate([c_prompt, c_sample], axis=0), W_ada[0], b_ada[0].reshape(1, -1))
    zeros = jnp.zeros((Bp, HEADS, DH, DH), F32)
    y_p, sr_p, sh_p = _trunk(x_prompt, mod[:Bp], zeros, zeros, 0, inv2, p)
    y_s, sr_s, sh_s = _trunk(x_sample, mod[Bp:Bp + Bs], state_ret[0], state_hgrn[0], PAST_LEN,
                             inv2, p)
    return (y_p, y_s, sr_p, sh_p, sr_s, sh_s)
```

```python
import functools

import numpy as np
import jax
import jax.numpy as jnp
from jax import lax
from jax.experimental import pallas as pl
from jax.experimental.pallas import tpu as pltpu

F32 = jnp.float32
BF16 = jnp.bfloat16

D_MODEL = 1024
HEADS = 4
DH = 128
D_IN = 8 * HEADS * DH
N_GROUPS = 4
EXP_PER_GROUP = 4
N_EXPERTS = N_GROUPS * EXP_PER_GROUP
D_EXPERT = 512
ROPE_BASE = 10000.0
EPS = 1e-6
REF_CHUNK = 64
PAST_LEN = 4096

V7X_VMEM_LIMIT = 56 * 1024 * 1024
ROUTE_LANES = 128

_PAIRS = [(0, 1), (0, 2), (0, 3), (1, 3), (1, 2), (3, 2)]
N_PAIRS = len(_PAIRS)
N_CLASSES = N_GROUPS * N_PAIRS
CLASS_EXPERT_A = np.array([g * EXP_PER_GROUP + i for g in range(N_GROUPS) for i, _ in _PAIRS],
                          np.int32)
CLASS_EXPERT_A = np.append(CLASS_EXPERT_A, CLASS_EXPERT_A[-1])
CLASS_EXPERT_B = np.array([g * EXP_PER_GROUP + j for g in range(N_GROUPS) for _, j in _PAIRS],
                          np.int32)
CLASS_EXPERT_B = np.append(CLASS_EXPERT_B, CLASS_EXPERT_B[-1])
MIXER_CHUNK = 128
HGRN_MIN_BLOCK = 4
MIXER_TILE = 512
MIXER_PROJ_ROWS = 256
ROUTED_MIN_TOKENS = 8192
ROUTED_TILE = 512
DISPATCH_TILE = 1024
COMBINE_TILE = 512


def _dot(a, b):
    return jnp.dot(a, b, preferred_element_type=F32)


def _dot_nt(a, b):
    return lax.dot_general(a, b, (((1,), (1,)), ((), ())), preferred_element_type=F32)


def _dot_tn(a, b):
    return lax.dot_general(a, b, (((0,), (0,)), ((), ())), preferred_element_type=F32)


def _lanes(parts):
    return parts[0] if len(parts) == 1 else jnp.concatenate(parts, axis=1)


def _block_diag(parts):
    if len(parts) == 1:
        return parts[0]
    rows = []
    for i, p in enumerate(parts):
        rows.append(_lanes([p if j == i else jnp.zeros((p.shape[0], q.shape[1]), p.dtype)
                            for j, q in enumerate(parts)]))
    return jnp.concatenate(rows, axis=0)


def _split3(a):
    a1 = a.astype(BF16)
    r1 = a - a1.astype(F32)
    a2 = r1.astype(BF16)
    a3 = (r1 - a2.astype(F32)).astype(BF16)
    return a1, a2, a3


def _silu(a):
    return a * (1.0 / (1.0 + jnp.exp(-a)))


def _tables_kernel(inv_ref, lbl_ref, cos_ref, sin_ref, rd_ref, idec_ref, kdec_ref, lb_ref,
                   lvl_ref, tri_ref, *, T, C, pos0, min_s):
    pos = (lax.broadcasted_iota(jnp.int32, (T, DH), 0) + pos0).astype(F32)
    ang = pos * inv_ref[...]
    lane = lax.broadcasted_iota(jnp.int32, (T, DH), 1)
    cos_ref[...] = jnp.cos(ang)
    s = jnp.sin(ang)
    sin_ref[...] = jnp.where(lane < DH // 2, -s, s)

    li = lax.broadcasted_iota(jnp.int32, (C, C), 0)
    mi = lax.broadcasted_iota(jnp.int32, (C, C), 1)
    diff = (li - mi).astype(F32)
    rowj = lax.broadcasted_iota(jnp.int32, (C, DH), 0).astype(F32)
    for h in range(HEADS):
        lg = jnp.log(jnp.full((1, 1), 1.0 - 2.0 ** (-5.0 - h), F32))
        rd_ref[h] = jnp.where(diff >= 0, jnp.exp(jnp.maximum(diff, 0.0) * lg), 0.0)
        idec_ref[h] = jnp.exp((rowj + 1.0) * lg)
        kdec_ref[h] = jnp.exp((C - 1.0 - rowj) * lg)

    lg_all = lbl_ref[...]
    m = jnp.max(lg_all, axis=0, keepdims=True)
    e = jnp.exp(lg_all - m)
    lb_ref[...] = e[0:1, :] / jnp.sum(e, axis=0, keepdims=True)

    x = li ^ mi
    code = jnp.zeros((C, C), jnp.int32)
    sz = min_s
    while sz < C:
        code = code + (x >= sz).astype(jnp.int32)
        sz *= 2
    code = jnp.where(x >= min_s, code + (min_s.bit_length() - 1), 0)
    lvl_ref[...] = jnp.where(li >= mi, code, -1)
    tri_ref[...] = (li >= mi).astype(BF16)


def _tables(inv2, lb_logits, *, T, C, pos0, min_s):
    out_shape = (
        jax.ShapeDtypeStruct((T, DH), F32),
        jax.ShapeDtypeStruct((T, DH), F32),
        jax.ShapeDtypeStruct((HEADS, C, C), F32),
        jax.ShapeDtypeStruct((HEADS, C, DH), F32),
        jax.ShapeDtypeStruct((HEADS, C, DH), F32),
        jax.ShapeDtypeStruct((1, HEADS * DH), F32),
        jax.ShapeDtypeStruct((C, C), jnp.int32),
        jax.ShapeDtypeStruct((C, C), BF16),
    )
    return pl.pallas_call(
        functools.partial(_tables_kernel, T=T, C=C, pos0=pos0, min_s=min_s),
        out_shape=out_shape,
        name="tables",
    )(inv2, lb_logits)


def _ada_kernel(c_ref, w_ref, b_ref, o_ref):
    a = _silu(c_ref[...])
    w = w_ref[...]
    a1, a2, _ = _split3(a)
    w1, w2, _ = _split3(w)
    o_ref[...] = _dot(a1, w1) + (_dot(a1, w2) + _dot(a2, w1)) + b_ref[...]


def _ada(c, w, b):
    R = c.shape[0]
    N = w.shape[1]
    TN = 1024
    return pl.pallas_call(
        _ada_kernel,
        grid=(N // TN,),
        in_specs=[pl.BlockSpec((R, D_MODEL), lambda j: (0, 0)),
                  pl.BlockSpec((D_MODEL, TN), lambda j: (0, j)),
                  pl.BlockSpec((1, TN), lambda j: (0, j))],
        out_specs=pl.BlockSpec((R, TN), lambda j: (0, j)),
        out_shape=jax.ShapeDtypeStruct((R, N), F32),
        name="ada",
    )(c, w, b)


def _block_ref(b, block, idx):
    C, W = b.shape
    if block >= 8:
        b3 = b.reshape(C // block, block, W)
        r = jnp.broadcast_to(b3[:, idx:idx + 1, :], b3.shape)
        return r.reshape(C, W)
    row = lax.broadcasted_iota(jnp.int32, (C, W), 0)
    p = row % block
    out = b
    for q in range(block):
        sh = q - idx
        if sh == 0:
            continue
        rolled = pltpu.roll(b, sh % C, 0)
        out = jnp.where(p == q, rolled, out)
    return out


def _head_norm_gate(o, g_row, gate):
    y = o * lax.rsqrt(jnp.mean(o * o, axis=-1, keepdims=True) + EPS)
    return (y * g_row) * _silu(gate)


def _mixer_chunk(r0, proj_scr, cos_ref, sin_ref, rd_ref, idec_ref, kdec_ref, lb_ref, lvl_ref,
                 tri_ref, retg_ref, hgg_ref, sr_scr, sht_scr, o_scr, *, C, min_s):
    W = HEADS * DH
    rows = pl.ds(r0, C)
    proj = proj_scr[rows, :]
    cosf = cos_ref[rows, :]
    sinf = sin_ref[rows, :]

    G = 2 if C % DH == 0 else 1
    groups = [tuple(range(h, h + G)) for h in range(0, HEADS, G)]

    for hs in groups:
        qb_parts, kr_parts, v_parts = [], [], []
        for h in hs:
            c0 = h * DH
            q = proj[:, c0:c0 + DH]
            k = proj[:, W + c0:W + c0 + DH]
            qb_parts.append((q * cosf + pltpu.roll(q, DH // 2, 1) * sinf).astype(BF16))
            kr_parts.append((k * cosf + pltpu.roll(k, DH // 2, 1) * sinf) * (DH ** -0.5))
            v_parts.append(proj[:, 2 * W + c0:2 * W + c0 + DH].astype(BF16))
        qb = _lanes(qb_parts)
        A = (_dot_nt(qb, _block_diag([kr.astype(BF16) for kr in kr_parts]))
             * _lanes([rd_ref[h] for h in hs]))
        S_parts = [sr_scr[h] for h in hs]
        o = (_dot(A.astype(BF16), _block_diag(v_parts))
             + _dot(qb, _block_diag([S.astype(BF16) for S in S_parts]))
             * _lanes([idec_ref[h] for h in hs]))
        for i, h in enumerate(hs):
            c0 = h * DH
            sr_scr[h] = (idec_ref[h][C - 1:C, :] * S_parts[i]
                         + _dot_tn((kr_parts[i] * kdec_ref[h]).astype(BF16), v_parts[i]))
            g = proj[:, 3 * W + c0:3 * W + c0 + DH]
            o_scr[rows, c0:c0 + DH] = _head_norm_gate(
                o[:, i * DH:(i + 1) * DH], retg_ref[:, c0:c0 + DH], g).astype(BF16)

    z = proj[:, 5 * W:6 * W]
    lb = lb_ref[...]
    e = jnp.exp(-jnp.abs(z))
    r = 1.0 / (1.0 + e)
    er = e * r
    sig = jnp.where(z >= 0, r, er)
    nsig = jnp.where(z >= 0, er, r)
    logf = jnp.log(lb + (1.0 - lb) * sig)
    kk_all = (1.0 - lb) * nsig
    tri = tri_ref[...]
    l1, l2, _ = _split3(logf)
    b_all = _dot(tri, l1) + _dot(tri, l2)

    GW = G * DH
    lvl = _lanes([lvl_ref[...]] * G)
    rowc = lax.broadcasted_iota(jnp.int32, (C, GW), 0)
    levels = []
    s = C // 2
    while s >= min_s:
        isq = (rowc % (2 * s)) >= s
        levels.append((s, isq, jnp.where(isq, 1.0, -1.0)))
        s //= 2

    def heads_of(a):
        return [a[:, i * DH:(i + 1) * DH] for i in range(G)]

    for hs in groups:
        c0 = hs[0] * DH
        q = proj[:, 4 * W + c0:4 * W + c0 + GW] * (DH ** -0.5)
        kk = kk_all[:, c0:c0 + GW]
        b = b_all[:, c0:c0 + GW]
        v_parts = heads_of(proj[:, 6 * W + c0:6 * W + c0 + GW])
        vb_parts = [v.astype(BF16) for v in v_parts]

        A = jnp.zeros((C, G * C), F32)
        for s, isq, sign in levels:
            ref = _block_ref(b, 2 * s, s - 1)
            zz = jnp.where(isq, q, kk) * jnp.exp((b - ref) * sign)
            zb = zz.astype(BF16)
            A = jnp.where(lvl == s.bit_length(), _dot_nt(zb, _block_diag(heads_of(zb))), A)
        if min_s > 1:
            ref = _block_ref(b, min_s, min_s // 2)
            d = b - ref
            P = _dot_nt((q * jnp.exp(d)).astype(BF16),
                        _block_diag(heads_of((kk * jnp.exp(-d)).astype(BF16))))
            A = jnp.where(lvl == 0, P, A)
            o = _dot(A.astype(BF16), _block_diag(vb_parts))
        else:
            diag = [jnp.sum(qh * kh, axis=-1, keepdims=True) * v
                    for qh, kh, v in zip(heads_of(q), heads_of(kk), v_parts)]
            o = _dot(A.astype(BF16), _block_diag(vb_parts)) + _lanes(diag)

        ST_parts = [sht_scr[h] for h in hs]
        bl = b[C - 1:C, :]
        o = o + _dot_nt((q * jnp.exp(b)).astype(BF16),
                        _block_diag([ST.astype(BF16) for ST in ST_parts]))
        kdec_parts = heads_of((kk * jnp.exp(bl - b)).astype(BF16))
        ebl_parts = heads_of(jnp.exp(bl))
        for i, h in enumerate(hs):
            ch = h * DH
            sht_scr[h] = ST_parts[i] * ebl_parts[i] + _dot_tn(vb_parts[i], kdec_parts[i])
            g = proj[:, 7 * W + ch:7 * W + ch + DH]
            o_scr[rows, W + ch:W + ch + DH] = _head_norm_gate(
                o[:, i * DH:(i + 1) * DH], hgg_ref[:, ch:ch + DH], g).astype(BF16)


def _mixer_kernel(x_ref, xnext_ref, mod_ref, modnext_ref, n1g_ref, win_ref, wout_ref, cos_ref,
                  sin_ref, rd_ref, idec_ref, kdec_ref, lb_ref, lvl_ref, tri_ref, retg_ref, hgg_ref,
                  sr0_ref, sh0_ref, n2g_ref, wr_ref, br_ref,
                  x1_ref, h2_ref, cls_ref, sro_ref, sho_ref,
                  sr_scr, sht_scr, proj_scr, o_scr, *, C, min_s, with_cls, has_next):
    t = pl.program_id(1)
    nt = pl.num_programs(1)
    TT = x_ref.shape[1]
    n_chunks = TT // C

    mod = mod_ref[0]
    gt1 = mod[:, 2 * D_MODEL:3 * D_MODEL]

    PR = xnext_ref.shape[1]
    per_block = PR // C

    def project(xc, m, r0):
        xn = xc * lax.rsqrt(jnp.mean(xc * xc, axis=-1, keepdims=True) + EPS) * n1g_ref[...]
        hmod = xn * (1.0 + m[:, D_MODEL:2 * D_MODEL]) + m[:, 0:D_MODEL]
        proj_scr[r0:r0 + PR, :] = _dot(hmod.astype(BF16), win_ref[...])

    @pl.when(t == 0)
    def _():
        for h in range(HEADS):
            sr_scr[h] = sr0_ref[0, h]
            sht_scr[h] = sh0_ref[0, h].T

    first = (t == 0) & (pl.program_id(0) == 0) if has_next else t == 0

    @pl.when(first)
    def _():
        project(x_ref[0, 0:PR, :], mod, 0)

    chunk = functools.partial(
        _mixer_chunk, proj_scr=proj_scr, cos_ref=cos_ref, sin_ref=sin_ref, rd_ref=rd_ref,
        idec_ref=idec_ref, kdec_ref=kdec_ref, lb_ref=lb_ref, lvl_ref=lvl_ref, tri_ref=tri_ref,
        retg_ref=retg_ref, hgg_ref=hgg_ref, sr_scr=sr_scr, sht_scr=sht_scr, o_scr=o_scr,
        C=C, min_s=min_s)
    for c in range(n_chunks):
        chunk(c * C)
        if c % per_block == 0:
            r1 = (c // per_block + 1) * PR
            if r1 < TT:
                project(x_ref[0, r1:r1 + PR, :], mod, r1)
            elif has_next:
                project(xnext_ref[0], modnext_ref[0], 0)

    x = x_ref[0]
    mix = _dot(o_scr[...], wout_ref[...])
    x1 = x + gt1 * mix
    x1_ref[0] = x1

    sh2 = mod[:, 3 * D_MODEL:4 * D_MODEL]
    sc2 = mod[:, 4 * D_MODEL:5 * D_MODEL]
    x1n = x1 * lax.rsqrt(jnp.mean(x1 * x1, axis=-1, keepdims=True) + EPS) * n2g_ref[...]
    h2 = x1n * (1.0 + sc2) + sh2
    h2_ref[0] = h2
    if with_cls:
        g_idx, i1, i2 = _route_topk(_router_logits(h2, wr_ref, br_ref).T)[:3]
        lo = jnp.minimum(i1, i2)
        hi = jnp.maximum(i1, i2)
        pair = jnp.where(lo == 0.0, hi - 1.0, jnp.where(lo == 1.0, 6.0 - hi, 5.0))
        cls_ref[0] = (g_idx * float(N_PAIRS) + pair).astype(jnp.int32)
    else:
        cls_ref[0] = jnp.zeros((1, TT), jnp.int32)

    @pl.when(t == nt - 1)
    def _():
        for h in range(HEADS):
            sro_ref[0, h] = sr_scr[h]
            sho_ref[0, h] = sht_scr[h].T


def _mixer(x, mod, n1g, win, wout, tabs, retg, hgg, sr0, sh0, n2g, wr, br, *, C, TT, min_s,
           with_cls):
    B, T, D = x.shape
    cosf, sinf, rd, idec, kdec, lb, lvl, tri = tabs
    nt = T // TT
    const2 = lambda b, t: (0, 0)
    const3 = lambda b, t: (0, 0, 0)
    state_spec = pl.BlockSpec((1, HEADS, DH, DH), lambda b, t: (b, 0, 0, 0))
    PR = min(TT, MIXER_PROJ_ROWS)
    per_tile = TT // PR
    n_blocks = T // PR

    def next_seq(b, t):
        return jnp.minimum(b + ((t + 1) * per_tile) // n_blocks, B - 1)

    in_specs = [
        pl.BlockSpec((1, TT, D), lambda b, t: (b, t, 0)),
        pl.BlockSpec((1, PR, D),
                     lambda b, t: (next_seq(b, t), ((t + 1) * per_tile) % n_blocks, 0)),
        pl.BlockSpec((1, 1, 6 * D), lambda b, t: (b, 0, 0)),
        pl.BlockSpec((1, 1, 6 * D), lambda b, t: (next_seq(b, t), 0, 0)),
        pl.BlockSpec((1, D), const2),
        pl.BlockSpec((D, D_IN), const2),
        pl.BlockSpec((D, D), const2),
        pl.BlockSpec((TT, DH), lambda b, t: (t, 0)),
        pl.BlockSpec((TT, DH), lambda b, t: (t, 0)),
        pl.BlockSpec((HEADS, C, C), const3),
        pl.BlockSpec((HEADS, C, DH), const3),
        pl.BlockSpec((HEADS, C, DH), const3),
        pl.BlockSpec((1, HEADS * DH), const2),
        pl.BlockSpec((C, C), const2),
        pl.BlockSpec((C, C), const2),
        pl.BlockSpec((1, HEADS * DH), const2),
        pl.BlockSpec((1, HEADS * DH), const2),
        state_spec, state_spec,
        pl.BlockSpec((1, D), const2),
        pl.BlockSpec((D, 2 * ROUTE_LANES), const2),
        pl.BlockSpec((1, ROUTE_LANES), const2),
    ]
    tok_spec = pl.BlockSpec((1, TT, D), lambda b, t: (b, t, 0))
    out_specs = [tok_spec, tok_spec,
                 pl.BlockSpec((1, 1, TT), lambda b, t: (b * nt + t, 0, 0)),
                 state_spec, state_spec]
    out_shape = [jax.ShapeDtypeStruct((B, T, D), F32),
                 jax.ShapeDtypeStruct((B, T, D), F32),
                 jax.ShapeDtypeStruct((B * nt, 1, TT), jnp.int32),
                 jax.ShapeDtypeStruct((B, HEADS, DH, DH), F32),
                 jax.ShapeDtypeStruct((B, HEADS, DH, DH), F32)]
    return pl.pallas_call(
        functools.partial(_mixer_kernel, C=C, min_s=min_s, with_cls=with_cls, has_next=nt > 1),
        grid=(B, nt),
        in_specs=in_specs,
        out_specs=out_specs,
        out_shape=out_shape,
        scratch_shapes=[pltpu.VMEM((HEADS, DH, DH), F32),
                        pltpu.VMEM((HEADS, DH, DH), F32),
                        pltpu.VMEM((TT, D_IN), F32),
                        pltpu.VMEM((TT, D), BF16)],
        compiler_params=pltpu.CompilerParams(
            dimension_semantics=("arbitrary", "arbitrary"),
            vmem_limit_bytes=V7X_VMEM_LIMIT),
        name="mixer",
    )(x, x, mod, mod, n1g, win, wout, cosf, sinf, rd, idec, kdec, lb, lvl, tri, retg, hgg, sr0, sh0,
      n2g, wr, br)


def _expert_w(w_ref):
    return w_ref[0].astype(BF16)


def _router_logits(h, wr_ref, br_ref):
    h1, h2, _ = _split3(h)
    hi = _dot(h1, wr_ref[...])
    lo = _dot(h2, wr_ref[:, 0:ROUTE_LANES])
    return hi[:, 0:ROUTE_LANES] + (hi[:, ROUTE_LANES:] + lo) + br_ref[...]


def _route_topk(logits_t):
    TM = logits_t.shape[1]
    lg = logits_t[0:N_GROUPS, :]
    gi = lax.broadcasted_iota(jnp.int32, (N_GROUPS, TM), 0).astype(F32)
    m = jnp.max(lg, axis=0, keepdims=True)
    g_w = 1.0 / jnp.sum(jnp.exp(lg - m), axis=0, keepdims=True)
    g_idx = jnp.min(jnp.where(lg == m, gi, float(N_GROUPS)), axis=0, keepdims=True)
    el = jnp.zeros((EXP_PER_GROUP, TM), F32)
    for g in range(N_GROUPS):
        lo = N_GROUPS + g * EXP_PER_GROUP
        el = jnp.where(g_idx == float(g), logits_t[lo:lo + EXP_PER_GROUP, :], el)
    ei = lax.broadcasted_iota(jnp.int32, (EXP_PER_GROUP, TM), 0).astype(F32)
    m1 = jnp.max(el, axis=0, keepdims=True)
    i1 = jnp.min(jnp.where(el == m1, ei, float(EXP_PER_GROUP)), axis=0, keepdims=True)
    el2 = jnp.where(ei == i1, -jnp.inf, el)
    m2 = jnp.max(el2, axis=0, keepdims=True)
    i2 = jnp.min(jnp.where(el2 == m2, ei, float(EXP_PER_GROUP)), axis=0, keepdims=True)
    p2 = jnp.exp(m2 - m1)
    den = 1.0 / (1.0 + p2)
    w0 = g_w * den
    w1 = g_w * (p2 * den)
    return g_idx, i1, i2, w0, w1


def _moe_kernel(x1_ref, h2_ref, mod_ref, wr_ref, br_ref, wg_ref, wu_ref, wd_ref, fg_ref,
                y_ref, h_scr, route_scr, acc_scr):
    e = pl.program_id(1)
    ne = pl.num_programs(1)
    TM, D = x1_ref.shape
    K = mod_ref.shape[0]
    TQ = TM // K

    @pl.when(e == 0)
    def _():
        h = h2_ref[...]
        h_scr[...] = h.astype(BF16)
        g_idx, i1, i2, w0, wgt1 = _route_topk(_router_logits(h, wr_ref, br_ref).T)
        e0 = g_idx * EXP_PER_GROUP + i1
        e1 = g_idx * EXP_PER_GROUP + i2
        ri = lax.broadcasted_iota(jnp.int32, (ROUTE_LANES, TM), 0)
        rows = jnp.where(ri == 0, e0, jnp.where(ri == 1, e1, jnp.where(ri == 2, w0,
                         jnp.where(ri == 3, wgt1, 0.0))))
        route_scr[...] = rows.T
        acc_scr[...] = jnp.zeros_like(acc_scr)

    hb = h_scr[...]
    hid = _silu(_dot(hb, _expert_w(wg_ref))) * _dot(hb, _expert_w(wu_ref))
    out = _dot(hid.astype(BF16), _expert_w(wd_ref))
    ef = e.astype(F32)
    route = route_scr[...]
    comb = (jnp.where(route[:, 0:1] == ef, route[:, 2:3], 0.0)
            + jnp.where(route[:, 1:2] == ef, route[:, 3:4], 0.0))
    acc_scr[...] += comb * out

    @pl.when(e == ne - 1)
    def _():
        gt2 = mod_ref[...][:, :, 5 * D:6 * D]
        ff = (gt2 * acc_scr[...].reshape(K, TQ, D)).reshape(TM, D)
        x2 = x1_ref[...] + ff
        y_ref[...] = x2 * lax.rsqrt(jnp.mean(x2 * x2, axis=-1, keepdims=True) + EPS) * fg_ref[...]


def _moe(x1, h2, mod, wr, br, wg, wu, wd, fg, *, T, TM):
    N, D = x1.shape
    if T >= TM:
        assert T % TM == 0
        seqs, tiles_per_seq = 1, T // TM
        mod_map = lambda i, e: (i // tiles_per_seq, 0, 0)
    else:
        assert TM % T == 0
        seqs = TM // T
        mod_map = lambda i, e: (i, 0, 0)
    return pl.pallas_call(
        _moe_kernel,
        grid=(N // TM, N_EXPERTS),
        in_specs=[
            pl.BlockSpec((TM, D), lambda i, e: (i, 0)),
            pl.BlockSpec((TM, D), lambda i, e: (i, 0)),
            pl.BlockSpec((seqs, 1, 6 * D), mod_map),
            pl.BlockSpec((D, 2 * ROUTE_LANES), lambda i, e: (0, 0)),
            pl.BlockSpec((1, ROUTE_LANES), lambda i, e: (0, 0)),
            pl.BlockSpec((1, D, D_EXPERT), lambda i, e: (e, 0, 0)),
            pl.BlockSpec((1, D, D_EXPERT), lambda i, e: (e, 0, 0)),
            pl.BlockSpec((1, D_EXPERT, D), lambda i, e: (e, 0, 0)),
            pl.BlockSpec((1, D), lambda i, e: (0, 0)),
        ],
        out_specs=pl.BlockSpec((TM, D), lambda i, e: (i, 0)),
        out_shape=jax.ShapeDtypeStruct((N, D), F32),
        scratch_shapes=[pltpu.VMEM((TM, D), BF16),
                        pltpu.VMEM((TM, ROUTE_LANES), F32),
                        pltpu.VMEM((TM, D), F32)],
        compiler_params=pltpu.CompilerParams(
            dimension_semantics=("arbitrary", "arbitrary"),
            vmem_limit_bytes=V7X_VMEM_LIMIT),
        name="moe",
    )(x1, h2, mod, wr, br, wg, wu, wd, fg)


INFO_TAIL_LEN = 32
INFO_END = 64


def _rank_kernel(cls_ref, pos_ref, tcls_ref, info_ref, *, TG):
    R, L = cls_ref.shape
    info_lane = lax.broadcasted_iota(jnp.int32, info_ref.shape, 1)
    info = jnp.zeros(info_ref.shape, F32)
    cf = cls_ref[...].astype(F32)
    ri = lax.broadcasted_iota(jnp.int32, (L, L), 0)
    ci = lax.broadcasted_iota(jnp.int32, (L, L), 1)
    upper = (ri < ci).astype(BF16)
    rr = lax.broadcasted_iota(jnp.int32, (R, R), 0)
    rc = lax.broadcasted_iota(jnp.int32, (R, R), 1)
    lower = (rc < rr).astype(BF16)
    tile_start = lax.broadcasted_iota(jnp.int32, tcls_ref.shape, 1).astype(F32) * float(TG)
    base = jnp.zeros((1, 1), F32)
    pos = jnp.zeros((R, L), F32)
    tcls = jnp.zeros(tcls_ref.shape, F32)
    for c in range(N_CLASSES):
        ind = jnp.where(cf == float(c), 1.0, 0.0)
        lane_pre = _dot(ind.astype(BF16), upper)
        row_tot = jnp.broadcast_to(jnp.sum(ind, axis=1, keepdims=True), (R, L))
        row_pre = _dot(lower, row_tot.astype(BF16))
        total = jnp.sum(row_tot[:, 0:1], axis=0, keepdims=True)
        pos = jnp.where(ind > 0.0, base + row_pre + lane_pre, pos)
        padded = jnp.floor((total + float(TG - 1)) * (1.0 / TG)) * float(TG)
        info = jnp.where(info_lane == c, base + total, info)
        info = jnp.where(info_lane == INFO_TAIL_LEN + c, padded - total, info)
        base = base + padded
        tcls = tcls + jnp.where(tile_start >= base, 1.0, 0.0)
    info = jnp.where(info_lane == INFO_END, base, info)
    pos_ref[...] = pos.astype(jnp.int32)
    tcls_ref[...] = tcls.astype(jnp.int32)
    info_ref[...] = info.astype(jnp.int32)


def _rank(cls2d, *, TG, n_tiles):
    R, L = cls2d.shape
    lanes = -(-n_tiles // 128) * 128
    return pl.pallas_call(
        functools.partial(_rank_kernel, TG=TG),
        out_shape=(jax.ShapeDtypeStruct((R, L), jnp.int32),
                   jax.ShapeDtypeStruct((1, lanes), jnp.int32),
                   jax.ShapeDtypeStruct((1, 128), jnp.int32)),
        name="rank",
    )(cls2d)


SUBLANES = 8


def _for_each_row(n_rows, fn):
    def group(g, carry):
        r0 = pl.multiple_of(g * SUBLANES, SUBLANES)
        for j in range(SUBLANES):
            fn(pl.ds(r0, SUBLANES), j, r0 + j)
        return carry
    lax.fori_loop(0, n_rows // SUBLANES, group, 0)


def _dispatch_kernel(info_ref, pos_ref, h2_hbm, xs_hbm, zero_scr, tiles, zsem, load_sems,
                     row_sems, *, TD):
    ZR = zero_scr.shape[0]

    def tail_copies(do):
        for c in range(N_CLASSES):
            start = info_ref[c]
            head = (-start) & (SUBLANES - 1)
            for i in range(SUBLANES - 1):
                @pl.when(i < head)
                def _(i=i):
                    do(pltpu.make_async_copy(zero_scr.at[pl.ds(0, 1)],
                                             xs_hbm.at[pl.ds(start + i, 1)], zsem))
            rest = info_ref[INFO_TAIL_LEN + c] - head
            off = start + head
            k = ZR
            while k >= SUBLANES:
                @pl.when((rest & k) != 0)
                def _(k=k, off=off):
                    do(pltpu.make_async_copy(
                        zero_scr.at[pl.ds(0, k)],
                        xs_hbm.at[pl.ds(pl.multiple_of(off, SUBLANES), k)], zsem))
                off = off + (rest & k)
                k //= 2

    def unused_copies(do):
        first = info_ref[INFO_END] // ZR

        def body(i, carry):
            do(pltpu.make_async_copy(
                zero_scr, xs_hbm.at[pl.ds(pl.multiple_of(i * ZR, ZR), ZR)], zsem))
            return carry
        lax.fori_loop(first, xs_hbm.shape[0] // ZR, body, 0)

    @pl.when(pl.program_id(0) == 0)
    def _():
        zero_scr[...] = jnp.zeros_like(zero_scr)
        tail_copies(lambda cp: cp.start())
        unused_copies(lambda cp: cp.start())
        tail_copies(lambda cp: cp.wait())
        unused_copies(lambda cp: cp.wait())

    i = pl.program_id(0)
    n = pl.num_programs(0)

    def load(tile, do):
        s = tile % 3
        do(pltpu.make_async_copy(h2_hbm.at[pl.ds(pl.multiple_of(tile * TD, TD), TD)],
                                 tiles.at[s], load_sems.at[s]))

    def rows_wait(tile):
        pltpu.make_async_copy(tiles.at[0], xs_hbm.at[pl.ds(0, TD)], row_sems.at[tile % 2]).wait()

    @pl.when(i == 0)
    def _():
        load(0, lambda cp: cp.start())

    @pl.when(i + 1 < n)
    def _():
        load(i + 1, lambda cp: cp.start())

    load(i, lambda cp: cp.wait())
    slot = i % 3

    def issue(tile, j, r):
        pltpu.make_async_copy(tiles.at[slot].at[tile].at[pl.ds(j, 1)],
                              xs_hbm.at[pl.ds(pos_ref[r], 1)], row_sems.at[i % 2]).start()

    _for_each_row(TD, issue)

    @pl.when(i > 0)
    def _():
        rows_wait(i - 1)

    @pl.when(i == n - 1)
    def _():
        rows_wait(i)


def _dispatch(info, pos, h2, *, n_rows, TG, TD):
    N, D = h2.shape
    grid_spec = pltpu.PrefetchScalarGridSpec(
        num_scalar_prefetch=1,
        grid=(N // TD,),
        in_specs=[pl.BlockSpec((TD,), lambda i, info: (i,), memory_space=pltpu.SMEM),
                  pl.BlockSpec(memory_space=pl.ANY)],
        out_specs=pl.BlockSpec(memory_space=pl.ANY),
        scratch_shapes=[pltpu.VMEM((TG // 2, D), F32), pltpu.VMEM((3, TD, D), F32),
                        pltpu.SemaphoreType.DMA(()), pltpu.SemaphoreType.DMA((3,)),
                        pltpu.SemaphoreType.DMA((2,))],
    )
    return pl.pallas_call(
        functools.partial(_dispatch_kernel, TD=TD),
        grid_spec=grid_spec,
        out_shape=jax.ShapeDtypeStruct((n_rows, D), F32),
        compiler_params=pltpu.CompilerParams(
            dimension_semantics=("arbitrary",), vmem_limit_bytes=V7X_VMEM_LIMIT),
        name="dispatch",
    )(info, pos, h2)


def _expert_kernel(ea_ref, eb_ref, nv_ref, xs_ref, wr_ref, br_ref, wga_ref, wua_ref, wda_ref,
                   wgb_ref, wub_ref, wdb_ref, ys_ref):
    j = pl.program_id(0)
    TG = xs_ref.shape[0]
    n_valid = nv_ref[0]

    @pl.when(j < n_valid)
    def _():
        a = ea_ref[j]
        b = eb_ref[j]
        g = lax.div(a, jnp.int32(EXP_PER_GROUP))
        hb = xs_ref[...].astype(BF16)
        logits = _dot(hb, wr_ref[:, 0:ROUTE_LANES]) + br_ref[...]
        lane = lax.broadcasted_iota(jnp.int32, (TG, ROUTE_LANES), 1)
        isg = lane < N_GROUPS
        m = jnp.max(jnp.where(isg, logits, -jnp.inf), axis=1, keepdims=True)
        ssum = jnp.sum(jnp.where(isg, jnp.exp(logits - m), 0.0), axis=1, keepdims=True)
        lg = jnp.sum(jnp.where(lane == g, logits, 0.0), axis=1, keepdims=True)
        la = jnp.sum(jnp.where(lane == N_GROUPS + a, logits, 0.0), axis=1, keepdims=True)
        lb = jnp.sum(jnp.where(lane == N_GROUPS + b, logits, 0.0), axis=1, keepdims=True)
        g_w = jnp.exp(lg - m) / ssum
        mm = jnp.maximum(la, lb)
        pa = jnp.exp(la - mm)
        pb = jnp.exp(lb - mm)
        den = g_w / (pa + pb)
        hid_a = _silu(_dot(hb, _expert_w(wga_ref))) * _dot(hb, _expert_w(wua_ref))
        out = (pa * den) * _dot(hid_a.astype(BF16), _expert_w(wda_ref))
        hid_b = _silu(_dot(hb, _expert_w(wgb_ref))) * _dot(hb, _expert_w(wub_ref))
        ys_ref[...] = out + (pb * den) * _dot(hid_b.astype(BF16), _expert_w(wdb_ref))

    @pl.when(j >= n_valid)
    def _():
        ys_ref[...] = jnp.zeros_like(ys_ref)


def _experts(ea, eb, n_valid, xs, wr, br, wg, wu, wd, *, TG):
    NS, D = xs.shape
    const2 = lambda j, ea, eb, nv: (0, 0)
    wmap_a = lambda j, ea, eb, nv: (ea[j], 0, 0)
    wmap_b = lambda j, ea, eb, nv: (eb[j], 0, 0)
    grid_spec = pltpu.PrefetchScalarGridSpec(
        num_scalar_prefetch=3,
        grid=(NS // TG,),
        in_specs=[
            pl.BlockSpec((TG, D), lambda j, ea, eb, nv: (jnp.minimum(j, nv[0] - 1), 0)),
            pl.BlockSpec((D, 2 * ROUTE_LANES), const2),
            pl.BlockSpec((1, ROUTE_LANES), const2),
            pl.BlockSpec((1, D, D_EXPERT), wmap_a),
            pl.BlockSpec((1, D, D_EXPERT), wmap_a),
            pl.BlockSpec((1, D_EXPERT, D), wmap_a),
            pl.BlockSpec((1, D, D_EXPERT), wmap_b),
            pl.BlockSpec((1, D, D_EXPERT), wmap_b),
            pl.BlockSpec((1, D_EXPERT, D), wmap_b),
        ],
        out_specs=pl.BlockSpec((TG, D), lambda j, ea, eb, nv: (j, 0)),
    )
    return pl.pallas_call(
        _expert_kernel,
        grid_spec=grid_spec,
        out_shape=jax.ShapeDtypeStruct((NS, D), F32),
        compiler_params=pltpu.CompilerParams(
            dimension_semantics=("arbitrary",), vmem_limit_bytes=V7X_VMEM_LIMIT),
        name="experts",
    )(ea, eb, n_valid, xs, wr, br, wg, wu, wd, wg, wu, wd)


def _combine_kernel(pos_ref, posn_ref, x1_ref, mod_ref, fg_ref, ys_hbm, y_ref, buf, sems, *, TC):
    i = pl.program_id(0)
    n = pl.num_programs(0)
    slot = i % 2

    def gather(p_ref, s):
        def issue(tile, j, r):
            pltpu.make_async_copy(ys_hbm.at[pl.ds(p_ref[r], 1)],
                                  buf.at[s].at[tile].at[pl.ds(j, 1)], sems.at[s]).start()
        _for_each_row(TC, issue)

    @pl.when(i == 0)
    def _():
        gather(pos_ref, 0)

    @pl.when(i + 1 < n)
    def _():
        gather(posn_ref, 1 - slot)

    pltpu.make_async_copy(ys_hbm.at[pl.ds(0, TC)], buf.at[slot], sems.at[slot]).wait()
    gt2 = mod_ref[0][:, 5 * D_MODEL:6 * D_MODEL]
    x2 = x1_ref[...] + gt2 * buf[slot]
    y_ref[...] = x2 * lax.rsqrt(jnp.mean(x2 * x2, axis=-1, keepdims=True) + EPS) * fg_ref[...]


def _combine(pos, x1, mod, fg, ys, *, T, TC):
    N, D = x1.shape
    n = N // TC
    tiles_per_seq = T // TC
    return pl.pallas_call(
        functools.partial(_combine_kernel, TC=TC),
        grid=(n,),
        in_specs=[pl.BlockSpec((TC,), lambda i: (i,), memory_space=pltpu.SMEM),
                  pl.BlockSpec((TC,), lambda i: (jnp.minimum(i + 1, n - 1),),
                               memory_space=pltpu.SMEM),
                  pl.BlockSpec((TC, D), lambda i: (i, 0)),
                  pl.BlockSpec((1, 1, 6 * D), lambda i: (i // tiles_per_seq, 0, 0)),
                  pl.BlockSpec((1, D), lambda i: (0, 0)),
                  pl.BlockSpec(memory_space=pl.ANY)],
        out_specs=pl.BlockSpec((TC, D), lambda i: (i, 0)),
        out_shape=jax.ShapeDtypeStruct((N, D), F32),
        scratch_shapes=[pltpu.VMEM((2, TC, D), F32), pltpu.SemaphoreType.DMA((2,))],
        compiler_params=pltpu.CompilerParams(
            dimension_semantics=("arbitrary",), vmem_limit_bytes=V7X_VMEM_LIMIT),
        name="combine",
    )(pos, pos, x1, mod, fg, ys)


def _routed_moe(x1, h2, cls, mod, p, *, T):
    N, D = x1.shape
    TG = ROUTED_TILE
    n_tiles = N // TG + N_CLASSES
    pos2d, tcls, info = _rank(cls.reshape(N // 128, 128), TG=TG, n_tiles=n_tiles)
    pos = pos2d.reshape(N)
    tcls = tcls[0, :n_tiles]
    ea = jnp.asarray(CLASS_EXPERT_A)[tcls]
    eb = jnp.asarray(CLASS_EXPERT_B)[tcls]
    n_valid = (info[0, INFO_END:INFO_END + 1] // TG).astype(jnp.int32)
    xs = _dispatch(info.reshape(-1), pos, h2, n_rows=n_tiles * TG, TG=TG, TD=DISPATCH_TILE)
    ys = _experts(ea, eb, n_valid, xs, p["wr"], p["br"], p["wg"], p["wu"], p["wd"], TG=TG)
    return _combine(pos, x1, mod, p["fg"], ys, T=T, TC=COMBINE_TILE)


def _tiling(T):
    if T <= REF_CHUNK:
        return T, T
    return MIXER_CHUNK, min(T, MIXER_TILE)


def _trunk(x, mod, s_ret, s_hg, pos0, inv2, p, *, min_s=HGRN_MIN_BLOCK):
    B, T, D = x.shape
    N = B * T
    C, TT = _tiling(T)
    min_s = min(min_s, C // 2)
    routed = N >= ROUTED_MIN_TOKENS
    tabs = _tables(inv2, p["lb_logits"], T=T, C=C, pos0=pos0, min_s=min_s)
    mod3 = mod.reshape(B, 1, 6 * D)
    x1, h2, cls, sr, sh = _mixer(x, mod3, p["n1g"], p["win"], p["wout"], tabs, p["retg"], p["hgg"],
                                 s_ret, s_hg, p["n2g"], p["wr"], p["br"], C=C, TT=TT,
                                 min_s=min_s, with_cls=routed)
    x1 = x1.reshape(N, D)
    h2 = h2.reshape(N, D)
    if routed:
        y = _routed_moe(x1, h2, cls, mod3, p, T=T)
    else:
        y = _moe(x1, h2, mod3, p["wr"], p["br"], p["wg"], p["wu"], p["wd"], p["fg"],
                 T=T, TM=min(N, 1024))
    return y.reshape(B, T, D), sr[None], sh[None]


def _prepare(W_ada, b_ada, norm1_g, norm2_g, W_in, ret_norm_g, hgrn_norm_g, hgrn_lb_logits, W_out,
             W_router_group, b_router_group, W_router_expert, b_router_expert, W_gate_e, W_up_e,
             W_down_e, final_norm_g):
    assert W_in.shape[0] == 1, "single-layer stack only"
    D = D_MODEL
    wr = jnp.concatenate([W_router_group[0], W_router_expert[0]], axis=1)
    wr = jnp.pad(wr, ((0, 0), (0, ROUTE_LANES - wr.shape[1])))
    wr1 = wr.astype(BF16)
    wr2 = (wr - wr1.astype(F32)).astype(BF16)
    br = jnp.concatenate([b_router_group[0], b_router_expert[0]])
    br = jnp.pad(br, (0, ROUTE_LANES - br.shape[0])).reshape(1, ROUTE_LANES)
    return dict(
        n1g=norm1_g[0].reshape(1, D), n2g=norm2_g[0].reshape(1, D), fg=final_norm_g.reshape(1, D),
        win=W_in[0].astype(BF16), wout=W_out[0].astype(BF16),
        retg=ret_norm_g[0].reshape(1, -1), hgg=hgrn_norm_g[0].reshape(1, -1),
        lb_logits=hgrn_lb_logits,
        wr=jnp.concatenate([wr1, wr2], axis=1), br=br,
        wg=W_gate_e[0], wu=W_up_e[0], wd=W_down_e[0],
    )


def _rope_inv():
    half = DH // 2
    inv = 1.0 / (ROPE_BASE ** (jnp.arange(half, dtype=F32) / half))
    return jnp.concatenate([inv, inv]).reshape(1, DH)


def kernel(x_prompt, x_sample, c_prompt, c_sample, state_ret, state_hgrn, W_ada, b_ada, norm1_g,
           norm2_g, W_in, ret_norm_g, hgrn_norm_g, hgrn_lb_logits, W_out, W_router_group,
           b_router_group, W_router_expert, b_router_expert, W_gate_e, W_up_e, W_down_e,
           final_norm_g):
    p = _prepare(W_ada, b_ada, norm1_g, norm2_g, W_in, ret_norm_g, hgrn_norm_g, hgrn_lb_logits,
                 W_out, W_router_group, b_router_group, W_router_expert, b_router_expert,
                 W_gate_e, W_up_e, W_down_e, final_norm_g)
    Bp = x_prompt.shape[0]
    Bs = x_sample.shape[0]
    inv2 = _rope_inv()
    mod = _ada(jnp.concatenate([c_prompt, c_sample], axis=0), W_ada[0], b_ada[0].reshape(1, -1))
    zeros = jnp.zeros((Bp, HEADS, DH, DH), F32)
    y_p, sr_p, sh_p = _trunk(x_prompt, mod[:Bp], zeros, zeros, 0, inv2, p)
    y_s, sr_s, sh_s = _trunk(x_sample, mod[Bp:Bp + Bs], state_ret[0], state_hgrn[0], PAST_LEN,
                             inv2, p)
    return (y_p, y_s, sr_p, sh_p, sr_s, sh_s)
```

```python
import functools

import numpy as np
import jax
import jax.numpy as jnp
from jax import lax
from jax.experimental import pallas as pl
from jax.experimental.pallas import tpu as pltpu

F32 = jnp.float32
BF16 = jnp.bfloat16

D_MODEL = 1024
HEADS = 4
DH = 128
D_IN = 8 * HEADS * DH
N_GROUPS = 4
EXP_PER_GROUP = 4
N_EXPERTS = N_GROUPS * EXP_PER_GROUP
D_EXPERT = 512
ROPE_BASE = 10000.0
EPS = 1e-6
REF_CHUNK = 64
PAST_LEN = 4096

V7X_VMEM_LIMIT = 56 * 1024 * 1024
ROUTE_LANES = 128

_PAIRS = [(0, 1), (0, 2), (0, 3), (1, 3), (1, 2), (3, 2)]
N_PAIRS = len(_PAIRS)
N_CLASSES = N_GROUPS * N_PAIRS
CLASS_EXPERT_A = np.array([g * EXP_PER_GROUP + i for g in range(N_GROUPS) for i, _ in _PAIRS],
                          np.int32)
CLASS_EXPERT_A = np.append(CLASS_EXPERT_A, CLASS_EXPERT_A[-1])
CLASS_EXPERT_B = np.array([g * EXP_PER_GROUP + j for g in range(N_GROUPS) for _, j in _PAIRS],
                          np.int32)
CLASS_EXPERT_B = np.append(CLASS_EXPERT_B, CLASS_EXPERT_B[-1])
MIXER_CHUNK = 128
HGRN_MIN_BLOCK = 4
MIXER_TILE = 512
MIXER_PROJ_ROWS = 256
ROUTED_MIN_TOKENS = 8192
ROUTED_TILE = 512
DISPATCH_TILE = 2048
COMBINE_TILE = 1024


def _dot(a, b):
    return jnp.dot(a, b, preferred_element_type=F32)


def _dot_nt(a, b):
    return lax.dot_general(a, b, (((1,), (1,)), ((), ())), preferred_element_type=F32)


def _dot_tn(a, b):
    return lax.dot_general(a, b, (((0,), (0,)), ((), ())), preferred_element_type=F32)


def _lanes(parts):
    return parts[0] if len(parts) == 1 else jnp.concatenate(parts, axis=1)


def _block_diag(parts):
    if len(parts) == 1:
        return parts[0]
    rows = []
    for i, p in enumerate(parts):
        rows.append(_lanes([p if j == i else jnp.zeros((p.shape[0], q.shape[1]), p.dtype)
                            for j, q in enumerate(parts)]))
    return jnp.concatenate(rows, axis=0)


def _split3(a):
    a1 = a.astype(BF16)
    r1 = a - a1.astype(F32)
    a2 = r1.astype(BF16)
    a3 = (r1 - a2.astype(F32)).astype(BF16)
    return a1, a2, a3


def _silu(a):
    return a * (1.0 / (1.0 + jnp.exp(-a)))


def _tables_kernel(inv_ref, lbl_ref, cos_ref, sin_ref, rd_ref, idec_ref, kdec_ref, lb_ref,
                   lvl_ref, tri_ref, *, T, C, pos0, min_s):
    pos = (lax.broadcasted_iota(jnp.int32, (T, DH), 0) + pos0).astype(F32)
    ang = pos * inv_ref[...]
    lane = lax.broadcasted_iota(jnp.int32, (T, DH), 1)
    cos_ref[...] = jnp.cos(ang)
    s = jnp.sin(ang)
    sin_ref[...] = jnp.where(lane < DH // 2, -s, s)

    li = lax.broadcasted_iota(jnp.int32, (C, C), 0)
    mi = lax.broadcasted_iota(jnp.int32, (C, C), 1)
    diff = (li - mi).astype(F32)
    rowj = lax.broadcasted_iota(jnp.int32, (C, DH), 0).astype(F32)
    for h in range(HEADS):
        lg = jnp.log(jnp.full((1, 1), 1.0 - 2.0 ** (-5.0 - h), F32))
        rd_ref[h] = jnp.where(diff >= 0, jnp.exp(jnp.maximum(diff, 0.0) * lg), 0.0)
        idec_ref[h] = jnp.exp((rowj + 1.0) * lg)
        kdec_ref[h] = jnp.exp((C - 1.0 - rowj) * lg)

    lg_all = lbl_ref[...]
    m = jnp.max(lg_all, axis=0, keepdims=True)
    e = jnp.exp(lg_all - m)
    lb_ref[...] = e[0:1, :] / jnp.sum(e, axis=0, keepdims=True)

    x = li ^ mi
    code = jnp.zeros((C, C), jnp.int32)
    sz = min_s
    while sz < C:
        code = code + (x >= sz).astype(jnp.int32)
        sz *= 2
    code = jnp.where(x >= min_s, code + (min_s.bit_length() - 1), 0)
    lvl_ref[...] = jnp.where(li >= mi, code, -1)
    tri_ref[...] = (li >= mi).astype(BF16)


def _tables(inv2, lb_logits, *, T, C, pos0, min_s):
    out_shape = (
        jax.ShapeDtypeStruct((T, DH), F32),
        jax.ShapeDtypeStruct((T, DH), F32),
        jax.ShapeDtypeStruct((HEADS, C, C), F32),
        jax.ShapeDtypeStruct((HEADS, C, DH), F32),
        jax.ShapeDtypeStruct((HEADS, C, DH), F32),
        jax.ShapeDtypeStruct((1, HEADS * DH), F32),
        jax.ShapeDtypeStruct((C, C), jnp.int32),
        jax.ShapeDtypeStruct((C, C), BF16),
    )
    return pl.pallas_call(
        functools.partial(_tables_kernel, T=T, C=C, pos0=pos0, min_s=min_s),
        out_shape=out_shape,
        name="tables",
    )(inv2, lb_logits)


def _ada_kernel(c_ref, w_ref, b_ref, o_ref):
    a = _silu(c_ref[...])
    w = w_ref[...]
    a1, a2, _ = _split3(a)
    w1, w2, _ = _split3(w)
    o_ref[...] = _dot(a1, w1) + (_dot(a1, w2) + _dot(a2, w1)) + b_ref[...]


def _ada(c, w, b):
    R = c.shape[0]
    N = w.shape[1]
    TN = 1024
    return pl.pallas_call(
        _ada_kernel,
        grid=(N // TN,),
        in_specs=[pl.BlockSpec((R, D_MODEL), lambda j: (0, 0)),
                  pl.BlockSpec((D_MODEL, TN), lambda j: (0, j)),
                  pl.BlockSpec((1, TN), lambda j: (0, j))],
        out_specs=pl.BlockSpec((R, TN), lambda j: (0, j)),
        out_shape=jax.ShapeDtypeStruct((R, N), F32),
        name="ada",
    )(c, w, b)


def _block_ref(b, block, idx):
    C, W = b.shape
    if block >= 8:
        b3 = b.reshape(C // block, block, W)
        r = jnp.broadcast_to(b3[:, idx:idx + 1, :], b3.shape)
        return r.reshape(C, W)
    row = lax.broadcasted_iota(jnp.int32, (C, W), 0)
    p = row % block
    out = b
    for q in range(block):
        sh = q - idx
        if sh == 0:
            continue
        rolled = pltpu.roll(b, sh % C, 0)
        out = jnp.where(p == q, rolled, out)
    return out


def _head_norm_gate(o, g_row, gate):
    y = o * lax.rsqrt(jnp.mean(o * o, axis=-1, keepdims=True) + EPS)
    return (y * g_row) * _silu(gate)


def _mixer_chunk(r0, proj_scr, cos_ref, sin_ref, rd_ref, idec_ref, kdec_ref, lb_ref, lvl_ref,
                 tri_ref, retg_ref, hgg_ref, sr_scr, sht_scr, o_scr, *, C, min_s):
    W = HEADS * DH
    rows = pl.ds(r0, C)
    proj = proj_scr[rows, :]
    cosf = cos_ref[rows, :]
    sinf = sin_ref[rows, :]

    G = 2 if C % DH == 0 else 1
    groups = [tuple(range(h, h + G)) for h in range(0, HEADS, G)]

    for hs in groups:
        qb_parts, kr_parts, v_parts = [], [], []
        for h in hs:
            c0 = h * DH
            q = proj[:, c0:c0 + DH]
            k = proj[:, W + c0:W + c0 + DH]
            qb_parts.append((q * cosf + pltpu.roll(q, DH // 2, 1) * sinf).astype(BF16))
            kr_parts.append((k * cosf + pltpu.roll(k, DH // 2, 1) * sinf) * (DH ** -0.5))
            v_parts.append(proj[:, 2 * W + c0:2 * W + c0 + DH].astype(BF16))
        qb = _lanes(qb_parts)
        A = (_dot_nt(qb, _block_diag([kr.astype(BF16) for kr in kr_parts]))
             * _lanes([rd_ref[h] for h in hs]))
        S_parts = [sr_scr[h] for h in hs]
        o = (_dot(A.astype(BF16), _block_diag(v_parts))
             + _dot(qb, _block_diag([S.astype(BF16) for S in S_parts]))
             * _lanes([idec_ref[h] for h in hs]))
        for i, h in enumerate(hs):
            c0 = h * DH
            sr_scr[h] = (idec_ref[h][C - 1:C, :] * S_parts[i]
                         + _dot_tn((kr_parts[i] * kdec_ref[h]).astype(BF16), v_parts[i]))
            g = proj[:, 3 * W + c0:3 * W + c0 + DH]
            o_scr[rows, c0:c0 + DH] = _head_norm_gate(
                o[:, i * DH:(i + 1) * DH], retg_ref[:, c0:c0 + DH], g).astype(BF16)

    z = proj[:, 5 * W:6 * W]
    lb = lb_ref[...]
    e = jnp.exp(-jnp.abs(z))
    r = 1.0 / (1.0 + e)
    er = e * r
    sig = jnp.where(z >= 0, r, er)
    nsig = jnp.where(z >= 0, er, r)
    logf = jnp.log(lb + (1.0 - lb) * sig)
    kk_all = (1.0 - lb) * nsig
    tri = tri_ref[...]
    l1, l2, _ = _split3(logf)
    b_all = _dot(tri, l1) + _dot(tri, l2)

    GW = G * DH
    lvl = _lanes([lvl_ref[...]] * G)
    rowc = lax.broadcasted_iota(jnp.int32, (C, GW), 0)
    levels = []
    s = C // 2
    while s >= min_s:
        isq = (rowc % (2 * s)) >= s
        levels.append((s, isq, jnp.where(isq, 1.0, -1.0)))
        s //= 2

    def heads_of(a):
        return [a[:, i * DH:(i + 1) * DH] for i in range(G)]

    for hs in groups:
        c0 = hs[0] * DH
        q = proj[:, 4 * W + c0:4 * W + c0 + GW] * (DH ** -0.5)
        kk = kk_all[:, c0:c0 + GW]
        b = b_all[:, c0:c0 + GW]
        v_parts = heads_of(proj[:, 6 * W + c0:6 * W + c0 + GW])
        vb_parts = [v.astype(BF16) for v in v_parts]

        A = jnp.zeros((C, G * C), F32)
        for s, isq, sign in levels:
            ref = _block_ref(b, 2 * s, s - 1)
            zz = jnp.where(isq, q, kk) * jnp.exp((b - ref) * sign)
            zb = zz.astype(BF16)
            A = jnp.where(lvl == s.bit_length(), _dot_nt(zb, _block_diag(heads_of(zb))), A)
        if min_s > 1:
            ref = _block_ref(b, min_s, min_s // 2)
            d = b - ref
            P = _dot_nt((q * jnp.exp(d)).astype(BF16),
                        _block_diag(heads_of((kk * jnp.exp(-d)).astype(BF16))))
            A = jnp.where(lvl == 0, P, A)
            o = _dot(A.astype(BF16), _block_diag(vb_parts))
        else:
            diag = [jnp.sum(qh * kh, axis=-1, keepdims=True) * v
                    for qh, kh, v in zip(heads_of(q), heads_of(kk), v_parts)]
            o = _dot(A.astype(BF16), _block_diag(vb_parts)) + _lanes(diag)

        ST_parts = [sht_scr[h] for h in hs]
        bl = b[C - 1:C, :]
        o = o + _dot_nt((q * jnp.exp(b)).astype(BF16),
                        _block_diag([ST.astype(BF16) for ST in ST_parts]))
        kdec_parts = heads_of((kk * jnp.exp(bl - b)).astype(BF16))
        ebl_parts = heads_of(jnp.exp(bl))
        for i, h in enumerate(hs):
            ch = h * DH
            sht_scr[h] = ST_parts[i] * ebl_parts[i] + _dot_tn(vb_parts[i], kdec_parts[i])
            g = proj[:, 7 * W + ch:7 * W + ch + DH]
            o_scr[rows, W + ch:W + ch + DH] = _head_norm_gate(
                o[:, i * DH:(i + 1) * DH], hgg_ref[:, ch:ch + DH], g).astype(BF16)


def _mixer_kernel(x_ref, xnext_ref, mod_ref, modnext_ref, n1g_ref, win_ref, wout_ref, cos_ref,
                  sin_ref, rd_ref, idec_ref, kdec_ref, lb_ref, lvl_ref, tri_ref, retg_ref, hgg_ref,
                  sr0_ref, sh0_ref, n2g_ref, wr_ref, br_ref,
                  x1_ref, h2_ref, cls_ref, sro_ref, sho_ref,
                  sr_scr, sht_scr, proj_scr, o_scr, *, C, min_s, with_cls, has_next):
    t = pl.program_id(1)
    nt = pl.num_programs(1)
    TT = x_ref.shape[1]
    n_chunks = TT // C

    mod = mod_ref[0]
    gt1 = mod[:, 2 * D_MODEL:3 * D_MODEL]

    PR = xnext_ref.shape[1]
    per_block = PR // C

    def project(xc, m, r0):
        xn = xc * lax.rsqrt(jnp.mean(xc * xc, axis=-1, keepdims=True) + EPS) * n1g_ref[...]
        hmod = xn * (1.0 + m[:, D_MODEL:2 * D_MODEL]) + m[:, 0:D_MODEL]
        proj_scr[r0:r0 + PR, :] = _dot(hmod.astype(BF16), win_ref[...])

    @pl.when(t == 0)
    def _():
        for h in range(HEADS):
            sr_scr[h] = sr0_ref[0, h]
            sht_scr[h] = sh0_ref[0, h].T

    first = (t == 0) & (pl.program_id(0) == 0) if has_next else t == 0

    @pl.when(first)
    def _():
        project(x_ref[0, 0:PR, :], mod, 0)

    chunk = functools.partial(
        _mixer_chunk, proj_scr=proj_scr, cos_ref=cos_ref, sin_ref=sin_ref, rd_ref=rd_ref,
        idec_ref=idec_ref, kdec_ref=kdec_ref, lb_ref=lb_ref, lvl_ref=lvl_ref, tri_ref=tri_ref,
        retg_ref=retg_ref, hgg_ref=hgg_ref, sr_scr=sr_scr, sht_scr=sht_scr, o_scr=o_scr,
        C=C, min_s=min_s)
    for c in range(n_chunks):
        chunk(c * C)
        if c % per_block == 0:
            r1 = (c // per_block + 1) * PR
            if r1 < TT:
                project(x_ref[0, r1:r1 + PR, :], mod, r1)
            elif has_next:
                project(xnext_ref[0], modnext_ref[0], 0)

    x = x_ref[0]
    mix = _dot(o_scr[...], wout_ref[...])
    x1 = x + gt1 * mix
    x1_ref[0] = x1

    sh2 = mod[:, 3 * D_MODEL:4 * D_MODEL]
    sc2 = mod[:, 4 * D_MODEL:5 * D_MODEL]
    x1n = x1 * lax.rsqrt(jnp.mean(x1 * x1, axis=-1, keepdims=True) + EPS) * n2g_ref[...]
    h2 = x1n * (1.0 + sc2) + sh2
    h2_ref[0] = h2
    if with_cls:
        g_idx, i1, i2 = _route_topk(_router_logits(h2, wr_ref, br_ref).T)[:3]
        lo = jnp.minimum(i1, i2)
        hi = jnp.maximum(i1, i2)
        pair = jnp.where(lo == 0.0, hi - 1.0, jnp.where(lo == 1.0, 6.0 - hi, 5.0))
        cls_ref[0] = (g_idx * float(N_PAIRS) + pair).astype(jnp.int32)
    else:
        cls_ref[0] = jnp.zeros((1, TT), jnp.int32)

    @pl.when(t == nt - 1)
    def _():
        for h in range(HEADS):
            sro_ref[0, h] = sr_scr[h]
            sho_ref[0, h] = sht_scr[h].T


def _mixer(x, mod, n1g, win, wout, tabs, retg, hgg, sr0, sh0, n2g, wr, br, *, C, TT, min_s,
           with_cls):
    B, T, D = x.shape
    cosf, sinf, rd, idec, kdec, lb, lvl, tri = tabs
    nt = T // TT
    const2 = lambda b, t: (0, 0)
    const3 = lambda b, t: (0, 0, 0)
    state_spec = pl.BlockSpec((1, HEADS, DH, DH), lambda b, t: (b, 0, 0, 0))
    PR = min(TT, MIXER_PROJ_ROWS)
    per_tile = TT // PR
    n_blocks = T // PR

    def next_seq(b, t):
        return jnp.minimum(b + ((t + 1) * per_tile) // n_blocks, B - 1)

    in_specs = [
        pl.BlockSpec((1, TT, D), lambda b, t: (b, t, 0)),
        pl.BlockSpec((1, PR, D),
                     lambda b, t: (next_seq(b, t), ((t + 1) * per_tile) % n_blocks, 0)),
        pl.BlockSpec((1, 1, 6 * D), lambda b, t: (b, 0, 0)),
        pl.BlockSpec((1, 1, 6 * D), lambda b, t: (next_seq(b, t), 0, 0)),
        pl.BlockSpec((1, D), const2),
        pl.BlockSpec((D, D_IN), const2),
        pl.BlockSpec((D, D), const2),
        pl.BlockSpec((TT, DH), lambda b, t: (t, 0)),
        pl.BlockSpec((TT, DH), lambda b, t: (t, 0)),
        pl.BlockSpec((HEADS, C, C), const3),
        pl.BlockSpec((HEADS, C, DH), const3),
        pl.BlockSpec((HEADS, C, DH), const3),
        pl.BlockSpec((1, HEADS * DH), const2),
        pl.BlockSpec((C, C), const2),
        pl.BlockSpec((C, C), const2),
        pl.BlockSpec((1, HEADS * DH), const2),
        pl.BlockSpec((1, HEADS * DH), const2),
        state_spec, state_spec,
        pl.BlockSpec((1, D), const2),
        pl.BlockSpec((D, 2 * ROUTE_LANES), const2),
        pl.BlockSpec((1, ROUTE_LANES), const2),
    ]
    tok_spec = pl.BlockSpec((1, TT, D), lambda b, t: (b, t, 0))
    out_specs = [tok_spec, tok_spec,
                 pl.BlockSpec((1, 1, TT), lambda b, t: (b * nt + t, 0, 0)),
                 state_spec, state_spec]
    out_shape = [jax.ShapeDtypeStruct((B, T, D), F32),
                 jax.ShapeDtypeStruct((B, T, D), F32),
                 jax.ShapeDtypeStruct((B * nt, 1, TT), jnp.int32),
                 jax.ShapeDtypeStruct((B, HEADS, DH, DH), F32),
                 jax.ShapeDtypeStruct((B, HEADS, DH, DH), F32)]
    return pl.pallas_call(
        functools.partial(_mixer_kernel, C=C, min_s=min_s, with_cls=with_cls, has_next=nt > 1),
        grid=(B, nt),
        in_specs=in_specs,
        out_specs=out_specs,
        out_shape=out_shape,
        scratch_shapes=[pltpu.VMEM((HEADS, DH, DH), F32),
                        pltpu.VMEM((HEADS, DH, DH), F32),
                        pltpu.VMEM((TT, D_IN), F32),
                        pltpu.VMEM((TT, D), BF16)],
        compiler_params=pltpu.CompilerParams(
            dimension_semantics=("arbitrary", "arbitrary"),
            vmem_limit_bytes=V7X_VMEM_LIMIT),
        name="mixer",
    )(x, x, mod, mod, n1g, win, wout, cosf, sinf, rd, idec, kdec, lb, lvl, tri, retg, hgg, sr0, sh0,
      n2g, wr, br)


def _expert_w(w_ref):
    return w_ref[0].astype(BF16)


def _router_logits(h, wr_ref, br_ref):
    h1, h2, _ = _split3(h)
    hi = _dot(h1, wr_ref[...])
    lo = _dot(h2, wr_ref[:, 0:ROUTE_LANES])
    return hi[:, 0:ROUTE_LANES] + (hi[:, ROUTE_LANES:] + lo) + br_ref[...]


def _route_topk(logits_t):
    TM = logits_t.shape[1]
    lg = logits_t[0:N_GROUPS, :]
    gi = lax.broadcasted_iota(jnp.int32, (N_GROUPS, TM), 0).astype(F32)
    m = jnp.max(lg, axis=0, keepdims=True)
    g_w = 1.0 / jnp.sum(jnp.exp(lg - m), axis=0, keepdims=True)
    g_idx = jnp.min(jnp.where(lg == m, gi, float(N_GROUPS)), axis=0, keepdims=True)
    el = jnp.zeros((EXP_PER_GROUP, TM), F32)
    for g in range(N_GROUPS):
        lo = N_GROUPS + g * EXP_PER_GROUP
        el = jnp.where(g_idx == float(g), logits_t[lo:lo + EXP_PER_GROUP, :], el)
    ei = lax.broadcasted_iota(jnp.int32, (EXP_PER_GROUP, TM), 0).astype(F32)
    m1 = jnp.max(el, axis=0, keepdims=True)
    i1 = jnp.min(jnp.where(el == m1, ei, float(EXP_PER_GROUP)), axis=0, keepdims=True)
    el2 = jnp.where(ei == i1, -jnp.inf, el)
    m2 = jnp.max(el2, axis=0, keepdims=True)
    i2 = jnp.min(jnp.where(el2 == m2, ei, float(EXP_PER_GROUP)), axis=0, keepdims=True)
    p2 = jnp.exp(m2 - m1)
    den = 1.0 / (1.0 + p2)
    w0 = g_w * den
    w1 = g_w * (p2 * den)
    return g_idx, i1, i2, w0, w1


def _moe_kernel(x1_ref, h2_ref, mod_ref, wr_ref, br_ref, wg_ref, wu_ref, wd_ref, fg_ref,
                y_ref, h_scr, route_scr, acc_scr):
    e = pl.program_id(1)
    ne = pl.num_programs(1)
    TM, D = x1_ref.shape
    K = mod_ref.shape[0]
    TQ = TM // K

    @pl.when(e == 0)
    def _():
        h = h2_ref[...]
        h_scr[...] = h.astype(BF16)
        g_idx, i1, i2, w0, wgt1 = _route_topk(_router_logits(h, wr_ref, br_ref).T)
        e0 = g_idx * EXP_PER_GROUP + i1
        e1 = g_idx * EXP_PER_GROUP + i2
        ri = lax.broadcasted_iota(jnp.int32, (ROUTE_LANES, TM), 0)
        rows = jnp.where(ri == 0, e0, jnp.where(ri == 1, e1, jnp.where(ri == 2, w0,
                         jnp.where(ri == 3, wgt1, 0.0))))
        route_scr[...] = rows.T
        acc_scr[...] = jnp.zeros_like(acc_scr)

    hb = h_scr[...]
    hid = _silu(_dot(hb, _expert_w(wg_ref))) * _dot(hb, _expert_w(wu_ref))
    out = _dot(hid.astype(BF16), _expert_w(wd_ref))
    ef = e.astype(F32)
    route = route_scr[...]
    comb = (jnp.where(route[:, 0:1] == ef, route[:, 2:3], 0.0)
            + jnp.where(route[:, 1:2] == ef, route[:, 3:4], 0.0))
    acc_scr[...] += comb * out

    @pl.when(e == ne - 1)
    def _():
        gt2 = mod_ref[...][:, :, 5 * D:6 * D]
        ff = (gt2 * acc_scr[...].reshape(K, TQ, D)).reshape(TM, D)
        x2 = x1_ref[...] + ff
        y_ref[...] = x2 * lax.rsqrt(jnp.mean(x2 * x2, axis=-1, keepdims=True) + EPS) * fg_ref[...]


def _moe(x1, h2, mod, wr, br, wg, wu, wd, fg, *, T, TM):
    N, D = x1.shape
    if T >= TM:
        assert T % TM == 0
        seqs, tiles_per_seq = 1, T // TM
        mod_map = lambda i, e: (i // tiles_per_seq, 0, 0)
    else:
        assert TM % T == 0
        seqs = TM // T
        mod_map = lambda i, e: (i, 0, 0)
    return pl.pallas_call(
        _moe_kernel,
        grid=(N // TM, N_EXPERTS),
        in_specs=[
            pl.BlockSpec((TM, D), lambda i, e: (i, 0)),
            pl.BlockSpec((TM, D), lambda i, e: (i, 0)),
            pl.BlockSpec((seqs, 1, 6 * D), mod_map),
            pl.BlockSpec((D, 2 * ROUTE_LANES), lambda i, e: (0, 0)),
            pl.BlockSpec((1, ROUTE_LANES), lambda i, e: (0, 0)),
            pl.BlockSpec((1, D, D_EXPERT), lambda i, e: (e, 0, 0)),
            pl.BlockSpec((1, D, D_EXPERT), lambda i, e: (e, 0, 0)),
            pl.BlockSpec((1, D_EXPERT, D), lambda i, e: (e, 0, 0)),
            pl.BlockSpec((1, D), lambda i, e: (0, 0)),
        ],
        out_specs=pl.BlockSpec((TM, D), lambda i, e: (i, 0)),
        out_shape=jax.ShapeDtypeStruct((N, D), F32),
        scratch_shapes=[pltpu.VMEM((TM, D), BF16),
                        pltpu.VMEM((TM, ROUTE_LANES), F32),
                        pltpu.VMEM((TM, D), F32)],
        compiler_params=pltpu.CompilerParams(
            dimension_semantics=("arbitrary", "arbitrary"),
            vmem_limit_bytes=V7X_VMEM_LIMIT),
        name="moe",
    )(x1, h2, mod, wr, br, wg, wu, wd, fg)


INFO_TAIL_LEN = 32
INFO_END = 64


def _rank_kernel(cls_ref, pos_ref, tcls_ref, info_ref, *, TG):
    R, L = cls_ref.shape
    info_lane = lax.broadcasted_iota(jnp.int32, info_ref.shape, 1)
    info = jnp.zeros(info_ref.shape, F32)
    cf = cls_ref[...].astype(F32)
    ri = lax.broadcasted_iota(jnp.int32, (L, L), 0)
    ci = lax.broadcasted_iota(jnp.int32, (L, L), 1)
    upper = (ri < ci).astype(BF16)
    rr = lax.broadcasted_iota(jnp.int32, (R, R), 0)
    rc = lax.broadcasted_iota(jnp.int32, (R, R), 1)
    lower = (rc < rr).astype(BF16)
    tile_start = lax.broadcasted_iota(jnp.int32, tcls_ref.shape, 1).astype(F32) * float(TG)
    base = jnp.zeros((1, 1), F32)
    pos = jnp.zeros((R, L), F32)
    tcls = jnp.zeros(tcls_ref.shape, F32)
    for c in range(N_CLASSES):
        ind = jnp.where(cf == float(c), 1.0, 0.0)
        lane_pre = _dot(ind.astype(BF16), upper)
        row_tot = jnp.broadcast_to(jnp.sum(ind, axis=1, keepdims=True), (R, L))
        row_pre = _dot(lower, row_tot.astype(BF16))
        total = jnp.sum(row_tot[:, 0:1], axis=0, keepdims=True)
        pos = jnp.where(ind > 0.0, base + row_pre + lane_pre, pos)
        padded = jnp.floor((total + float(TG - 1)) * (1.0 / TG)) * float(TG)
        info = jnp.where(info_lane == c, base + total, info)
        info = jnp.where(info_lane == INFO_TAIL_LEN + c, padded - total, info)
        base = base + padded
        tcls = tcls + jnp.where(tile_start >= base, 1.0, 0.0)
    info = jnp.where(info_lane == INFO_END, base, info)
    pos_ref[...] = pos.astype(jnp.int32)
    tcls_ref[...] = tcls.astype(jnp.int32)
    info_ref[...] = info.astype(jnp.int32)


def _rank(cls2d, *, TG, n_tiles):
    R, L = cls2d.shape
    lanes = -(-n_tiles // 128) * 128
    return pl.pallas_call(
        functools.partial(_rank_kernel, TG=TG),
        out_shape=(jax.ShapeDtypeStruct((R, L), jnp.int32),
                   jax.ShapeDtypeStruct((1, lanes), jnp.int32),
                   jax.ShapeDtypeStruct((1, 128), jnp.int32)),
        name="rank",
    )(cls2d)


SUBLANES = 8


def _for_each_row(n_rows, fn):
    def group(g, carry):
        r0 = pl.multiple_of(g * SUBLANES, SUBLANES)
        for j in range(SUBLANES):
            fn(pl.ds(r0, SUBLANES), j, r0 + j)
        return carry
    lax.fori_loop(0, n_rows // SUBLANES, group, 0)


def _dispatch_kernel(info_ref, pos_ref, h2_hbm, xs_hbm, zero_scr, tiles, zsem, load_sems,
                     row_sems, *, TD):
    ZR = zero_scr.shape[0]

    def tail_copies(do):
        for c in range(N_CLASSES):
            start = info_ref[c]
            head = (-start) & (SUBLANES - 1)
            for i in range(SUBLANES - 1):
                @pl.when(i < head)
                def _(i=i):
                    do(pltpu.make_async_copy(zero_scr.at[pl.ds(0, 1)],
                                             xs_hbm.at[pl.ds(start + i, 1)], zsem))
            rest = info_ref[INFO_TAIL_LEN + c] - head
            off = start + head
            k = ZR
            while k >= SUBLANES:
                @pl.when((rest & k) != 0)
                def _(k=k, off=off):
                    do(pltpu.make_async_copy(
                        zero_scr.at[pl.ds(0, k)],
                        xs_hbm.at[pl.ds(pl.multiple_of(off, SUBLANES), k)], zsem))
                off = off + (rest & k)
                k //= 2

    def unused_copies(do):
        first = info_ref[INFO_END] // ZR

        def body(i, carry):
            do(pltpu.make_async_copy(
                zero_scr, xs_hbm.at[pl.ds(pl.multiple_of(i * ZR, ZR), ZR)], zsem))
            return carry
        lax.fori_loop(first, xs_hbm.shape[0] // ZR, body, 0)

    @pl.when(pl.program_id(0) == 0)
    def _():
        zero_scr[...] = jnp.zeros_like(zero_scr)
        tail_copies(lambda cp: cp.start())
        unused_copies(lambda cp: cp.start())
        tail_copies(lambda cp: cp.wait())
        unused_copies(lambda cp: cp.wait())

    i = pl.program_id(0)
    n = pl.num_programs(0)

    def load(tile, do):
        s = tile % 3
        do(pltpu.make_async_copy(h2_hbm.at[pl.ds(pl.multiple_of(tile * TD, TD), TD)],
                                 tiles.at[s], load_sems.at[s]))

    def rows_wait(tile):
        pltpu.make_async_copy(tiles.at[0], xs_hbm.at[pl.ds(0, TD)], row_sems.at[tile % 2]).wait()

    @pl.when(i == 0)
    def _():
        load(0, lambda cp: cp.start())

    @pl.when(i + 1 < n)
    def _():
        load(i + 1, lambda cp: cp.start())

    load(i, lambda cp: cp.wait())
    slot = i % 3

    def issue(tile, j, r):
        pltpu.make_async_copy(tiles.at[slot].at[tile].at[pl.ds(j, 1)],
                              xs_hbm.at[pl.ds(pos_ref[r], 1)], row_sems.at[i % 2]).start()

    _for_each_row(TD, issue)

    @pl.when(i > 0)
    def _():
        rows_wait(i - 1)

    @pl.when(i == n - 1)
    def _():
        rows_wait(i)


def _dispatch(info, pos, h2, *, n_rows, TG, TD):
    N, D = h2.shape
    grid_spec = pltpu.PrefetchScalarGridSpec(
        num_scalar_prefetch=1,
        grid=(N // TD,),
        in_specs=[pl.BlockSpec((TD,), lambda i, info: (i,), memory_space=pltpu.SMEM),
                  pl.BlockSpec(memory_space=pl.ANY)],
        out_specs=pl.BlockSpec(memory_space=pl.ANY),
        scratch_shapes=[pltpu.VMEM((TG // 2, D), F32), pltpu.VMEM((3, TD, D), F32),
                        pltpu.SemaphoreType.DMA(()), pltpu.SemaphoreType.DMA((3,)),
                        pltpu.SemaphoreType.DMA((2,))],
    )
    return pl.pallas_call(
        functools.partial(_dispatch_kernel, TD=TD),
        grid_spec=grid_spec,
        out_shape=jax.ShapeDtypeStruct((n_rows, D), F32),
        compiler_params=pltpu.CompilerParams(
            dimension_semantics=("arbitrary",), vmem_limit_bytes=V7X_VMEM_LIMIT),
        name="dispatch",
    )(info, pos, h2)


def _expert_kernel(ea_ref, eb_ref, nv_ref, xs_ref, wr_ref, br_ref, wga_ref, wua_ref, wda_ref,
                   wgb_ref, wub_ref, wdb_ref, ys_ref):
    j = pl.program_id(0)
    TG = xs_ref.shape[0]
    n_valid = nv_ref[0]

    @pl.when(j < n_valid)
    def _():
        a = ea_ref[j]
        b = eb_ref[j]
        g = lax.div(a, jnp.int32(EXP_PER_GROUP))
        hb = xs_ref[...].astype(BF16)
        logits = _dot(hb, wr_ref[:, 0:ROUTE_LANES]) + br_ref[...]
        lane = lax.broadcasted_iota(jnp.int32, (TG, ROUTE_LANES), 1)
        isg = lane < N_GROUPS
        m = jnp.max(jnp.where(isg, logits, -jnp.inf), axis=1, keepdims=True)
        ssum = jnp.sum(jnp.where(isg, jnp.exp(logits - m), 0.0), axis=1, keepdims=True)
        lg = jnp.sum(jnp.where(lane == g, logits, 0.0), axis=1, keepdims=True)
        la = jnp.sum(jnp.where(lane == N_GROUPS + a, logits, 0.0), axis=1, keepdims=True)
        lb = jnp.sum(jnp.where(lane == N_GROUPS + b, logits, 0.0), axis=1, keepdims=True)
        g_w = jnp.exp(lg - m) / ssum
        mm = jnp.maximum(la, lb)
        pa = jnp.exp(la - mm)
        pb = jnp.exp(lb - mm)
        den = g_w / (pa + pb)
        hid_a = _silu(_dot(hb, _expert_w(wga_ref))) * _dot(hb, _expert_w(wua_ref))
        out = (pa * den) * _dot(hid_a.astype(BF16), _expert_w(wda_ref))
        hid_b = _silu(_dot(hb, _expert_w(wgb_ref))) * _dot(hb, _expert_w(wub_ref))
        ys_ref[...] = out + (pb * den) * _dot(hid_b.astype(BF16), _expert_w(wdb_ref))

    @pl.when(j >= n_valid)
    def _():
        ys_ref[...] = jnp.zeros_like(ys_ref)


def _experts(ea, eb, n_valid, xs, wr, br, wg, wu, wd, *, TG):
    NS, D = xs.shape
    const2 = lambda j, ea, eb, nv: (0, 0)
    wmap_a = lambda j, ea, eb, nv: (ea[j], 0, 0)
    wmap_b = lambda j, ea, eb, nv: (eb[j], 0, 0)
    grid_spec = pltpu.PrefetchScalarGridSpec(
        num_scalar_prefetch=3,
        grid=(NS // TG,),
        in_specs=[
            pl.BlockSpec((TG, D), lambda j, ea, eb, nv: (jnp.minimum(j, nv[0] - 1), 0)),
            pl.BlockSpec((D, 2 * ROUTE_LANES), const2),
            pl.BlockSpec((1, ROUTE_LANES), const2),
            pl.BlockSpec((1, D, D_EXPERT), wmap_a),
            pl.BlockSpec((1, D, D_EXPERT), wmap_a),
            pl.BlockSpec((1, D_EXPERT, D), wmap_a),
            pl.BlockSpec((1, D, D_EXPERT), wmap_b),
            pl.BlockSpec((1, D, D_EXPERT), wmap_b),
            pl.BlockSpec((1, D_EXPERT, D), wmap_b),
        ],
        out_specs=pl.BlockSpec((TG, D), lambda j, ea, eb, nv: (j, 0)),
    )
    return pl.pallas_call(
        _expert_kernel,
        grid_spec=grid_spec,
        out_shape=jax.ShapeDtypeStruct((NS, D), F32),
        compiler_params=pltpu.CompilerParams(
            dimension_semantics=("arbitrary",), vmem_limit_bytes=V7X_VMEM_LIMIT),
        name="experts",
    )(ea, eb, n_valid, xs, wr, br, wg, wu, wd, wg, wu, wd)


def _combine_kernel(pos_ref, posn_ref, x1_ref, mod_ref, fg_ref, ys_hbm, y_ref, buf, sems, *, TC):
    i = pl.program_id(0)
    n = pl.num_programs(0)
    slot = i % 2

    def gather(p_ref, s):
        def issue(tile, j, r):
            pltpu.make_async_copy(ys_hbm.at[pl.ds(p_ref[r], 1)],
                                  buf.at[s].at[tile].at[pl.ds(j, 1)], sems.at[s]).start()
        _for_each_row(TC, issue)

    @pl.when(i == 0)
    def _():
        gather(pos_ref, 0)

    @pl.when(i + 1 < n)
    def _():
        gather(posn_ref, 1 - slot)

    pltpu.make_async_copy(ys_hbm.at[pl.ds(0, TC)], buf.at[slot], sems.at[slot]).wait()
    gt2 = mod_ref[0][:, 5 * D_MODEL:6 * D_MODEL]
    x2 = x1_ref[...] + gt2 * buf[slot]
    y_ref[...] = x2 * lax.rsqrt(jnp.mean(x2 * x2, axis=-1, keepdims=True) + EPS) * fg_ref[...]


def _combine(pos, x1, mod, fg, ys, *, T, TC):
    N, D = x1.shape
    n = N // TC
    tiles_per_seq = T // TC
    return pl.pallas_call(
        functools.partial(_combine_kernel, TC=TC),
        grid=(n,),
        in_specs=[pl.BlockSpec((TC,), lambda i: (i,), memory_space=pltpu.SMEM),
                  pl.BlockSpec((TC,), lambda i: (jnp.minimum(i + 1, n - 1),),
                               memory_space=pltpu.SMEM),
                  pl.BlockSpec((TC, D), lambda i: (i, 0)),
                  pl.BlockSpec((1, 1, 6 * D), lambda i: (i // tiles_per_seq, 0, 0)),
                  pl.BlockSpec((1, D), lambda i: (0, 0)),
                  pl.BlockSpec(memory_space=pl.ANY)],
        out_specs=pl.BlockSpec((TC, D), lambda i: (i, 0)),
        out_shape=jax.ShapeDtypeStruct((N, D), F32),
        scratch_shapes=[pltpu.VMEM((2, TC, D), F32), pltpu.SemaphoreType.DMA((2,))],
        compiler_params=pltpu.CompilerParams(
            dimension_semantics=("arbitrary",), vmem_limit_bytes=V7X_VMEM_LIMIT),
        name="combine",
    )(pos, pos, x1, mod, fg, ys)


def _routed_moe(x1, h2, cls, mod, p, *, T):
    N, D = x1.shape
    TG = ROUTED_TILE
    n_tiles = N // TG + N_CLASSES
    pos2d, tcls, info = _rank(cls.reshape(N // 128, 128), TG=TG, n_tiles=n_tiles)
    pos = pos2d.reshape(N)
    tcls = tcls[0, :n_tiles]
    ea = jnp.asarray(CLASS_EXPERT_A)[tcls]
    eb = jnp.asarray(CLASS_EXPERT_B)[tcls]
    n_valid = (info[0, INFO_END:INFO_END + 1] // TG).astype(jnp.int32)
    xs = _dispatch(info.reshape(-1), pos, h2, n_rows=n_tiles * TG, TG=TG, TD=DISPATCH_TILE)
    ys = _experts(ea, eb, n_valid, xs, p["wr"], p["br"], p["wg"], p["wu"], p["wd"], TG=TG)
    return _combine(pos, x1, mod, p["fg"], ys, T=T, TC=COMBINE_TILE)


def _tiling(T):
    if T <= REF_CHUNK:
        return T, T
    return MIXER_CHUNK, min(T, MIXER_TILE)


def _trunk(x, mod, s_ret, s_hg, pos0, inv2, p, *, min_s=HGRN_MIN_BLOCK):
    B, T, D = x.shape
    N = B * T
    C, TT = _tiling(T)
    min_s = min(min_s, C // 2)
    routed = N >= ROUTED_MIN_TOKENS
    tabs = _tables(inv2, p["lb_logits"], T=T, C=C, pos0=pos0, min_s=min_s)
    mod3 = mod.reshape(B, 1, 6 * D)
    x1, h2, cls, sr, sh = _mixer(x, mod3, p["n1g"], p["win"], p["wout"], tabs, p["retg"], p["hgg"],
                                 s_ret, s_hg, p["n2g"], p["wr"], p["br"], C=C, TT=TT,
                                 min_s=min_s, with_cls=routed)
    x1 = x1.reshape(N, D)
    h2 = h2.reshape(N, D)
    if routed:
        y = _routed_moe(x1, h2, cls, mod3, p, T=T)
    else:
        y = _moe(x1, h2, mod3, p["wr"], p["br"], p["wg"], p["wu"], p["wd"], p["fg"],
                 T=T, TM=min(N, 1024))
    return y.reshape(B, T, D), sr[None], sh[None]


def _prepare(W_ada, b_ada, norm1_g, norm2_g, W_in, ret_norm_g, hgrn_norm_g, hgrn_lb_logits, W_out,
             W_router_group, b_router_group, W_router_expert, b_router_expert, W_gate_e, W_up_e,
             W_down_e, final_norm_g):
    assert W_in.shape[0] == 1, "single-layer stack only"
    D = D_MODEL
    wr = jnp.concatenate([W_router_group[0], W_router_expert[0]], axis=1)
    wr = jnp.pad(wr, ((0, 0), (0, ROUTE_LANES - wr.shape[1])))
    wr1 = wr.astype(BF16)
    wr2 = (wr - wr1.astype(F32)).astype(BF16)
    br = jnp.concatenate([b_router_group[0], b_router_expert[0]])
    br = jnp.pad(br, (0, ROUTE_LANES - br.shape[0])).reshape(1, ROUTE_LANES)
    return dict(
        n1g=norm1_g[0].reshape(1, D), n2g=norm2_g[0].reshape(1, D), fg=final_norm_g.reshape(1, D),
        win=W_in[0].astype(BF16), wout=W_out[0].astype(BF16),
        retg=ret_norm_g[0].reshape(1, -1), hgg=hgrn_norm_g[0].reshape(1, -1),
        lb_logits=hgrn_lb_logits,
        wr=jnp.concatenate([wr1, wr2], axis=1), br=br,
        wg=W_gate_e[0], wu=W_up_e[0], wd=W_down_e[0],
    )


def _rope_inv():
    half = DH // 2
    inv = 1.0 / (ROPE_BASE ** (jnp.arange(half, dtype=F32) / half))
    return jnp.concatenate([inv, inv]).reshape(1, DH)


def kernel(x_prompt, x_sample, c_prompt, c_sample, state_ret, state_hgrn, W_ada, b_ada, norm1_g,
           norm2_g, W_in, ret_norm_g, hgrn_norm_g, hgrn_lb_logits, W_out, W_router_group,
           b_router_group, W_router_expert, b_router_expert, W_gate_e, W_up_e, W_down_e,
           final_norm_g):
    p = _prepare(W_ada, b_ada, norm1_g, norm2_g, W_in, ret_norm_g, hgrn_norm_g, hgrn_lb_logits,
                 W_out, W_router_group, b_router_group, W_router_expert, b_router_expert,
                 W_gate_e, W_up_e, W_down_e, final_norm_g)
    Bp = x_prompt.shape[0]
    Bs = x_sample.shape[0]
    inv2 = _rope_inv()
    mod = _ada(jnp.concatenate([c_prompt, c_sample], axis=0), W_ada[0], b_ada[0].reshape(1, -1))
    zeros = jnp.zeros((Bp, HEADS, DH, DH), F32)
    y_p, sr_p, sh_p = _trunk(x_prompt, mod[:Bp], zeros, zeros, 0, inv2, p)
    y_s, sr_s, sh_s = _trunk(x_sample, mod[Bp:Bp + Bs], state_ret[0], state_hgrn[0], PAST_LEN,
                             inv2, p)
    return (y_p, y_s, sr_p, sh_p, sr_s, sh_s)
```

```python
import functools

import numpy as np
import jax
import jax.numpy as jnp
from jax import lax
from jax.experimental import pallas as pl
from jax.experimental.pallas import tpu as pltpu

F32 = jnp.float32
BF16 = jnp.bfloat16

D_MODEL = 1024
HEADS = 4
DH = 128
D_IN = 8 * HEADS * DH
N_GROUPS = 4
EXP_PER_GROUP = 4
N_EXPERTS = N_GROUPS * EXP_PER_GROUP
D_EXPERT = 512
ROPE_BASE = 10000.0
EPS = 1e-6
REF_CHUNK = 64
PAST_LEN = 4096

V7X_VMEM_LIMIT = 56 * 1024 * 1024
ROUTE_LANES = 128

_PAIRS = [(0, 1), (0, 2), (0, 3), (1, 3), (1, 2), (3, 2)]
N_PAIRS = len(_PAIRS)
N_CLASSES = N_GROUPS * N_PAIRS
CLASS_EXPERT_A = np.array([g * EXP_PER_GROUP + i for g in range(N_GROUPS) for i, _ in _PAIRS],
                          np.int32)
CLASS_EXPERT_A = np.append(CLASS_EXPERT_A, CLASS_EXPERT_A[-1])
CLASS_EXPERT_B = np.array([g * EXP_PER_GROUP + j for g in range(N_GROUPS) for _, j in _PAIRS],
                          np.int32)
CLASS_EXPERT_B = np.append(CLASS_EXPERT_B, CLASS_EXPERT_B[-1])
MIXER_CHUNK = 128
HGRN_MIN_BLOCK = 4
MIXER_TILE = 512
MIXER_PROJ_ROWS = 256
ROUTED_MIN_TOKENS = 8192
ROUTED_TILE = 512
DISPATCH_TILE = 1024
COMBINE_TILE = 512


def _dot(a, b):
    return jnp.dot(a, b, preferred_element_type=F32)


def _dot_nt(a, b):
    return lax.dot_general(a, b, (((1,), (1,)), ((), ())), preferred_element_type=F32)


def _dot_tn(a, b):
    return lax.dot_general(a, b, (((0,), (0,)), ((), ())), preferred_element_type=F32)


def _lanes(parts):
    return parts[0] if len(parts) == 1 else jnp.concatenate(parts, axis=1)


def _block_diag(parts):
    if len(parts) == 1:
        return parts[0]
    rows = []
    for i, p in enumerate(parts):
        rows.append(_lanes([p if j == i else jnp.zeros((p.shape[0], q.shape[1]), p.dtype)
                            for j, q in enumerate(parts)]))
    return jnp.concatenate(rows, axis=0)


def _split3(a):
    a1 = a.astype(BF16)
    r1 = a - a1.astype(F32)
    a2 = r1.astype(BF16)
    a3 = (r1 - a2.astype(F32)).astype(BF16)
    return a1, a2, a3


def _silu(a):
    return a * (1.0 / (1.0 + jnp.exp(-a)))


def _tables_kernel(inv_ref, lbl_ref, cos_ref, sin_ref, rd_ref, idec_ref, kdec_ref, lb_ref,
                   lvl_ref, tri_ref, *, T, C, pos0, min_s):
    pos = (lax.broadcasted_iota(jnp.int32, (T, DH), 0) + pos0).astype(F32)
    ang = pos * inv_ref[...]
    lane = lax.broadcasted_iota(jnp.int32, (T, DH), 1)
    cos_ref[...] = jnp.cos(ang)
    s = jnp.sin(ang)
    sin_ref[...] = jnp.where(lane < DH // 2, -s, s)

    li = lax.broadcasted_iota(jnp.int32, (C, C), 0)
    mi = lax.broadcasted_iota(jnp.int32, (C, C), 1)
    diff = (li - mi).astype(F32)
    rowj = lax.broadcasted_iota(jnp.int32, (C, DH), 0).astype(F32)
    for h in range(HEADS):
        lg = jnp.log(jnp.full((1, 1), 1.0 - 2.0 ** (-5.0 - h), F32))
        rd_ref[h] = jnp.where(diff >= 0, jnp.exp(jnp.maximum(diff, 0.0) * lg), 0.0)
        idec_ref[h] = jnp.exp((rowj + 1.0) * lg)
        kdec_ref[h] = jnp.exp((C - 1.0 - rowj) * lg)

    lg_all = lbl_ref[...]
    m = jnp.max(lg_all, axis=0, keepdims=True)
    e = jnp.exp(lg_all - m)
    lb_ref[...] = e[0:1, :] / jnp.sum(e, axis=0, keepdims=True)

    x = li ^ mi
    code = jnp.zeros((C, C), jnp.int32)
    sz = min_s
    while sz < C:
        code = code + (x >= sz).astype(jnp.int32)
        sz *= 2
    code = jnp.where(x >= min_s, code + (min_s.bit_length() - 1), 0)
    lvl_ref[...] = jnp.where(li >= mi, code, -1)
    tri_ref[...] = (li >= mi).astype(BF16)


def _tables(inv2, lb_logits, *, T, C, pos0, min_s):
    out_shape = (
        jax.ShapeDtypeStruct((T, DH), F32),
        jax.ShapeDtypeStruct((T, DH), F32),
        jax.ShapeDtypeStruct((HEADS, C, C), F32),
        jax.ShapeDtypeStruct((HEADS, C, DH), F32),
        jax.ShapeDtypeStruct((HEADS, C, DH), F32),
        jax.ShapeDtypeStruct((1, HEADS * DH), F32),
        jax.ShapeDtypeStruct((C, C), jnp.int32),
        jax.ShapeDtypeStruct((C, C), BF16),
    )
    return pl.pallas_call(
        functools.partial(_tables_kernel, T=T, C=C, pos0=pos0, min_s=min_s),
        out_shape=out_shape,
        name="tables",
    )(inv2, lb_logits)


def _ada_kernel(c_ref, w_ref, b_ref, o_ref):
    a = _silu(c_ref[...])
    w = w_ref[...]
    a1, a2, _ = _split3(a)
    w1, w2, _ = _split3(w)
    o_ref[...] = _dot(a1, w1) + (_dot(a1, w2) + _dot(a2, w1)) + b_ref[...]


def _ada(c, w, b):
    R = c.shape[0]
    N = w.shape[1]
    TN = 1024
    return pl.pallas_call(
        _ada_kernel,
        grid=(N // TN,),
        in_specs=[pl.BlockSpec((R, D_MODEL), lambda j: (0, 0)),
                  pl.BlockSpec((D_MODEL, TN), lambda j: (0, j)),
                  pl.BlockSpec((1, TN), lambda j: (0, j))],
        out_specs=pl.BlockSpec((R, TN), lambda j: (0, j)),
        out_shape=jax.ShapeDtypeStruct((R, N), F32),
        name="ada",
    )(c, w, b)


def _block_ref(b, block, idx):
    C, W = b.shape
    if block >= 8:
        b3 = b.reshape(C // block, block, W)
        r = jnp.broadcast_to(b3[:, idx:idx + 1, :], b3.shape)
        return r.reshape(C, W)
    row = lax.broadcasted_iota(jnp.int32, (C, W), 0)
    p = row % block
    out = b
    for q in range(block):
        sh = q - idx
        if sh == 0:
            continue
        rolled = pltpu.roll(b, sh % C, 0)
        out = jnp.where(p == q, rolled, out)
    return out


def _head_norm_gate(o, g_row, gate):
    y = o * lax.rsqrt(jnp.mean(o * o, axis=-1, keepdims=True) + EPS)
    return (y * g_row) * _silu(gate)


def _mixer_chunk(r0, proj_scr, cos_ref, sin_ref, rd_ref, idec_ref, kdec_ref, lb_ref, lvl_ref,
                 tri_ref, retg_ref, hgg_ref, sr_scr, sht_scr, o_scr, *, C, min_s):
    W = HEADS * DH
    rows = pl.ds(r0, C)
    proj = proj_scr[rows, :]
    cosf = cos_ref[rows, :]
    sinf = sin_ref[rows, :]

    G = 2 if C % DH == 0 else 1
    groups = [tuple(range(h, h + G)) for h in range(0, HEADS, G)]

    for hs in groups:
        qb_parts, kr_parts, v_parts = [], [], []
        for h in hs:
            c0 = h * DH
            q = proj[:, c0:c0 + DH]
            k = proj[:, W + c0:W + c0 + DH]
            qb_parts.append((q * cosf + pltpu.roll(q, DH // 2, 1) * sinf).astype(BF16))
            kr_parts.append((k * cosf + pltpu.roll(k, DH // 2, 1) * sinf) * (DH ** -0.5))
            v_parts.append(proj[:, 2 * W + c0:2 * W + c0 + DH].astype(BF16))
        qb = _lanes(qb_parts)
        A = (_dot_nt(qb, _block_diag([kr.astype(BF16) for kr in kr_parts]))
             * _lanes([rd_ref[h] for h in hs]))
        S_parts = [sr_scr[h] for h in hs]
        o = (_dot(A.astype(BF16), _block_diag(v_parts))
             + _dot(qb, _block_diag([S.astype(BF16) for S in S_parts]))
             * _lanes([idec_ref[h] for h in hs]))
        for i, h in enumerate(hs):
            c0 = h * DH
            sr_scr[h] = (idec_ref[h][C - 1:C, :] * S_parts[i]
                         + _dot_tn((kr_parts[i] * kdec_ref[h]).astype(BF16), v_parts[i]))
            g = proj[:, 3 * W + c0:3 * W + c0 + DH]
            o_scr[rows, c0:c0 + DH] = _head_norm_gate(
                o[:, i * DH:(i + 1) * DH], retg_ref[:, c0:c0 + DH], g).astype(BF16)

    z = proj[:, 5 * W:6 * W]
    lb = lb_ref[...]
    e = jnp.exp(-jnp.abs(z))
    r = 1.0 / (1.0 + e)
    er = e * r
    sig = jnp.where(z >= 0, r, er)
    nsig = jnp.where(z >= 0, er, r)
    logf = jnp.log(lb + (1.0 - lb) * sig)
    kk_all = (1.0 - lb) * nsig
    tri = tri_ref[...]
    l1, l2, _ = _split3(logf)
    b_all = _dot(tri, l1) + _dot(tri, l2)

    GW = G * DH
    lvl = _lanes([lvl_ref[...]] * G)
    rowc = lax.broadcasted_iota(jnp.int32, (C, GW), 0)
    levels = []
    s = C // 2
    while s >= min_s:
        isq = (rowc % (2 * s)) >= s
        levels.append((s, isq, jnp.where(isq, 1.0, -1.0)))
        s //= 2

    def heads_of(a):
        return [a[:, i * DH:(i + 1) * DH] for i in range(G)]

    for hs in groups:
        c0 = hs[0] * DH
        q = proj[:, 4 * W + c0:4 * W + c0 + GW] * (DH ** -0.5)
        kk = kk_all[:, c0:c0 + GW]
        b = b_all[:, c0:c0 + GW]
        v_parts = heads_of(proj[:, 6 * W + c0:6 * W + c0 + GW])
        vb_parts = [v.astype(BF16) for v in v_parts]

        A = jnp.zeros((C, G * C), F32)
        for s, isq, sign in levels:
            ref = _block_ref(b, 2 * s, s - 1)
            zz = jnp.where(isq, q, kk) * jnp.exp((b - ref) * sign)
            zb = zz.astype(BF16)
            A = jnp.where(lvl == s.bit_length(), _dot_nt(zb, _block_diag(heads_of(zb))), A)
        if min_s > 1:
            ref = _block_ref(b, min_s, min_s // 2)
            d = b - ref
            P = _dot_nt((q * jnp.exp(d)).astype(BF16),
                        _block_diag(heads_of((kk * jnp.exp(-d)).astype(BF16))))
            A = jnp.where(lvl == 0, P, A)
            o = _dot(A.astype(BF16), _block_diag(vb_parts))
        else:
            diag = [jnp.sum(qh * kh, axis=-1, keepdims=True) * v
                    for qh, kh, v in zip(heads_of(q), heads_of(kk), v_parts)]
            o = _dot(A.astype(BF16), _block_diag(vb_parts)) + _lanes(diag)

        ST_parts = [sht_scr[h] for h in hs]
        bl = b[C - 1:C, :]
        o = o + _dot_nt((q * jnp.exp(b)).astype(BF16),
                        _block_diag([ST.astype(BF16) for ST in ST_parts]))
        kdec_parts = heads_of((kk * jnp.exp(bl - b)).astype(BF16))
        ebl_parts = heads_of(jnp.exp(bl))
        for i, h in enumerate(hs):
            ch = h * DH
            sht_scr[h] = ST_parts[i] * ebl_parts[i] + _dot_tn(vb_parts[i], kdec_parts[i])
            g = proj[:, 7 * W + ch:7 * W + ch + DH]
            o_scr[rows, W + ch:W + ch + DH] = _head_norm_gate(
                o[:, i * DH:(i + 1) * DH], hgg_ref[:, ch:ch + DH], g).astype(BF16)


def _mixer_kernel(x_ref, xnext_ref, mod_ref, modnext_ref, n1g_ref, win_ref, wout_ref, cos_ref,
                  sin_ref, rd_ref, idec_ref, kdec_ref, lb_ref, lvl_ref, tri_ref, retg_ref, hgg_ref,
                  sr0_ref, sh0_ref, n2g_ref, wr_ref, br_ref,
                  x1_ref, h2_ref, cls_ref, sro_ref, sho_ref,
                  sr_scr, sht_scr, proj_scr, o_scr, *, C, min_s, with_cls, has_next):
    t = pl.program_id(1)
    nt = pl.num_programs(1)
    TT = x_ref.shape[1]
    n_chunks = TT // C

    mod = mod_ref[0]
    gt1 = mod[:, 2 * D_MODEL:3 * D_MODEL]

    PR = xnext_ref.shape[1]
    per_block = PR // C

    def project(xc, m, r0):
        xn = xc * lax.rsqrt(jnp.mean(xc * xc, axis=-1, keepdims=True) + EPS) * n1g_ref[...]
        hmod = xn * (1.0 + m[:, D_MODEL:2 * D_MODEL]) + m[:, 0:D_MODEL]
        proj_scr[r0:r0 + PR, :] = _dot(hmod.astype(BF16), win_ref[...])

    @pl.when(t == 0)
    def _():
        for h in range(HEADS):
            sr_scr[h] = sr0_ref[0, h]
            sht_scr[h] = sh0_ref[0, h].T

    first = (t == 0) & (pl.program_id(0) == 0) if has_next else t == 0

    @pl.when(first)
    def _():
        project(x_ref[0, 0:PR, :], mod, 0)

    chunk = functools.partial(
        _mixer_chunk, proj_scr=proj_scr, cos_ref=cos_ref, sin_ref=sin_ref, rd_ref=rd_ref,
        idec_ref=idec_ref, kdec_ref=kdec_ref, lb_ref=lb_ref, lvl_ref=lvl_ref, tri_ref=tri_ref,
        retg_ref=retg_ref, hgg_ref=hgg_ref, sr_scr=sr_scr, sht_scr=sht_scr, o_scr=o_scr,
        C=C, min_s=min_s)
    for c in range(n_chunks):
        chunk(c * C)
        if c % per_block == 0:
            r1 = (c // per_block + 1) * PR
            if r1 < TT:
                project(x_ref[0, r1:r1 + PR, :], mod, r1)
            elif has_next:
                project(xnext_ref[0], modnext_ref[0], 0)

    x = x_ref[0]
    mix = _dot(o_scr[...], wout_ref[...])
    x1 = x + gt1 * mix
    x1_ref[0] = x1

    sh2 = mod[:, 3 * D_MODEL:4 * D_MODEL]
    sc2 = mod[:, 4 * D_MODEL:5 * D_MODEL]
    x1n = x1 * lax.rsqrt(jnp.mean(x1 * x1, axis=-1, keepdims=True) + EPS) * n2g_ref[...]
    h2 = x1n * (1.0 + sc2) + sh2
    h2_ref[0] = h2
    if with_cls:
        g_idx, i1, i2 = _route_topk(_router_logits(h2, wr_ref, br_ref).T)[:3]
        lo = jnp.minimum(i1, i2)
        hi = jnp.maximum(i1, i2)
        pair = jnp.where(lo == 0.0, hi - 1.0, jnp.where(lo == 1.0, 6.0 - hi, 5.0))
        cls_ref[0] = (g_idx * float(N_PAIRS) + pair).astype(jnp.int32)
    else:
        cls_ref[0] = jnp.zeros((1, TT), jnp.int32)

    @pl.when(t == nt - 1)
    def _():
        for h in range(HEADS):
            sro_ref[0, h] = sr_scr[h]
            sho_ref[0, h] = sht_scr[h].T


def _mixer(x, mod, n1g, win, wout, tabs, retg, hgg, sr0, sh0, n2g, wr, br, *, C, TT, min_s,
           with_cls):
    B, T, D = x.shape
    cosf, sinf, rd, idec, kdec, lb, lvl, tri = tabs
    nt = T // TT
    const2 = lambda b, t: (0, 0)
    const3 = lambda b, t: (0, 0, 0)
    state_spec = pl.BlockSpec((1, HEADS, DH, DH), lambda b, t: (b, 0, 0, 0))
    PR = min(TT, MIXER_PROJ_ROWS)
    per_tile = TT // PR
    n_blocks = T // PR

    def next_seq(b, t):
        return jnp.minimum(b + ((t + 1) * per_tile) // n_blocks, B - 1)

    in_specs = [
        pl.BlockSpec((1, TT, D), lambda b, t: (b, t, 0)),
        pl.BlockSpec((1, PR, D),
                     lambda b, t: (next_seq(b, t), ((t + 1) * per_tile) % n_blocks, 0)),
        pl.BlockSpec((1, 1, 6 * D), lambda b, t: (b, 0, 0)),
        pl.BlockSpec((1, 1, 6 * D), lambda b, t: (next_seq(b, t), 0, 0)),
        pl.BlockSpec((1, D), const2),
        pl.BlockSpec((D, D_IN), const2),
        pl.BlockSpec((D, D), const2),
        pl.BlockSpec((TT, DH), lambda b, t: (t, 0)),
        pl.BlockSpec((TT, DH), lambda b, t: (t, 0)),
        pl.BlockSpec((HEADS, C, C), const3),
        pl.BlockSpec((HEADS, C, DH), const3),
        pl.BlockSpec((HEADS, C, DH), const3),
        pl.BlockSpec((1, HEADS * DH), const2),
        pl.BlockSpec((C, C), const2),
        pl.BlockSpec((C, C), const2),
        pl.BlockSpec((1, HEADS * DH), const2),
        pl.BlockSpec((1, HEADS * DH), const2),
        state_spec, state_spec,
        pl.BlockSpec((1, D), const2),
        pl.BlockSpec((D, 2 * ROUTE_LANES), const2),
        pl.BlockSpec((1, ROUTE_LANES), const2),
    ]
    tok_spec = pl.BlockSpec((1, TT, D), lambda b, t: (b, t, 0))
    out_specs = [tok_spec, tok_spec,
                 pl.BlockSpec((1, 1, TT), lambda b, t: (b * nt + t, 0, 0)),
                 state_spec, state_spec]
    out_shape = [jax.ShapeDtypeStruct((B, T, D), F32),
                 jax.ShapeDtypeStruct((B, T, D), F32),
                 jax.ShapeDtypeStruct((B * nt, 1, TT), jnp.int32),
                 jax.ShapeDtypeStruct((B, HEADS, DH, DH), F32),
                 jax.ShapeDtypeStruct((B, HEADS, DH, DH), F32)]
    return pl.pallas_call(
        functools.partial(_mixer_kernel, C=C, min_s=min_s, with_cls=with_cls, has_next=nt > 1),
        grid=(B, nt),
        in_specs=in_specs,
        out_specs=out_specs,
        out_shape=out_shape,
        scratch_shapes=[pltpu.VMEM((HEADS, DH, DH), F32),
                        pltpu.VMEM((HEADS, DH, DH), F32),
                        pltpu.VMEM((TT, D_IN), F32),
                        pltpu.VMEM((TT, D), BF16)],
        compiler_params=pltpu.CompilerParams(
            dimension_semantics=("arbitrary", "arbitrary"),
            vmem_limit_bytes=V7X_VMEM_LIMIT),
        name="mixer",
    )(x, x, mod, mod, n1g, win, wout, cosf, sinf, rd, idec, kdec, lb, lvl, tri, retg, hgg, sr0, sh0,
      n2g, wr, br)


def _expert_w(w_ref):
    return w_ref[0].astype(BF16)


def _router_logits(h, wr_ref, br_ref):
    h1, h2, _ = _split3(h)
    hi = _dot(h1, wr_ref[...])
    lo = _dot(h2, wr_ref[:, 0:ROUTE_LANES])
    return hi[:, 0:ROUTE_LANES] + (hi[:, ROUTE_LANES:] + lo) + br_ref[...]


def _route_topk(logits_t):
    TM = logits_t.shape[1]
    lg = logits_t[0:N_GROUPS, :]
    gi = lax.broadcasted_iota(jnp.int32, (N_GROUPS, TM), 0).astype(F32)
    m = jnp.max(lg, axis=0, keepdims=True)
    g_w = 1.0 / jnp.sum(jnp.exp(lg - m), axis=0, keepdims=True)
    g_idx = jnp.min(jnp.where(lg == m, gi, float(N_GROUPS)), axis=0, keepdims=True)
    el = jnp.zeros((EXP_PER_GROUP, TM), F32)
    for g in range(N_GROUPS):
        lo = N_GROUPS + g * EXP_PER_GROUP
        el = jnp.where(g_idx == float(g), logits_t[lo:lo + EXP_PER_GROUP, :], el)
    ei = lax.broadcasted_iota(jnp.int32, (EXP_PER_GROUP, TM), 0).astype(F32)
    m1 = jnp.max(el, axis=0, keepdims=True)
    i1 = jnp.min(jnp.where(el == m1, ei, float(EXP_PER_GROUP)), axis=0, keepdims=True)
    el2 = jnp.where(ei == i1, -jnp.inf, el)
    m2 = jnp.max(el2, axis=0, keepdims=True)
    i2 = jnp.min(jnp.where(el2 == m2, ei, float(EXP_PER_GROUP)), axis=0, keepdims=True)
    p2 = jnp.exp(m2 - m1)
    den = 1.0 / (1.0 + p2)
    w0 = g_w * den
    w1 = g_w * (p2 * den)
    return g_idx, i1, i2, w0, w1


def _moe_kernel(x1_ref, h2_ref, mod_ref, wr_ref, br_ref, wg_ref, wu_ref, wd_ref, fg_ref,
                y_ref, h_scr, route_scr, acc_scr):
    e = pl.program_id(1)
    ne = pl.num_programs(1)
    TM, D = x1_ref.shape
    K = mod_ref.shape[0]
    TQ = TM // K

    @pl.when(e == 0)
    def _():
        h = h2_ref[...]
        h_scr[...] = h.astype(BF16)
        g_idx, i1, i2, w0, wgt1 = _route_topk(_router_logits(h, wr_ref, br_ref).T)
        e0 = g_idx * EXP_PER_GROUP + i1
        e1 = g_idx * EXP_PER_GROUP + i2
        ri = lax.broadcasted_iota(jnp.int32, (ROUTE_LANES, TM), 0)
        rows = jnp.where(ri == 0, e0, jnp.where(ri == 1, e1, jnp.where(ri == 2, w0,
                         jnp.where(ri == 3, wgt1, 0.0))))
        route_scr[...] = rows.T
        acc_scr[...] = jnp.zeros_like(acc_scr)

    hb = h_scr[...]
    hid = _silu(_dot(hb, _expert_w(wg_ref))) * _dot(hb, _expert_w(wu_ref))
    out = _dot(hid.astype(BF16), _expert_w(wd_ref))
    ef = e.astype(F32)
    route = route_scr[...]
    comb = (jnp.where(route[:, 0:1] == ef, route[:, 2:3], 0.0)
            + jnp.where(route[:, 1:2] == ef, route[:, 3:4], 0.0))
    acc_scr[...] += comb * out

    @pl.when(e == ne - 1)
    def _():
        gt2 = mod_ref[...][:, :, 5 * D:6 * D]
        ff = (gt2 * acc_scr[...].reshape(K, TQ, D)).reshape(TM, D)
        x2 = x1_ref[...] + ff
        y_ref[...] = x2 * lax.rsqrt(jnp.mean(x2 * x2, axis=-1, keepdims=True) + EPS) * fg_ref[...]


def _moe(x1, h2, mod, wr, br, wg, wu, wd, fg, *, T, TM):
    N, D = x1.shape
    if T >= TM:
        assert T % TM == 0
        seqs, tiles_per_seq = 1, T // TM
        mod_map = lambda i, e: (i // tiles_per_seq, 0, 0)
    else:
        assert TM % T == 0
        seqs = TM // T
        mod_map = lambda i, e: (i, 0, 0)
    return pl.pallas_call(
        _moe_kernel,
        grid=(N // TM, N_EXPERTS),
        in_specs=[
            pl.BlockSpec((TM, D), lambda i, e: (i, 0)),
            pl.BlockSpec((TM, D), lambda i, e: (i, 0)),
            pl.BlockSpec((seqs, 1, 6 * D), mod_map),
            pl.BlockSpec((D, 2 * ROUTE_LANES), lambda i, e: (0, 0)),
            pl.BlockSpec((1, ROUTE_LANES), lambda i, e: (0, 0)),
            pl.BlockSpec((1, D, D_EXPERT), lambda i, e: (e, 0, 0)),
            pl.BlockSpec((1, D, D_EXPERT), lambda i, e: (e, 0, 0)),
            pl.BlockSpec((1, D_EXPERT, D), lambda i, e: (e, 0, 0)),
            pl.BlockSpec((1, D), lambda i, e: (0, 0)),
        ],
        out_specs=pl.BlockSpec((TM, D), lambda i, e: (i, 0)),
        out_shape=jax.ShapeDtypeStruct((N, D), F32),
        scratch_shapes=[pltpu.VMEM((TM, D), BF16),
                        pltpu.VMEM((TM, ROUTE_LANES), F32),
                        pltpu.VMEM((TM, D), F32)],
        compiler_params=pltpu.CompilerParams(
            dimension_semantics=("arbitrary", "arbitrary"),
            vmem_limit_bytes=V7X_VMEM_LIMIT),
        name="moe",
    )(x1, h2, mod, wr, br, wg, wu, wd, fg)


INFO_TAIL_LEN = 32
INFO_END = 64


def _rank_kernel(cls_ref, pos_ref, tcls_ref, info_ref, *, TG):
    R, L = cls_ref.shape
    info_lane = lax.broadcasted_iota(jnp.int32, info_ref.shape, 1)
    info = jnp.zeros(info_ref.shape, F32)
    cf = cls_ref[...].astype(F32)
    ri = lax.broadcasted_iota(jnp.int32, (L, L), 0)
    ci = lax.broadcasted_iota(jnp.int32, (L, L), 1)
    upper = (ri < ci).astype(BF16)
    rr = lax.broadcasted_iota(jnp.int32, (R, R), 0)
    rc = lax.broadcasted_iota(jnp.int32, (R, R), 1)
    lower = (rc < rr).astype(BF16)
    tile_start = lax.broadcasted_iota(jnp.int32, tcls_ref.shape, 1).astype(F32) * float(TG)
    base = jnp.zeros((1, 1), F32)
    pos = jnp.zeros((R, L), F32)
    tcls = jnp.zeros(tcls_ref.shape, F32)
    for c in range(N_CLASSES):
        ind = jnp.where(cf == float(c), 1.0, 0.0)
        lane_pre = _dot(ind.astype(BF16), upper)
        row_tot = jnp.broadcast_to(jnp.sum(ind, axis=1, keepdims=True), (R, L))
        row_pre = _dot(lower, row_tot.astype(BF16))
        total = jnp.sum(row_tot[:, 0:1], axis=0, keepdims=True)
        pos = jnp.where(ind > 0.0, base + row_pre + lane_pre, pos)
        padded = jnp.floor((total + float(TG - 1)) * (1.0 / TG)) * float(TG)
        info = jnp.where(info_lane == c, base + total, info)
        info = jnp.where(info_lane == INFO_TAIL_LEN + c, padded - total, info)
        base = base + padded
        tcls = tcls + jnp.where(tile_start >= base, 1.0, 0.0)
    info = jnp.where(info_lane == INFO_END, base, info)
    pos_ref[...] = pos.astype(jnp.int32)
    tcls_ref[...] = tcls.astype(jnp.int32)
    info_ref[...] = info.astype(jnp.int32)


def _rank(cls2d, *, TG, n_tiles):
    R, L = cls2d.shape
    lanes = -(-n_tiles // 128) * 128
    return pl.pallas_call(
        functools.partial(_rank_kernel, TG=TG),
        out_shape=(jax.ShapeDtypeStruct((R, L), jnp.int32),
                   jax.ShapeDtypeStruct((1, lanes), jnp.int32),
                   jax.ShapeDtypeStruct((1, 128), jnp.int32)),
        name="rank",
    )(cls2d)


SUBLANES = 8
N_DMA_PRIORITIES = 2


def _for_each_row(n_rows, fn):
    def group(g, carry):
        r0 = pl.multiple_of(g * SUBLANES, SUBLANES)
        for j in range(SUBLANES):
            fn(pl.ds(r0, SUBLANES), j, r0 + j)
        return carry
    lax.fori_loop(0, n_rows // SUBLANES, group, 0)


def _dispatch_kernel(info_ref, pos_ref, h2_hbm, xs_hbm, zero_scr, tiles, zsem, load_sems,
                     row_sems, *, TD):
    ZR = zero_scr.shape[0]

    def tail_copies(do):
        for c in range(N_CLASSES):
            start = info_ref[c]
            head = (-start) & (SUBLANES - 1)
            for i in range(SUBLANES - 1):
                @pl.when(i < head)
                def _(i=i):
                    do(pltpu.make_async_copy(zero_scr.at[pl.ds(0, 1)],
                                             xs_hbm.at[pl.ds(start + i, 1)], zsem))
            rest = info_ref[INFO_TAIL_LEN + c] - head
            off = start + head
            k = ZR
            while k >= SUBLANES:
                @pl.when((rest & k) != 0)
                def _(k=k, off=off):
                    do(pltpu.make_async_copy(
                        zero_scr.at[pl.ds(0, k)],
                        xs_hbm.at[pl.ds(pl.multiple_of(off, SUBLANES), k)], zsem))
                off = off + (rest & k)
                k //= 2

    def unused_copies(do):
        first = info_ref[INFO_END] // ZR

        def body(i, carry):
            do(pltpu.make_async_copy(
                zero_scr, xs_hbm.at[pl.ds(pl.multiple_of(i * ZR, ZR), ZR)], zsem))
            return carry
        lax.fori_loop(first, xs_hbm.shape[0] // ZR, body, 0)

    @pl.when(pl.program_id(0) == 0)
    def _():
        zero_scr[...] = jnp.zeros_like(zero_scr)
        tail_copies(lambda cp: cp.start())
        unused_copies(lambda cp: cp.start())
        tail_copies(lambda cp: cp.wait())
        unused_copies(lambda cp: cp.wait())

    i = pl.program_id(0)
    n = pl.num_programs(0)

    def load(tile, do):
        s = tile % 3
        do(pltpu.make_async_copy(h2_hbm.at[pl.ds(pl.multiple_of(tile * TD, TD), TD)],
                                 tiles.at[s], load_sems.at[s]))

    def rows_wait(tile):
        pltpu.make_async_copy(tiles.at[0], xs_hbm.at[pl.ds(0, TD)], row_sems.at[tile % 2]).wait()

    @pl.when(i == 0)
    def _():
        load(0, lambda cp: cp.start())

    @pl.when(i + 1 < n)
    def _():
        load(i + 1, lambda cp: cp.start())

    load(i, lambda cp: cp.wait())
    slot = i % 3

    def issue(tile, j, r):
        pltpu.make_async_copy(tiles.at[slot].at[tile].at[pl.ds(j, 1)],
                              xs_hbm.at[pl.ds(pos_ref[r], 1)],
                              row_sems.at[i % 2]).start(priority=j % N_DMA_PRIORITIES)

    _for_each_row(TD, issue)

    @pl.when(i > 0)
    def _():
        rows_wait(i - 1)

    @pl.when(i == n - 1)
    def _():
        rows_wait(i)


def _dispatch(info, pos, h2, *, n_rows, TG, TD):
    N, D = h2.shape
    grid_spec = pltpu.PrefetchScalarGridSpec(
        num_scalar_prefetch=1,
        grid=(N // TD,),
        in_specs=[pl.BlockSpec((TD,), lambda i, info: (i,), memory_space=pltpu.SMEM),
                  pl.BlockSpec(memory_space=pl.ANY)],
        out_specs=pl.BlockSpec(memory_space=pl.ANY),
        scratch_shapes=[pltpu.VMEM((TG // 2, D), F32), pltpu.VMEM((3, TD, D), F32),
                        pltpu.SemaphoreType.DMA(()), pltpu.SemaphoreType.DMA((3,)),
                        pltpu.SemaphoreType.DMA((2,))],
    )
    return pl.pallas_call(
        functools.partial(_dispatch_kernel, TD=TD),
        grid_spec=grid_spec,
        out_shape=jax.ShapeDtypeStruct((n_rows, D), F32),
        compiler_params=pltpu.CompilerParams(
            dimension_semantics=("arbitrary",), vmem_limit_bytes=V7X_VMEM_LIMIT),
        name="dispatch",
    )(info, pos, h2)


def _expert_kernel(ea_ref, eb_ref, nv_ref, xs_ref, wr_ref, br_ref, wga_ref, wua_ref, wda_ref,
                   wgb_ref, wub_ref, wdb_ref, ys_ref):
    j = pl.program_id(0)
    TG = xs_ref.shape[0]
    n_valid = nv_ref[0]

    @pl.when(j < n_valid)
    def _():
        a = ea_ref[j]
        b = eb_ref[j]
        g = lax.div(a, jnp.int32(EXP_PER_GROUP))
        hb = xs_ref[...].astype(BF16)
        logits = _dot(hb, wr_ref[:, 0:ROUTE_LANES]) + br_ref[...]
        lane = lax.broadcasted_iota(jnp.int32, (TG, ROUTE_LANES), 1)
        isg = lane < N_GROUPS
        m = jnp.max(jnp.where(isg, logits, -jnp.inf), axis=1, keepdims=True)
        ssum = jnp.sum(jnp.where(isg, jnp.exp(logits - m), 0.0), axis=1, keepdims=True)
        lg = jnp.sum(jnp.where(lane == g, logits, 0.0), axis=1, keepdims=True)
        la = jnp.sum(jnp.where(lane == N_GROUPS + a, logits, 0.0), axis=1, keepdims=True)
        lb = jnp.sum(jnp.where(lane == N_GROUPS + b, logits, 0.0), axis=1, keepdims=True)
        g_w = jnp.exp(lg - m) / ssum
        mm = jnp.maximum(la, lb)
        pa = jnp.exp(la - mm)
        pb = jnp.exp(lb - mm)
        den = g_w / (pa + pb)
        hid_a = _silu(_dot(hb, _expert_w(wga_ref))) * _dot(hb, _expert_w(wua_ref))
        out = (pa * den) * _dot(hid_a.astype(BF16), _expert_w(wda_ref))
        hid_b = _silu(_dot(hb, _expert_w(wgb_ref))) * _dot(hb, _expert_w(wub_ref))
        ys_ref[...] = out + (pb * den) * _dot(hid_b.astype(BF16), _expert_w(wdb_ref))

    @pl.when(j >= n_valid)
    def _():
        ys_ref[...] = jnp.zeros_like(ys_ref)


def _experts(ea, eb, n_valid, xs, wr, br, wg, wu, wd, *, TG):
    NS, D = xs.shape
    const2 = lambda j, ea, eb, nv: (0, 0)
    wmap_a = lambda j, ea, eb, nv: (ea[j], 0, 0)
    wmap_b = lambda j, ea, eb, nv: (eb[j], 0, 0)
    grid_spec = pltpu.PrefetchScalarGridSpec(
        num_scalar_prefetch=3,
        grid=(NS // TG,),
        in_specs=[
            pl.BlockSpec((TG, D), lambda j, ea, eb, nv: (jnp.minimum(j, nv[0] - 1), 0)),
            pl.BlockSpec((D, 2 * ROUTE_LANES), const2),
            pl.BlockSpec((1, ROUTE_LANES), const2),
            pl.BlockSpec((1, D, D_EXPERT), wmap_a),
            pl.BlockSpec((1, D, D_EXPERT), wmap_a),
            pl.BlockSpec((1, D_EXPERT, D), wmap_a),
            pl.BlockSpec((1, D, D_EXPERT), wmap_b),
            pl.BlockSpec((1, D, D_EXPERT), wmap_b),
            pl.BlockSpec((1, D_EXPERT, D), wmap_b),
        ],
        out_specs=pl.BlockSpec((TG, D), lambda j, ea, eb, nv: (j, 0)),
    )
    return pl.pallas_call(
        _expert_kernel,
        grid_spec=grid_spec,
        out_shape=jax.ShapeDtypeStruct((NS, D), F32),
        compiler_params=pltpu.CompilerParams(
            dimension_semantics=("arbitrary",), vmem_limit_bytes=V7X_VMEM_LIMIT),
        name="experts",
    )(ea, eb, n_valid, xs, wr, br, wg, wu, wd, wg, wu, wd)


def _combine_kernel(pos_ref, posn_ref, x1_ref, mod_ref, fg_ref, ys_hbm, y_ref, buf, sems, *, TC):
    i = pl.program_id(0)
    n = pl.num_programs(0)
    slot = i % 2

    def gather(p_ref, s):
        def issue(tile, j, r):
            pltpu.make_async_copy(ys_hbm.at[pl.ds(p_ref[r], 1)],
                                  buf.at[s].at[tile].at[pl.ds(j, 1)],
                                  sems.at[s]).start(priority=j % N_DMA_PRIORITIES)
        _for_each_row(TC, issue)

    @pl.when(i == 0)
    def _():
        gather(pos_ref, 0)

    @pl.when(i + 1 < n)
    def _():
        gather(posn_ref, 1 - slot)

    pltpu.make_async_copy(ys_hbm.at[pl.ds(0, TC)], buf.at[slot], sems.at[slot]).wait()
    gt2 = mod_ref[0][:, 5 * D_MODEL:6 * D_MODEL]
    x2 = x1_ref[...] + gt2 * buf[slot]
    y_ref[...] = x2 * lax.rsqrt(jnp.mean(x2 * x2, axis=-1, keepdims=True) + EPS) * fg_ref[...]


def _combine(pos, x1, mod, fg, ys, *, T, TC):
    N, D = x1.shape
    n = N // TC
    tiles_per_seq = T // TC
    return pl.pallas_call(
        functools.partial(_combine_kernel, TC=TC),
        grid=(n,),
        in_specs=[pl.BlockSpec((TC,), lambda i: (i,), memory_space=pltpu.SMEM),
                  pl.BlockSpec((TC,), lambda i: (jnp.minimum(i + 1, n - 1),),
                               memory_space=pltpu.SMEM),
                  pl.BlockSpec((TC, D), lambda i: (i, 0)),
                  pl.BlockSpec((1, 1, 6 * D), lambda i: (i // tiles_per_seq, 0, 0)),
                  pl.BlockSpec((1, D), lambda i: (0, 0)),
                  pl.BlockSpec(memory_space=pl.ANY)],
        out_specs=pl.BlockSpec((TC, D), lambda i: (i, 0)),
        out_shape=jax.ShapeDtypeStruct((N, D), F32),
        scratch_shapes=[pltpu.VMEM((2, TC, D), F32), pltpu.SemaphoreType.DMA((2,))],
        compiler_params=pltpu.CompilerParams(
            dimension_semantics=("arbitrary",), vmem_limit_bytes=V7X_VMEM_LIMIT),
        name="combine",
    )(pos, pos, x1, mod, fg, ys)


def _routed_moe(x1, h2, cls, mod, p, *, T):
    N, D = x1.shape
    TG = ROUTED_TILE
    n_tiles = N // TG + N_CLASSES
    pos2d, tcls, info = _rank(cls.reshape(N // 128, 128), TG=TG, n_tiles=n_tiles)
    pos = pos2d.reshape(N)
    tcls = tcls[0, :n_tiles]
    ea = jnp.asarray(CLASS_EXPERT_A)[tcls]
    eb = jnp.asarray(CLASS_EXPERT_B)[tcls]
    n_valid = (info[0, INFO_END:INFO_END + 1] // TG).astype(jnp.int32)
    xs = _dispatch(info.reshape(-1), pos, h2, n_rows=n_tiles * TG, TG=TG, TD=DISPATCH_TILE)
    ys = _experts(ea, eb, n_valid, xs, p["wr"], p["br"], p["wg"], p["wu"], p["wd"], TG=TG)
    return _combine(pos, x1, mod, p["fg"], ys, T=T, TC=COMBINE_TILE)


def _tiling(T):
    if T <= REF_CHUNK:
        return T, T
    return MIXER_CHUNK, min(T, MIXER_TILE)


def _trunk(x, mod, s_ret, s_hg, pos0, inv2, p, *, min_s=HGRN_MIN_BLOCK):
    B, T, D = x.shape
    N = B * T
    C, TT = _tiling(T)
    min_s = min(min_s, C // 2)
    routed = N >= ROUTED_MIN_TOKENS
    tabs = _tables(inv2, p["lb_logits"], T=T, C=C, pos0=pos0, min_s=min_s)
    mod3 = mod.reshape(B, 1, 6 * D)
    x1, h2, cls, sr, sh = _mixer(x, mod3, p["n1g"], p["win"], p["wout"], tabs, p["retg"], p["hgg"],
                                 s_ret, s_hg, p["n2g"], p["wr"], p["br"], C=C, TT=TT,
                                 min_s=min_s, with_cls=routed)
    x1 = x1.reshape(N, D)
    h2 = h2.reshape(N, D)
    if routed:
        y = _routed_moe(x1, h2, cls, mod3, p, T=T)
    else:
        y = _moe(x1, h2, mod3, p["wr"], p["br"], p["wg"], p["wu"], p["wd"], p["fg"],
                 T=T, TM=min(N, 1024))
    return y.reshape(B, T, D), sr[None], sh[None]


def _prepare(W_ada, b_ada, norm1_g, norm2_g, W_in, ret_norm_g, hgrn_norm_g, hgrn_lb_logits, W_out,
             W_router_group, b_router_group, W_router_expert, b_router_expert, W_gate_e, W_up_e,
             W_down_e, final_norm_g):
    assert W_in.shape[0] == 1, "single-layer stack only"
    D = D_MODEL
    wr = jnp.concatenate([W_router_group[0], W_router_expert[0]], axis=1)
    wr = jnp.pad(wr, ((0, 0), (0, ROUTE_LANES - wr.shape[1])))
    wr1 = wr.astype(BF16)
    wr2 = (wr - wr1.astype(F32)).astype(BF16)
    br = jnp.concatenate([b_router_group[0], b_router_expert[0]])
    br = jnp.pad(br, (0, ROUTE_LANES - br.shape[0])).reshape(1, ROUTE_LANES)
    return dict(
        n1g=norm1_g[0].reshape(1, D), n2g=norm2_g[0].reshape(1, D), fg=final_norm_g.reshape(1, D),
        win=W_in[0].astype(BF16), wout=W_out[0].astype(BF16),
        retg=ret_norm_g[0].reshape(1, -1), hgg=hgrn_norm_g[0].reshape(1, -1),
        lb_logits=hgrn_lb_logits,
        wr=jnp.concatenate([wr1, wr2], axis=1), br=br,
        wg=W_gate_e[0], wu=W_up_e[0], wd=W_down_e[0],
    )


def _rope_inv():
    half = DH // 2
    inv = 1.0 / (ROPE_BASE ** (jnp.arange(half, dtype=F32) / half))
    return jnp.concatenate([inv, inv]).reshape(1, DH)


def kernel(x_prompt, x_sample, c_prompt, c_sample, state_ret, state_hgrn, W_ada, b_ada, norm1_g,
           norm2_g, W_in, ret_norm_g, hgrn_norm_g, hgrn_lb_logits, W_out, W_router_group,
           b_router_group, W_router_expert, b_router_expert, W_gate_e, W_up_e, W_down_e,
           final_norm_g):
    p = _prepare(W_ada, b_ada, norm1_g, norm2_g, W_in, ret_norm_g, hgrn_norm_g, hgrn_lb_logits,
                 W_out, W_router_group, b_router_group, W_router_expert, b_router_expert,
                 W_gate_e, W_up_e, W_down_e, final_norm_g)
    Bp = x_prompt.shape[0]
    Bs = x_sample.shape[0]
    inv2 = _rope_inv()
    mod = _ada(jnp.concatenate([c_prompt, c_sample], axis=0), W_ada[0], b_ada[0].reshape(1, -1))
    zeros = jnp.zeros((Bp, HEADS, DH, DH), F32)
    y_p, sr_p, sh_p = _trunk(x_prompt, mod[:Bp], zeros, zeros, 0, inv2, p)
    y_s, sr_s, sh_s = _trunk(x_sample, mod[Bp:Bp + Bs], state_ret[0], state_hgrn[0], PAST_LEN,
                             inv2, p)
    return (y_p, y_s, sr_p, sh_p, sr_s, sh_s)
```
